```python
import jax
import jax.numpy as jnp
from jax import lax
import numpy as np

D_MODEL = 1024
BATCH = 8
SEQ = 4096
DEPTH = 2

GRID_W = 64
CTX_LEN = 256
HEAD_DIM = 64
N_HEADS_A = 8
N_KV_A = 2
N_HEADS_B = 8
N_KV_B = 2
N_HEADS_N = 8
BRANCH_WIDTH = 8 * HEAD_DIM
N_BRANCHES = 3
Q_BLOCK = 128
WINDOW = 128
NA_ROWS = 8
NA_COLS = 16
ROPE_THETA = 10000.0
N_EXPERTS = 16
N_GROUPS = 4
EXPERTS_PER_GROUP = N_EXPERTS // N_GROUPS
TOP_K = 2
D_FF_EXPERT = 512
EPS = 1e-6
NEG_INF = -1e30
ATTN_SCALE = HEAD_DIM ** -0.5

IN_SPLITS = (N_HEADS_A * HEAD_DIM, N_KV_A * HEAD_DIM, N_KV_A * HEAD_DIM,
             N_HEADS_B * HEAD_DIM, N_KV_B * HEAD_DIM, N_KV_B * HEAD_DIM,
             N_HEADS_N * HEAD_DIM, N_HEADS_N * HEAD_DIM, N_HEADS_N * HEAD_DIM,
             N_BRANCHES * D_MODEL)
IN_COLS = sum(IN_SPLITS)
IN_OFFSETS = tuple(int(o) for o in np.cumsum(IN_SPLITS)[:-1])

kernel_name = 'hybrid_dit_parallel_attn_moe'


def _rmsnorm(x, g):
    xf = x.astype(jnp.float32)
    y = xf * lax.rsqrt(jnp.mean(xf * xf, axis=-1, keepdims=True) + EPS)
    return (y * g.astype(jnp.float32)).astype(x.dtype)


def _modulate(x, g, shift, scale):
    return _rmsnorm(x, g) * (1 + scale) + shift


def _heads(x, n):
    return x.reshape(*x.shape[:-1], n, HEAD_DIM)


def _rope_tables(S):
    t = jnp.arange(S, dtype=jnp.int32)
    row = (t // GRID_W).astype(jnp.float32)
    col = (t % GRID_W).astype(jnp.float32)
    half = HEAD_DIM // 2
    inv = ROPE_THETA ** (-jnp.arange(0, half, 2, dtype=jnp.float32) / half)
    ang = jnp.stack([row[:, None] * inv, col[:, None] * inv], axis=1)
    return jnp.cos(ang), jnp.sin(ang)


def _rope_2d(x, cos, sin):
    xr = x.reshape(*x.shape[:-1], 2, 2, HEAD_DIM // 4)
    x1, x2 = xr[..., 0, :], xr[..., 1, :]
    cs = cos[None, :, None].astype(x.dtype)
    sn = sin[None, :, None].astype(x.dtype)
    out = jnp.stack([x1 * cs - x2 * sn, x1 * sn + x2 * cs], axis=-2)
    return out.reshape(x.shape)


def _ctx_attention(q, k, v, sink=None):
    B, L, HQ, DH = q.shape
    HKV = k.shape[2]
    G = HQ // HKV
    s = jnp.einsum('bqkgd,bskd->bkgqs', q.reshape(B, L, HKV, G, DH), k).astype(jnp.float32) * ATTN_SCALE
    if sink is not None:
        s_sink = jnp.broadcast_to(sink.astype(jnp.float32).reshape(1, HKV, G, 1, 1), s.shape[:-1] + (1,))
        s = jnp.concatenate([s, s_sink], axis=-1)
    p = jax.nn.softmax(s, axis=-1)
    if sink is not None:
        p = p[..., :-1]
    o = jnp.einsum('bkgqs,bskd->bqkgd', p.astype(v.dtype), v)
    return o.reshape(B, L, HQ * DH)


def _global_attention(q, k, v, ck, cv):
    B, S, HQ, DH = q.shape
    HKV = k.shape[2]
    G = HQ // HKV
    nb = S // Q_BLOCK
    keys = jnp.concatenate([k, ck], axis=1)
    vals = jnp.concatenate([v, cv], axis=1)
    qb = q.reshape(B, nb, Q_BLOCK, HKV, G, DH).transpose(1, 0, 2, 3, 4, 5)

    def block(qi):
        s = jnp.einsum('bqkgd,bskd->bkgqs', qi, keys).astype(jnp.float32) * ATTN_SCALE
        p = jax.nn.softmax(s, axis=-1).astype(vals.dtype)
        return jnp.einsum('bkgqs,bskd->bqkgd', p, vals)

    o = lax.map(block, qb)
    return o.transpose(1, 0, 2, 3, 4, 5).reshape(B, S, HQ * DH)


def _window_attention(q, k, v, ck, cv, sink):
    B, S, HQ, DH = q.shape
    HKV = k.shape[2]
    G = HQ // HKV
    nb = S // Q_BLOCK
    L = ck.shape[1]
    span = Q_BLOCK + 2 * WINDOW
    pad = ((0, 0), (WINDOW, WINDOW), (0, 0), (0, 0))
    k_pad = jnp.pad(k, pad)
    v_pad = jnp.pad(v, pad)
    qi_idx = jnp.arange(Q_BLOCK)[:, None]
    kj_idx = jnp.arange(span)[None, :]
    band = jnp.abs(kj_idx - WINDOW - qi_idx) <= WINDOW
    sink_col = sink.astype(jnp.float32).reshape(1, HKV, G, 1, 1)
    qb = q.reshape(B, nb, Q_BLOCK, HKV, G, DH).transpose(1, 0, 2, 3, 4, 5)

    def block(args):
        i, qi = args
        start = i * Q_BLOCK
        kw = lax.dynamic_slice_in_dim(k_pad, start, span, axis=1)
        vw = lax.dynamic_slice_in_dim(v_pad, start, span, axis=1)
        kpos = start - WINDOW + kj_idx
        valid = band & (kpos >= 0) & (kpos < S)
        s_loc = jnp.einsum('bqkgd,bskd->bkgqs', qi, kw).astype(jnp.float32) * ATTN_SCALE
        s_loc = jnp.where(valid, s_loc, NEG_INF)
        s_ctx = jnp.einsum('bqkgd,bskd->bkgqs', qi, ck).astype(jnp.float32) * ATTN_SCALE
        s_sink = jnp.broadcast_to(sink_col, s_ctx.shape[:-1] + (1,))
        p = jax.nn.softmax(jnp.concatenate([s_loc, s_ctx, s_sink], axis=-1), axis=-1).astype(v.dtype)
        return (jnp.einsum('bkgqs,bskd->bqkgd', p[..., :span], vw)
                + jnp.einsum('bkgqs,bskd->bqkgd', p[..., span:span + L], cv))

    o = lax.map(block, (jnp.arange(nb, dtype=jnp.int32), qb))
    return o.transpose(1, 0, 2, 3, 4, 5).reshape(B, S, HQ * DH)


def _neighbourhood_attention(q, k, v, ck, cv, rpb, rows):
    B, S, H, DH = q.shape
    kh = min(NA_ROWS, rows)
    kw = NA_COLS
    qg = q.reshape(B, rows, GRID_W, H, DH).transpose(1, 0, 2, 3, 4)
    kg = k.reshape(B, rows, GRID_W, H, DH)
    vg = v.reshape(B, rows, GRID_W, H, DH)
    col = jnp.arange(GRID_W, dtype=jnp.int32)
    cidx = jnp.clip(col - kw // 2, 0, GRID_W - kw)[:, None] + jnp.arange(kw, dtype=jnp.int32)[None, :]
    rpb_cols = rpb[:, :, cidx - col[:, None] + (NA_COLS - 1)]
    L = ck.shape[1]

    def block(args):
        r, qr = args
        r0 = jnp.clip(r - kh // 2, 0, rows - kh)
        kr = lax.dynamic_slice_in_dim(kg, r0, kh, axis=1)
        vr = lax.dynamic_slice_in_dim(vg, r0, kh, axis=1)
        kn = kr[:, :, cidx]
        vn = vr[:, :, cidx]
        bias = rpb_cols[:, r0 + jnp.arange(kh, dtype=jnp.int32) - r + (NA_ROWS - 1)]
        s_loc = jnp.einsum('bqhd,biqjhd->bhqij', qr, kn).astype(jnp.float32) * ATTN_SCALE
        s_loc = s_loc + bias.transpose(0, 2, 1, 3)[None].astype(jnp.float32)
        s_loc = s_loc.reshape(B, H, GRID_W, kh * kw)
        s_ctx = jnp.einsum('bqhd,bshd->bhqs', qr, ck).astype(jnp.float32) * ATTN_SCALE
        p = jax.nn.softmax(jnp.concatenate([s_loc, s_ctx], axis=-1), axis=-1).astype(v.dtype)
        p_loc = p[..., :kh * kw].reshape(B, H, GRID_W, kh, kw)
        return (jnp.einsum('bhqij,biqjhd->bqhd', p_loc, vn)
                + jnp.einsum('bhqs,bshd->bqhd', p[..., kh * kw:kh * kw + L], cv))

    o = lax.map(block, (jnp.arange(rows, dtype=jnp.int32), qg))
    return o.transpose(1, 0, 2, 3, 4).reshape(B, S, H * DH)


def _merge(outs, gate_logits, w_branch, w_out):
    g = jax.nn.sigmoid(gate_logits.astype(jnp.float32)).astype(gate_logits.dtype)
    g = g.reshape(*g.shape[:-1], N_BRANCHES, D_MODEL)
    y = g[..., 0, :] * (outs[0] @ w_branch[0])
    for n in range(1, N_BRANCHES):
        y = y + g[..., n, :] * (outs[n] @ w_branch[n])
    return y @ w_out


def _route(hf, w_router, router_bias):
    scores = jax.nn.sigmoid((hf @ w_router).astype(jnp.float32))
    sel = scores + router_bias.astype(jnp.float32)
    grouped = sel.reshape(-1, N_GROUPS, EXPERTS_PER_GROUP)
    group_score = lax.top_k(grouped, TOP_K)[0].sum(-1)
    best = jnp.argmax(group_score, axis=-1)
    in_group = (jnp.arange(N_EXPERTS) // EXPERTS_PER_GROUP)[None, :] == best[:, None]
    _, idx = lax.top_k(jnp.where(in_group, sel, NEG_INF), TOP_K)
    w = jnp.take_along_axis(scores, idx, axis=-1)
    w = w / jnp.sum(w, axis=-1, keepdims=True)
    return jnp.sum(jax.nn.one_hot(idx, N_EXPERTS, dtype=jnp.float32) * w[..., None], axis=1)


def _moe(h, w_router, router_bias, w_gate, w_up, w_down):
    shp = h.shape
    hf = h.reshape(-1, shp[-1])
    combine = _route(hf, w_router, router_bias).astype(h.dtype)
    y = jnp.zeros_like(hf)
    for e in range(N_EXPERTS):
        a = jax.nn.silu(hf @ w_gate[e]) * (hf @ w_up[e])
        y = y + combine[:, e:e + 1] * (a @ w_down[e])
    return y.reshape(shp)


def setup_inputs(seed: int = 0) -> dict:
    key = jax.random.key(seed)
    ks = jax.random.split(key, 24)

    def nrm(k, shape, s):
        return jax.random.normal(k, shape, jnp.float32) * s

    return {
        'x': nrm(ks[0], (BATCH, SEQ, D_MODEL), 1.0),
        'c': nrm(ks[1], (BATCH, D_MODEL), 1.0),
        'ctx': nrm(ks[2], (BATCH, CTX_LEN, D_MODEL), 1.0),
        'c_ctx': nrm(ks[3], (D_MODEL,), 1.0),
        'w_mod': nrm(ks[4], (DEPTH, D_MODEL, 6 * D_MODEL), 0.5 * D_MODEL ** -0.5),
        'b_mod': nrm(ks[5], (DEPTH, 6 * D_MODEL), 0.02),
        'norm1': 1.0 + nrm(ks[6], (DEPTH, D_MODEL), 0.1),
        'norm2': 1.0 + nrm(ks[7], (DEPTH, D_MODEL), 0.1),
        'w_in': nrm(ks[8], (DEPTH, D_MODEL, IN_COLS), D_MODEL ** -0.5),
        'q_norm_a': 1.0 + nrm(ks[9], (DEPTH, HEAD_DIM), 0.1),
        'k_norm_a': 1.0 + nrm(ks[10], (DEPTH, HEAD_DIM), 0.1),
        'sink_b': nrm(ks[11], (DEPTH, N_HEADS_B), 1.0),
        'rpb_n': nrm(ks[12], (DEPTH, N_HEADS_N, 2 * NA_ROWS - 1, 2 * NA_COLS - 1), 0.5),
        'w_branch': nrm(ks[13], (DEPTH, N_BRANCHES, BRANCH_WIDTH, D_MODEL), BRANCH_WIDTH ** -0.5),
        'w_out': nrm(ks[14], (DEPTH, D_MODEL, D_MODEL), D_MODEL ** -0.5),
        'w_router': nrm(ks[15], (D_MODEL, N_EXPERTS), D_MODEL ** -0.5),
        'router_bias': nrm(ks[16], (N_EXPERTS,), 0.01),
        'w_e_gate': nrm(ks[17], (DEPTH, N_EXPERTS, D_MODEL, D_FF_EXPERT), D_MODEL ** -0.5),
        'w_e_up': nrm(ks[18], (DEPTH, N_EXPERTS, D_MODEL, D_FF_EXPERT), D_MODEL ** -0.5),
        'w_e_down': nrm(ks[19], (DEPTH, N_EXPERTS, D_FF_EXPERT, D_MODEL), D_FF_EXPERT ** -0.5),
        'final_norm': 1.0 + nrm(ks[20], (D_MODEL,), 0.1),
    }


def reference(x, c, ctx, c_ctx, w_mod, b_mod, norm1, norm2, w_in, q_norm_a, k_norm_a, sink_b,
              rpb_n, w_branch, w_out, w_router, router_bias, w_e_gate, w_e_up, w_e_down, final_norm):
    S = x.shape[1]
    rows = S // GRID_W
    cos, sin = _rope_tables(S)
    for l in range(DEPTH):
        last = l == DEPTH - 1
        mod = (jax.nn.silu(c) @ w_mod[l] + b_mod[l])[:, None, :]
        mod_c = (jax.nn.silu(c_ctx) @ w_mod[l] + b_mod[l])[None, None, :]
        sh1, sc1, g1, sh2, sc2, g2 = jnp.split(mod, 6, axis=-1)
        csh1, csc1, cg1, csh2, csc2, cg2 = jnp.split(mod_c, 6, axis=-1)

        h = _modulate(x, norm1[l], sh1, sc1)
        hc = _modulate(ctx, norm1[l], csh1, csc1)
        q_a, k_a, v_a, q_b, k_b, v_b, q_n, k_n, v_n, gate_l = jnp.split(h @ w_in[l], IN_OFFSETS, axis=-1)
        cq_a, ck_a, cv_a, cq_b, ck_b, cv_b, cq_n, ck_n, cv_n, gate_c = jnp.split(hc @ w_in[l], IN_OFFSETS, axis=-1)

        ck_a = _rmsnorm(_heads(ck_a, N_KV_A), k_norm_a[l])
        cv_a = _heads(cv_a, N_KV_A)
        ck_b = _heads(ck_b, N_KV_B)
        cv_b = _heads(cv_b, N_KV_B)
        ck_n = _heads(ck_n, N_HEADS_N)
        cv_n = _heads(cv_n, N_HEADS_N)

        qa = _rope_2d(_rmsnorm(_heads(q_a, N_HEADS_A), q_norm_a[l]), cos, sin)
        ka = _rope_2d(_rmsnorm(_heads(k_a, N_KV_A), k_norm_a[l]), cos, sin)
        o_a = _global_attention(qa, ka, _heads(v_a, N_KV_A), ck_a, cv_a)
        qb = _rope_2d(_heads(q_b, N_HEADS_B), cos, sin)
        kb = _rope_2d(_heads(k_b, N_KV_B), cos, sin)
        o_b = _window_attention(qb, kb, _heads(v_b, N_KV_B), ck_b, cv_b, sink_b[l])
        o_n = _neighbourhood_attention(_heads(q_n, N_HEADS_N), _heads(k_n, N_HEADS_N), _heads(v_n, N_HEADS_N),
                                       ck_n, cv_n, rpb_n[l], rows)
        x = x + g1 * _merge((o_a, o_b, o_n), gate_l, w_branch[l], w_out[l])

        if not last:
            oc_a = _ctx_attention(_rmsnorm(_heads(cq_a, N_HEADS_A), q_norm_a[l]), ck_a, cv_a)
            oc_b = _ctx_attention(_heads(cq_b, N_HEADS_B), ck_b, cv_b, sink_b[l])
            oc_n = _ctx_attention(_heads(cq_n, N_HEADS_N), ck_n, cv_n)
            ctx = ctx + cg1 * _merge((oc_a, oc_b, oc_n), gate_c, w_branch[l], w_out[l])
            ctx = ctx + cg2 * _moe(_modulate(ctx, norm2[l], csh2, csc2), w_router, router_bias,
                                   w_e_gate[l], w_e_up[l], w_e_down[l])

        x = x + g2 * _moe(_modulate(x, norm2[l], sh2, sc2), w_router, router_bias,
                          w_e_gate[l], w_e_up[l], w_e_down[l])
    return _rmsnorm(x, final_norm)
```

```python
import functools

import numpy as np
import jax
import jax.numpy as jnp
from jax import lax
from jax.experimental import pallas as pl
from jax.experimental.pallas import tpu as pltpu

D_MODEL = 1024
HEAD_DIM = 64
GRID_W = 64
N_HEADS = 8
WINDOW = 128
NA_ROWS = 8
NA_COLS = 16
ROPE_THETA = 10000.0
N_EXPERTS = 16
N_GROUPS = 4
EXPERTS_PER_GROUP = N_EXPERTS // N_GROUPS
D_FF = 512
EPS = 1e-6
NEG_INF = -1e30
ATTN_SCALE = HEAD_DIM ** -0.5
BRANCH = N_HEADS * HEAD_DIM
KV_A = 2 * HEAD_DIM
IN_COLS = 6144

OFF_QA, OFF_KA, OFF_VA = 0, 512, 640
OFF_QB, OFF_KB, OFF_VB = 768, 1280, 1408
OFF_QN, OFF_KN, OFF_VN = 1536, 2048, 2560
OFF_GATE = 3072

TM = 256
TM_E = 512
NA_WIN_ROWS = 12
VMEM_LIMIT = 56 * 1024 * 1024

_F32 = jnp.float32
_BF16 = jnp.bfloat16

_PERM_Q = np.concatenate(
    [np.concatenate([np.arange(64 * j, 64 * j + 64), np.arange(64 * (4 + j), 64 * (4 + j) + 64)]) for j in range(4)]
)
_COL_PERM = np.arange(IN_COLS)
_COL_PERM[OFF_QA:OFF_QA + BRANCH] = OFF_QA + _PERM_Q
_COL_PERM[OFF_QB:OFF_QB + BRANCH] = OFF_QB + _PERM_Q


def _nt(a, b):
    return lax.dot_general(a, b, (((1,), (1,)), ((), ())), preferred_element_type=_F32)


def _dot(a, b):
    return jnp.dot(a, b, preferred_element_type=_F32)


def _cparams(sem, vmem=None):
    return pltpu.CompilerParams(dimension_semantics=sem, vmem_limit_bytes=vmem)


def _mod_body(c_ref, w_ref, b_ref, o_ref):
    cv = c_ref[...]
    a = (cv * jax.nn.sigmoid(cv)).astype(_BF16)
    o_ref[0] = _dot(a, w_ref[0].astype(_BF16)) + b_ref[0]


def _mod_call(cvec, w_mod, b_mod):
    depth, d, n = w_mod.shape
    tn = 1536
    return pl.pallas_call(
        _mod_body,
        grid=(depth, n // tn),
        in_specs=[
            pl.BlockSpec((16, d), lambda l, j: (0, 0)),
            pl.BlockSpec((1, d, tn), lambda l, j: (l, 0, j)),
            pl.BlockSpec((1, 1, tn), lambda l, j: (l, 0, j)),
        ],
        out_specs=pl.BlockSpec((1, 16, tn), lambda l, j: (l, 0, j)),
        out_shape=jax.ShapeDtypeStruct((depth, 16, n), _F32),
        compiler_params=_cparams(("arbitrary", "arbitrary")),
        name="mod_vectors",
    )(cvec, w_mod, b_mod.reshape(depth, 1, n))


def _rope(v, c, s):
    w = v.shape[1]
    lane = lax.broadcasted_iota(jnp.int32, v.shape, 1)
    first = (lane & 16) == 0
    sw = jnp.where(first, pltpu.roll(v, w - 16, 1), pltpu.roll(v, 16, 1))
    return v * c + sw * s


def _inproj_body(x_ref, g_ref, sh_ref, sc_ref, w_ref, cos_ref, sin_ref, gq_ref, gk_ref, bd_ref,
                 qa_ref, ka_ref, va_ref, qb_ref, kb_ref, vb_ref, qn_ref, kn_ref, vn_ref, gate_ref):
    x = x_ref[0]
    ms = jnp.mean(x * x, axis=-1, keepdims=True)
    h = (x * lax.rsqrt(ms + EPS) * g_ref[...]) * (1.0 + sc_ref[0]) + sh_ref[0]
    hb = h.astype(_BF16)

    def seg(off, width):
        return _dot(hb, w_ref[:, off:off + width])

    c1 = cos_ref[...]
    s1 = sin_ref[...]
    c4 = jnp.concatenate([c1] * 4, axis=1)
    s4 = jnp.concatenate([s1] * 4, axis=1)
    bd = bd_ref[...]

    qa = seg(OFF_QA, BRANCH)
    ssq = _dot((qa * qa).astype(_BF16), bd)
    qa = qa * lax.rsqrt(ssq * (1.0 / HEAD_DIM) + EPS) * gq_ref[...]
    qa_ref[0] = _rope(qa, c4, s4).astype(_BF16)
    ka = seg(OFF_KA, KV_A)
    ssk = _dot((ka * ka).astype(_BF16), bd_ref[:KV_A, :KV_A])
    ka = ka * lax.rsqrt(ssk * (1.0 / HEAD_DIM) + EPS) * gk_ref[...]
    ka_ref[0] = _rope(ka, c1, s1).astype(_BF16)
    va_ref[0] = seg(OFF_VA, KV_A).astype(_BF16)
    qb_ref[0] = _rope(seg(OFF_QB, BRANCH) * ATTN_SCALE, c4, s4).astype(_BF16)
    kb_ref[0] = _rope(seg(OFF_KB, KV_A), c1, s1).astype(_BF16)
    vb_ref[0] = seg(OFF_VB, KV_A).astype(_BF16)
    qn_ref[0] = (seg(OFF_QN, BRANCH) * ATTN_SCALE).astype(_BF16)
    kn_ref[0] = seg(OFF_KN, BRANCH).astype(_BF16)
    vn_ref[0] = seg(OFF_VN, BRANCH).astype(_BF16)
    for n in range(3):
        gate_ref[0, :, n * D_MODEL:(n + 1) * D_MODEL] = seg(OFF_GATE + n * D_MODEL, D_MODEL).astype(_BF16)


def _inproj_call(xc, n_lat_tiles, g1n, sh, sc, w, cos_t, sin_t, gq, gk, bd):
    b, t, d = xc.shape
    nb = sh.shape[0] - 1
    nt = t // TM

    def mod_idx(bi, i):
        return (jnp.where(i < n_lat_tiles, bi, nb), 0, 0)

    def tok(width):
        return pl.BlockSpec((1, TM, width), lambda bi, i: (bi, i, 0))

    widths = [BRANCH, KV_A, KV_A, BRANCH, KV_A, KV_A, BRANCH, BRANCH, BRANCH, 3 * D_MODEL]
    return pl.pallas_call(
        _inproj_body,
        grid=(b, nt),
        in_specs=[
            tok(d),
            pl.BlockSpec((1, d), lambda bi, i: (0, 0)),
            pl.BlockSpec((1, 1, d), mod_idx),
            pl.BlockSpec((1, 1, d), mod_idx),
            pl.BlockSpec((d, IN_COLS), lambda bi, i: (0, 0)),
            pl.BlockSpec((TM, 128), lambda bi, i: (i, 0)),
            pl.BlockSpec((TM, 128), lambda bi, i: (i, 0)),
            pl.BlockSpec((1, BRANCH), lambda bi, i: (0, 0)),
            pl.BlockSpec((1, KV_A), lambda bi, i: (0, 0)),
            pl.BlockSpec((BRANCH, BRANCH), lambda bi, i: (0, 0)),
        ],
        out_specs=[tok(wd) for wd in widths],
        out_shape=[jax.ShapeDtypeStruct((b, t, wd), _BF16) for wd in widths],
        compiler_params=_cparams(("arbitrary", "arbitrary"), VMEM_LIMIT),
        name="inproj",
    )(xc, g1n, sh, sc, w, cos_t, sin_t, gq, gk, bd)


def _stack_heads(q):
    lane = lax.broadcasted_iota(jnp.int32, q.shape, 1)
    lo = lane < HEAD_DIM
    zero = jnp.zeros_like(q)
    return jnp.concatenate([jnp.where(lo, q, zero), jnp.where(lo, zero, q)], axis=0)


def _unstack_heads(o):
    half = o.shape[0] // 2
    lane = lax.broadcasted_iota(jnp.int32, (half, o.shape[1]), 1)
    return jnp.where(lane < HEAD_DIM, o[:half], o[half:])


def _attn_a_body(q_ref, k_ref, v_ref, o_ref, s_ref, p_ref, *, n_chunks):
    lhs = _stack_heads(q_ref[0])
    m128 = jnp.full((2 * TM, 128), NEG_INF, _F32)
    for c in range(n_chunks):
        s = _nt(lhs, k_ref[0, c * 256:(c + 1) * 256, :])
        s_ref[:, c * 256:(c + 1) * 256] = s
        m128 = jnp.maximum(m128, jnp.maximum(s[:, :128], s[:, 128:]))
    m = jnp.max(m128, axis=1, keepdims=True)
    l128 = jnp.zeros((2 * TM, 128), _F32)
    for c in range(n_chunks):
        p = jnp.exp(s_ref[:, c * 256:(c + 1) * 256] - m)
        l128 = l128 + (p[:, :128] + p[:, 128:])
        p_ref[:, c * 256:(c + 1) * 256] = p.astype(_BF16)
    l = jnp.sum(l128, axis=1, keepdims=True)
    o = _dot(p_ref[...], v_ref[0]) / l
    o_ref[0] = _unstack_heads(o).astype(_BF16)


def _attn_a_call(q, k, v, s_len):
    b, t, _ = q.shape
    nq = s_len // TM
    return pl.pallas_call(
        functools.partial(_attn_a_body, n_chunks=t // 256),
        grid=(b, nq, 4),
        in_specs=[
            pl.BlockSpec((1, TM, 128), lambda bi, i, j: (bi, i, j)),
            pl.BlockSpec((1, t, 128), lambda bi, i, j: (bi, 0, 0)),
            pl.BlockSpec((1, t, 128), lambda bi, i, j: (bi, 0, 0)),
        ],
        out_specs=pl.BlockSpec((1, TM, 128), lambda bi, i, j: (bi, i, j)),
        out_shape=jax.ShapeDtypeStruct((b, t, BRANCH), _BF16),
        scratch_shapes=[pltpu.VMEM((2 * TM, t), _F32), pltpu.VMEM((2 * TM, t), _BF16)],
        compiler_params=_cparams(("arbitrary", "arbitrary", "arbitrary"), VMEM_LIMIT),
        name="attn_global",
    )(q, k, v)


def _softmax_parts(parts, extra=None):
    m = functools.reduce(jnp.maximum, [jnp.max(s, axis=1, keepdims=True) for s in parts])
    if extra is not None:
        m = jnp.maximum(m, extra)
    ps = [jnp.exp(s - m) for s in parts]
    l = functools.reduce(lambda a, c: a + c, [jnp.sum(p, axis=1, keepdims=True) for p in ps])
    if extra is not None:
        l = l + jnp.exp(extra - m)
    return ps, l


def _attn_b_body(sink_ref, q_ref, k_ref, v_ref, o_ref, *, s_len, l_ctx):
    span = TM + 2 * WINDOW
    i = pl.program_id(1)
    start = i * TM
    ws = pl.multiple_of(jnp.clip(start - WINDOW, 0, s_len - span), WINDOW)
    kw = k_ref[0, pl.ds(ws, span), :]
    vw = v_ref[0, pl.ds(ws, span), :]
    kc = k_ref[0, s_len:s_len + l_ctx, :]
    vc = v_ref[0, s_len:s_len + l_ctx, :]
    row = lax.broadcasted_iota(jnp.int32, (2 * TM, span), 0)
    qpos = start + jnp.where(row >= TM, row - TM, row)
    kpos = ws + lax.broadcasted_iota(jnp.int32, (2 * TM, span), 1)
    band = jnp.abs(kpos - qpos) <= WINDOW
    is_hi = lax.broadcasted_iota(jnp.int32, (2 * TM, 1), 0) >= TM
    for j in range(4):
        lhs = _stack_heads(q_ref[0, :, j * 128:(j + 1) * 128])
        s_loc = jnp.where(band, _nt(lhs, kw), NEG_INF)
        s_ctx = _nt(lhs, kc)
        sink = jnp.where(is_hi, sink_ref[4 + j], sink_ref[j])
        (p_loc, p_ctx), l = _softmax_parts([s_loc, s_ctx], sink)
        o = (_dot(p_loc.astype(_BF16), vw) + _dot(p_ctx.astype(_BF16), vc)) / l
        o_ref[0, :, j * 128:(j + 1) * 128] = _unstack_heads(o).astype(_BF16)


def _attn_b_call(q, k, v, sink, s_len):
    b, t, _ = q.shape
    nq = s_len // TM
    return pl.pallas_call(
        functools.partial(_attn_b_body, s_len=s_len, l_ctx=t - s_len),
        grid=(b, nq),
        in_specs=[
            pl.BlockSpec(memory_space=pltpu.SMEM),
            pl.BlockSpec((1, TM, BRANCH), lambda bi, i: (bi, i, 0)),
            pl.BlockSpec((1, t, 128), lambda bi, i: (bi, 0, 0)),
            pl.BlockSpec((1, t, 128), lambda bi, i: (bi, 0, 0)),
        ],
        out_specs=pl.BlockSpec((1, TM, BRANCH), lambda bi, i: (bi, i, 0)),
        out_shape=jax.ShapeDtypeStruct((b, t, BRANCH), _BF16),
        compiler_params=_cparams(("arbitrary", "arbitrary"), VMEM_LIMIT),
        name="attn_window",
    )(sink, q, k, v)


def _attn_n_body(q_ref, k_ref, v_ref, bias_ref, o_ref, *, s_len, l_ctx):
    rows = s_len // GRID_W
    span = NA_WIN_ROWS * GRID_W
    r = pl.program_id(2)
    u0 = jnp.clip(4 * r - 4, 0, rows - NA_WIN_ROWS)
    ws = pl.multiple_of(u0 * GRID_W, GRID_W)
    kw = k_ref[0, pl.ds(ws, span), :]
    vw = v_ref[0, pl.ds(ws, span), :]
    kc = k_ref[0, s_len:s_len + l_ctx, :]
    vc = v_ref[0, s_len:s_len + l_ctx, :]
    lhs = _stack_heads(q_ref[0])
    s_loc = _nt(lhs, kw) + bias_ref[0, 0]
    s_ctx = _nt(lhs, kc)
    (p_loc, p_ctx), l = _softmax_parts([s_loc, s_ctx])
    o = (_dot(p_loc.astype(_BF16), vw) + _dot(p_ctx.astype(_BF16), vc)) / l
    o_ref[0] = _unstack_heads(o).astype(_BF16)


def _attn_n_call(q, k, v, bias_tbl, s_len):
    b, t, _ = q.shape
    n_r = s_len // TM
    span = NA_WIN_ROWS * GRID_W

    def case(ri):
        return jnp.where(ri == 0, 0, jnp.where(ri == n_r - 1, 2, 1))

    return pl.pallas_call(
        functools.partial(_attn_n_body, s_len=s_len, l_ctx=t - s_len),
        grid=(b, 4, n_r),
        in_specs=[
            pl.BlockSpec((1, TM, 128), lambda bi, j, ri: (bi, ri, j)),
            pl.BlockSpec((1, t, 128), lambda bi, j, ri: (bi, 0, j)),
            pl.BlockSpec((1, t, 128), lambda bi, j, ri: (bi, 0, j)),
            pl.BlockSpec((1, 1, 2 * TM, span), lambda bi, j, ri: (case(ri), j, 0, 0)),
        ],
        out_specs=pl.BlockSpec((1, TM, 128), lambda bi, j, ri: (bi, ri, j)),
        out_shape=jax.ShapeDtypeStruct((b, t, BRANCH), _BF16),
        compiler_params=_cparams(("arbitrary", "arbitrary", "arbitrary"), VMEM_LIMIT),
        name="attn_neighbourhood",
    )(q, k, v, bias_tbl)


def _na_tables(rows):
    a = np.arange(4)[:, None, None, None]
    c = np.arange(GRID_W)[None, :, None, None]
    i = np.arange(NA_WIN_ROWS)[None, None, :, None]
    kc = np.arange(GRID_W)[None, None, None, :]
    c0 = np.clip(c - NA_COLS // 2, 0, GRID_W - NA_COLS)
    col_ok = (kc >= c0) & (kc < c0 + NA_COLS)
    cidx = kc - c + (NA_COLS - 1)
    row_ok = [i < NA_ROWS + 0 * a, (i >= a) & (i < a + NA_ROWS), i >= NA_WIN_ROWS - NA_ROWS + 0 * a]
    ridx = [i - a + 7, i - a + 3, i - a - 1]
    shape = (4, GRID_W, NA_WIN_ROWS, GRID_W)
    sel_r, sel_c, ok = [], [], []
    for cs in range(3):
        valid = np.broadcast_to(row_ok[cs] & col_ok, shape)
        ok.append(valid.reshape(4 * GRID_W, NA_WIN_ROWS * GRID_W))
        rsel = (np.broadcast_to(ridx[cs], (4, 1, NA_WIN_ROWS, 1))[:, 0, :, 0][..., None]
                == np.arange(2 * NA_ROWS - 1)[None, None, :])
        sel_r.append(rsel.astype(np.float32))
    csel = (np.broadcast_to(cidx, (1, GRID_W, 1, GRID_W))[0, :, 0, :][..., None]
            == np.arange(2 * NA_COLS - 1)[None, None, :]).astype(np.float32)
    return np.stack(sel_r), csel, np.stack(ok)


def _na_bias_table(rpb, rows):
    sel_r, sel_c, ok = _na_tables(rows)
    hi = lax.Precision.HIGHEST
    t = jnp.einsum("sair,hrg->shaig", jnp.asarray(sel_r), rpb.astype(_F32), precision=hi)
    t = jnp.einsum("shaig,ckg->shacik", t, jnp.asarray(sel_c), precision=hi)
    t = t.reshape(3, N_HEADS, 4 * GRID_W, NA_WIN_ROWS * GRID_W)
    t = jnp.where(jnp.asarray(ok)[:, None], t, NEG_INF)
    return t.reshape(3, 4, 2 * TM, NA_WIN_ROWS * GRID_W)


def _attn_ctx_body(sink_ref, qa_ref, ka_ref, va_ref, qb_ref, kb_ref, vb_ref, qn_ref, kn_ref, vn_ref,
                   oa_in, ob_in, on_in, oa_ref, ob_ref, on_ref):
    del oa_in, ob_in, on_in
    is_hi = lax.broadcasted_iota(jnp.int32, (2 * TM, 1), 0) >= TM
    for j in range(4):
        sl = slice(j * 128, (j + 1) * 128)
        lhs = _stack_heads(qa_ref[0, :, sl])
        (p,), l = _softmax_parts([_nt(lhs, ka_ref[0])])
        oa_ref[0, :, sl] = _unstack_heads(_dot(p.astype(_BF16), va_ref[0]) / l).astype(_BF16)
        lhs = _stack_heads(qb_ref[0, :, sl])
        sink = jnp.where(is_hi, sink_ref[4 + j], sink_ref[j])
        (p,), l = _softmax_parts([_nt(lhs, kb_ref[0])], sink)
        ob_ref[0, :, sl] = _unstack_heads(_dot(p.astype(_BF16), vb_ref[0]) / l).astype(_BF16)
        lhs = _stack_heads(qn_ref[0, :, sl])
        (p,), l = _softmax_parts([_nt(lhs, kn_ref[0, :, sl])])
        on_ref[0, :, sl] = _unstack_heads(_dot(p.astype(_BF16), vn_ref[0, :, sl]) / l).astype(_BF16)


def _attn_ctx_call(sink, qa, ka, va, qb, kb, vb, qn, kn, vn, oa, ob, on, s_len):
    b, t, _ = qa.shape
    assert t - s_len == TM
    ci = s_len // TM

    def blk(width):
        return pl.BlockSpec((1, TM, width), lambda bi: (bi, ci, 0))

    anyspec = pl.BlockSpec(memory_space=pl.ANY)
    return pl.pallas_call(
        _attn_ctx_body,
        grid=(b,),
        in_specs=[pl.BlockSpec(memory_space=pltpu.SMEM),
                  blk(BRANCH), blk(KV_A), blk(KV_A), blk(BRANCH), blk(KV_A), blk(KV_A),
                  blk(BRANCH), blk(BRANCH), blk(BRANCH), anyspec, anyspec, anyspec],
        out_specs=[blk(BRANCH)] * 3,
        out_shape=[jax.ShapeDtypeStruct((b, t, BRANCH), _BF16)] * 3,
        input_output_aliases={10: 0, 11: 1, 12: 2},
        compiler_params=_cparams(("arbitrary",)),
        name="attn_context",
    )(sink, qa, ka, va, qb, kb, vb, qn, kn, vn, oa, ob, on)


def _route_rows(sel, sc):
    gs, gi1, gi2, gs1, gs2 = [], [], [], [], []
    for g in range(N_GROUPS):
        v = sel[4 * g:4 * g + 4]
        s = sc[4 * g:4 * g + 4]
        m1 = jnp.maximum(jnp.maximum(v[0], v[1]), jnp.maximum(v[2], v[3]))
        i1 = jnp.where(v[0] == m1, 0, jnp.where(v[1] == m1, 1, jnp.where(v[2] == m1, 2, 3)))
        rest = [jnp.where(i1 == k, NEG_INF, v[k]) for k in range(4)]
        m2 = jnp.maximum(jnp.maximum(rest[0], rest[1]), jnp.maximum(rest[2], rest[3]))
        i2 = jnp.where(rest[0] == m2, 0, jnp.where(rest[1] == m2, 1, jnp.where(rest[2] == m2, 2, 3)))

        def pick(idx, s=s):
            return jnp.where(idx == 0, s[0], jnp.where(idx == 1, s[1], jnp.where(idx == 2, s[2], s[3])))

        gs.append(m1 + m2)
        gi1.append(i1 + 4 * g)
        gi2.append(i2 + 4 * g)
        gs1.append(pick(i1))
        gs2.append(pick(i2))
    best = jnp.maximum(jnp.maximum(gs[0], gs[1]), jnp.maximum(gs[2], gs[3]))
    gsel = jnp.where(gs[0] == best, 0, jnp.where(gs[1] == best, 1, jnp.where(gs[2] == best, 2, 3)))

    def pickg(vals):
        return jnp.where(gsel == 0, vals[0], jnp.where(gsel == 1, vals[1], jnp.where(gsel == 2, vals[2], vals[3])))

    return pickg(gi1), pickg(gi2), pickg(gs1), pickg(gs2)


def _merge_body(oa_ref, ob_ref, on_ref, gate_ref, x_ref, wb_ref, wo_ref, g1_ref, sh2_ref, sc2_ref, n2_ref,
                wrh_ref, wrl_ref, rb_ref,
                x1_ref, h2_ref, idx_ref, rank_ref, wts_ref, cnt_ref, carry_ref):
    first = (pl.program_id(0) == 0) & (pl.program_id(1) == 0)

    @pl.when(first)
    def _():
        carry_ref[...] = jnp.zeros_like(carry_ref)

    outs = (oa_ref, ob_ref, on_ref)
    y = None
    for n in range(3):
        gt = jax.nn.sigmoid(gate_ref[0, :, n * D_MODEL:(n + 1) * D_MODEL].astype(_F32))
        term = gt * _dot(outs[n][0], wb_ref[n])
        y = term if y is None else y + term
    z = _dot(y.astype(_BF16), wo_ref[...])
    x1 = x_ref[0] + g1_ref[0] * z
    x1_ref[0] = x1
    ms = jnp.mean(x1 * x1, axis=-1, keepdims=True)
    h2 = (x1 * lax.rsqrt(ms + EPS) * n2_ref[...]) * (1.0 + sc2_ref[0]) + sh2_ref[0]
    h2_ref[0] = h2

    hb = h2.astype(_BF16)
    hl = (h2 - hb.astype(_F32)).astype(_BF16)
    logits = _nt(wrh_ref[...], hb) + _nt(wrh_ref[...], hl) + _nt(wrl_ref[...], hb)
    scores = jax.nn.sigmoid(logits)
    sel = scores + rb_ref[...]
    sel_rows = [sel[e:e + 1, :] for e in range(N_EXPERTS)]
    sc_rows = [scores[e:e + 1, :] for e in range(N_EXPERTS)]
    e1, e2, s1, s2 = _route_rows(sel_rows, sc_rows)
    den = s1 + s2
    idx_ref[0:1, :] = e1
    idx_ref[1:2, :] = e2
    wts_ref[0:1, :] = s1 / den
    wts_ref[1:2, :] = s2 / den

    eid = lax.broadcasted_iota(jnp.int32, (N_EXPERTS, TM), 0)
    oh1 = eid == e1
    oh2 = eid == e2
    tri = (lax.broadcasted_iota(jnp.int32, (TM, TM), 0) <= lax.broadcasted_iota(jnp.int32, (TM, TM), 1))
    tri = tri.astype(_BF16)
    cum1 = _dot(oh1.astype(_BF16), tri)
    cum2 = _dot(oh2.astype(_BF16), tri)
    tot1 = cum1[:, TM - 1:TM]
    tot2 = cum2[:, TM - 1:TM]
    carry = carry_ref[:, 0:1]
    r1 = jnp.sum(jnp.where(oh1, carry + cum1 - 1.0, 0.0), axis=0, keepdims=True)
    r2 = jnp.sum(jnp.where(oh2, carry + tot1 + cum2 - 1.0, 0.0), axis=0, keepdims=True)
    rank_ref[0:1, :] = r1.astype(jnp.int32)
    rank_ref[1:2, :] = r2.astype(jnp.int32)
    new_carry = carry + tot1 + tot2
    carry_ref[...] = jnp.broadcast_to(new_carry, carry_ref.shape)
    cnt_ref[...] = jnp.broadcast_to(new_carry, cnt_ref.shape).astype(jnp.int32)


def _merge_call(oa, ob, on, gate, xc, wb, wo, g1, sh2, sc2, n2, wrh, wrl, rb, t_out, n_lat_tiles):
    b, t, d = xc.shape
    nb = g1.shape[0] - 1
    nt = t_out // TM
    n_tok = b * t_out

    def mod_idx(bi, i):
        return (jnp.where(i < n_lat_tiles, bi, nb), 0, 0)

    def tok(width):
        return pl.BlockSpec((1, TM, width), lambda bi, i: (bi, i, 0))

    def full(shape):
        return pl.BlockSpec(shape, lambda bi, i: (0,) * len(shape))

    lane_out = pl.BlockSpec((2, TM), lambda bi, i: (0, bi * nt + i))
    return pl.pallas_call(
        _merge_body,
        grid=(b, nt),
        in_specs=[tok(BRANCH), tok(BRANCH), tok(BRANCH), tok(3 * d), tok(d),
                  full((3, BRANCH, d)), full((d, d)),
                  pl.BlockSpec((1, 1, d), mod_idx), pl.BlockSpec((1, 1, d), mod_idx),
                  pl.BlockSpec((1, 1, d), mod_idx), full((1, d)),
                  full((N_EXPERTS, d)), full((N_EXPERTS, d)), full((N_EXPERTS, 1))],
        out_specs=[tok(d), tok(d), lane_out, lane_out, lane_out, full((N_EXPERTS, 128))],
        out_shape=[jax.ShapeDtypeStruct((b, t_out, d), _F32), jax.ShapeDtypeStruct((b, t_out, d), _F32),
                   jax.ShapeDtypeStruct((2, n_tok), jnp.int32), jax.ShapeDtypeStruct((2, n_tok), jnp.int32),
                   jax.ShapeDtypeStruct((2, n_tok), _F32), jax.ShapeDtypeStruct((N_EXPERTS, 128), jnp.int32)],
        scratch_shapes=[pltpu.VMEM((N_EXPERTS, 128), _F32)],
        compiler_params=_cparams(("arbitrary", "arbitrary"), VMEM_LIMIT),
        name="merge_route",
    )(oa, ob, on, gate, xc, wb, wo, g1, sh2, sc2, n2, wrh, wrl, rb)


def _dispatch_body(pos_ref, h_ref, xs_in, xs_ref, sem):
    del xs_in
    base = pl.program_id(0) * TM

    def row_copy(t, k):
        return pltpu.make_async_copy(h_ref.at[pl.ds(base + t, 1)], xs_ref.at[pl.ds(pos_ref[0, k, t], 1)], sem)

    def issue(t, carry):
        row_copy(t, 0).start()
        row_copy(t, 1).start()
        return carry

    lax.fori_loop(0, TM, issue, 0)

    def drain(t, carry):
        row_copy(t, 0).wait()
        row_copy(t, 1).wait()
        return carry

    lax.fori_loop(0, TM, drain, 0)


def _dispatch_call(pos3, h2, xs_zero):
    n_tiles = pos3.shape[0]
    anyspec = pl.BlockSpec(memory_space=pl.ANY)
    return pl.pallas_call(
        _dispatch_body,
        grid=(n_tiles,),
        in_specs=[pl.BlockSpec((1, 2, TM), lambda i: (i, 0, 0), memory_space=pltpu.SMEM), anyspec, anyspec],
        out_specs=anyspec,
        out_shape=jax.ShapeDtypeStruct(xs_zero.shape, xs_zero.dtype),
        input_output_aliases={2: 0},
        scratch_shapes=[pltpu.SemaphoreType.DMA],
        compiler_params=_cparams(("arbitrary",)),
        name="moe_dispatch",
    )(pos3, h2, xs_zero)


def _moe_body(te_ref, nv_ref, xs_ref, wg_ref, wu_ref, wd_ref, ys_ref):
    del te_ref

    @pl.when(pl.program_id(0) < nv_ref[0])
    def _():
        xb = xs_ref[...].astype(_BF16)
        gate = _dot(xb, wg_ref[0])
        up = _dot(xb, wu_ref[0])
        act = (gate * jax.nn.sigmoid(gate)) * up
        ys_ref[...] = _dot(act.astype(_BF16), wd_ref[0])


def _moe_call(tile_expert, n_valid, xs, wg, wu, wd):
    p_rows, d = xs.shape
    n_tiles = p_rows // TM_E

    def row_idx(i, te, nv):
        return (jnp.minimum(i, nv[0] - 1), 0)

    return pl.pallas_call(
        _moe_body,
        grid_spec=pltpu.PrefetchScalarGridSpec(
            num_scalar_prefetch=2,
            grid=(n_tiles,),
            in_specs=[pl.BlockSpec((TM_E, d), row_idx),
                      pl.BlockSpec((1, d, D_FF), lambda i, te, nv: (te[i], 0, 0)),
                      pl.BlockSpec((1, d, D_FF), lambda i, te, nv: (te[i], 0, 0)),
                      pl.BlockSpec((1, D_FF, d), lambda i, te, nv: (te[i], 0, 0))],
            out_specs=pl.BlockSpec((TM_E, d), row_idx),
        ),
        out_shape=jax.ShapeDtypeStruct((p_rows, d), _F32),
        compiler_params=_cparams(("arbitrary",), VMEM_LIMIT),
        name="moe_experts",
    )(tile_expert, n_valid, xs, wg, wu, wd)


def _col_from_row(row):
    blk = jnp.broadcast_to(row, (128, 128))
    return blk.T[:, 0:1]


def _combine_body(pos_ref, w_ref, x1_ref, g2_ref, fn_ref, ys_ref, o_ref, buf, sem, *, final):
    def row_copy(t, k):
        return pltpu.make_async_copy(ys_ref.at[pl.ds(pos_ref[0, k, t], 1)], buf.at[k, pl.ds(t, 1)], sem)

    def issue(t, carry):
        row_copy(t, 0).start()
        row_copy(t, 1).start()
        return carry

    lax.fori_loop(0, TM, issue, 0)

    def drain(t, carry):
        row_copy(t, 0).wait()
        row_copy(t, 1).wait()
        return carry

    lax.fori_loop(0, TM, drain, 0)

    w = w_ref[...]
    cols = []
    for k in range(2):
        cols.append(jnp.concatenate(
            [_col_from_row(w[k:k + 1, h * 128:(h + 1) * 128]) for h in range(TM // 128)], axis=0))
    y = cols[0] * buf[0] + cols[1] * buf[1]
    x2 = x1_ref[...] + g2_ref[0] * y
    if final:
        ms = jnp.mean(x2 * x2, axis=-1, keepdims=True)
        x2 = x2 * lax.rsqrt(ms + EPS) * fn_ref[...]
    o_ref[...] = x2


def _combine_call(pos3, wts, x1, g2, fnorm, ys, t_out, n_lat_tiles, final):
    n_tok, d = x1.shape
    nb = g2.shape[0] - 1
    nt = t_out // TM

    def mod_idx(i):
        return (jnp.where(i % nt < n_lat_tiles, i // nt, nb), 0, 0)

    return pl.pallas_call(
        functools.partial(_combine_body, final=final),
        grid=(n_tok // TM,),
        in_specs=[pl.BlockSpec((1, 2, TM), lambda i: (i, 0, 0), memory_space=pltpu.SMEM),
                  pl.BlockSpec((2, TM), lambda i: (0, i)),
                  pl.BlockSpec((TM, d), lambda i: (i, 0)),
                  pl.BlockSpec((1, 1, d), mod_idx),
                  pl.BlockSpec((1, d), lambda i: (0, 0)),
                  pl.BlockSpec(memory_space=pl.ANY)],
        out_specs=pl.BlockSpec((TM, d), lambda i: (i, 0)),
        out_shape=jax.ShapeDtypeStruct((n_tok, d), _F32),
        scratch_shapes=[pltpu.VMEM((2, TM, d), _F32), pltpu.SemaphoreType.DMA],
        compiler_params=_cparams(("arbitrary",)),
        name="moe_combine",
    )(pos3, wts, x1, g2, fnorm, ys)


def _rope_tables(s_len, l_ctx):
    t = jnp.arange(s_len, dtype=jnp.int32)
    row = (t // GRID_W).astype(_F32)
    col = (t % GRID_W).astype(_F32)
    half = HEAD_DIM // 2
    inv = ROPE_THETA ** (-jnp.arange(0, half, 2, dtype=_F32) / half)
    ar = row[:, None] * inv
    ac = col[:, None] * inv
    cos64 = jnp.concatenate([jnp.cos(ar), jnp.cos(ar), jnp.cos(ac), jnp.cos(ac)], axis=1)
    sin64 = jnp.concatenate([-jnp.sin(ar), jnp.sin(ar), -jnp.sin(ac), jnp.sin(ac)], axis=1)
    cos_t = jnp.concatenate([jnp.tile(cos64, (1, 2)), jnp.ones((l_ctx, 128), _F32)], axis=0)
    sin_t = jnp.concatenate([jnp.tile(sin64, (1, 2)), jnp.zeros((l_ctx, 128), _F32)], axis=0)
    return cos_t, sin_t


def _block_diag_ones():
    r = np.arange(BRANCH)
    return jnp.asarray((r[:, None] // HEAD_DIM) == (r[None, :] // HEAD_DIM), dtype=_BF16)


def kernel(x, c, ctx, c_ctx, w_mod, b_mod, norm1, norm2, w_in, q_norm_a, k_norm_a, sink_b, rpb_n, w_branch, w_out,
           w_router, router_bias, w_e_gate, w_e_up, w_e_down, final_norm):
    b, s_len, d = x.shape
    l_ctx = ctx.shape[1]
    depth = w_in.shape[0]
    t_all = s_len + l_ctx
    rows = s_len // GRID_W
    assert d == D_MODEL and l_ctx == TM and s_len % TM == 0 and b + 1 <= 16
    assert s_len >= TM + 2 * WINDOW and rows >= NA_WIN_ROWS and rows % 4 == 0
    n_lat_tiles = s_len // TM

    cos_t, sin_t = _rope_tables(s_len, l_ctx)
    bd = _block_diag_ones()
    cvec = jnp.zeros((16, d), _F32).at[:b].set(c).at[b].set(c_ctx)
    mods = _mod_call(cvec, w_mod, b_mod)[:, :b + 1].reshape(depth, b + 1, 6, 1, d)

    wr = w_router.T.astype(_F32)
    wrh = wr.astype(_BF16)
    wrl = (wr - wrh.astype(_F32)).astype(_BF16)
    rb = router_bias.astype(_F32).reshape(N_EXPERTS, 1)

    xc = jnp.concatenate([x, ctx], axis=1)
    for l in range(depth):
        last = l == depth - 1
        sh1, sc1, g1, sh2, sc2, g2 = [mods[l, :, n] for n in range(6)]
        w_l = jnp.take(w_in[l], jnp.asarray(_COL_PERM), axis=1).astype(_BF16)
        gq = (jnp.tile(q_norm_a[l].astype(_F32), N_HEADS) * ATTN_SCALE).reshape(1, BRANCH)
        gk = jnp.tile(k_norm_a[l].astype(_F32), 2).reshape(1, KV_A)
        qa, ka, va, qb, kb, vb, qn, kn, vn, gate = _inproj_call(
            xc, n_lat_tiles, norm1[l].reshape(1, d), sh1, sc1, w_l, cos_t, sin_t, gq, gk, bd)

        sink = sink_b[l].astype(_F32)
        oa = _attn_a_call(qa, ka, va, s_len)
        ob = _attn_b_call(qb, kb, vb, sink, s_len)
        on = _attn_n_call(qn, kn, vn, _na_bias_table(rpb_n[l], rows), s_len)
        if not last:
            oa, ob, on = _attn_ctx_call(sink, qa, ka, va, qb, kb, vb, qn, kn, vn, oa, ob, on, s_len)

        t_out = s_len if last else t_all
        wb = jnp.stack([jnp.take(w_branch[l, 0], jnp.asarray(_PERM_Q), axis=0),
                        jnp.take(w_branch[l, 1], jnp.asarray(_PERM_Q), axis=0),
                        w_branch[l, 2]]).astype(_BF16)
        x1, h2, idx, rank, wts, cnt = _merge_call(
            oa, ob, on, gate, xc, wb, w_out[l].astype(_BF16), g1, sh2, sc2, norm2[l].reshape(1, d),
            wrh, wrl, rb, t_out, n_lat_tiles)

        n_tok = b * t_out
        counts = cnt[:, 0]
        padded = ((counts + TM_E - 1) // TM_E) * TM_E
        ends = jnp.cumsum(padded)
        offs = ends - padded
        pos = rank + jnp.sum(jnp.where(idx[None] == jnp.arange(N_EXPERTS)[:, None, None], offs[:, None, None], 0), axis=0)
        pos3 = pos.reshape(2, n_tok // TM, TM).transpose(1, 0, 2)
        p_rows = 2 * n_tok + N_EXPERTS * TM_E
        n_tiles = p_rows // TM_E
        n_valid = (ends[-1] // TM_E).astype(jnp.int32).reshape(1)
        tile_start = jnp.arange(n_tiles, dtype=jnp.int32) * TM_E
        tile_expert = jnp.minimum(jnp.sum(tile_start[:, None] >= ends[None, :], axis=1), N_EXPERTS - 1).astype(jnp.int32)

        xs = _dispatch_call(pos3, h2.reshape(n_tok, d), jnp.zeros((p_rows, d), _F32))
        ys = _moe_call(tile_expert, n_valid, xs, w_e_gate[l].astype(_BF16), w_e_up[l].astype(_BF16),
                       w_e_down[l].astype(_BF16))
        x2 = _combine_call(pos3, wts, x1.reshape(n_tok, d), g2, final_norm.reshape(1, d), ys, t_out, n_lat_tiles, last)
        xc = x2.reshape(b, t_out, d)
    return xc
```

```python
import functools

import numpy as np
import jax
import jax.numpy as jnp
from jax import lax
from jax.experimental import pallas as pl
from jax.experimental.pallas import tpu as pltpu

D_MODEL = 1024
HEAD_DIM = 64
GRID_W = 64
N_HEADS = 8
WINDOW = 128
NA_ROWS = 8
NA_COLS = 16
ROPE_THETA = 10000.0
N_EXPERTS = 16
N_GROUPS = 4
EXPERTS_PER_GROUP = N_EXPERTS // N_GROUPS
D_FF = 512
EPS = 1e-6
NEG_INF = -1e30
ATTN_SCALE = HEAD_DIM ** -0.5
BRANCH = N_HEADS * HEAD_DIM
KV_A = 2 * HEAD_DIM
IN_COLS = 6144

OFF_QA, OFF_KA, OFF_VA = 0, 512, 640
OFF_QB, OFF_KB, OFF_VB = 768, 1280, 1408
OFF_QN, OFF_KN, OFF_VN = 1536, 2048, 2560
OFF_GATE = 3072

TM = 256
TM_E = 512
NA_WIN_ROWS = 12
VMEM_LIMIT = 56 * 1024 * 1024

_F32 = jnp.float32
_BF16 = jnp.bfloat16

_PERM_Q = np.concatenate(
    [np.concatenate([np.arange(64 * j, 64 * j + 64), np.arange(64 * (4 + j), 64 * (4 + j) + 64)]) for j in range(4)]
)
_COL_PERM = np.arange(IN_COLS)
_COL_PERM[OFF_QA:OFF_QA + BRANCH] = OFF_QA + _PERM_Q
_COL_PERM[OFF_QB:OFF_QB + BRANCH] = OFF_QB + _PERM_Q


def _nt(a, b):
    return lax.dot_general(a, b, (((1,), (1,)), ((), ())), preferred_element_type=_F32)


def _dot(a, b):
    return jnp.dot(a, b, preferred_element_type=_F32)


def _cparams(sem, vmem=None):
    return pltpu.CompilerParams(dimension_semantics=sem, vmem_limit_bytes=vmem)


def _mod_body(c_ref, w_ref, b_ref, o_ref):
    cv = c_ref[...]
    a = (cv * jax.nn.sigmoid(cv)).astype(_BF16)
    o_ref[0] = _dot(a, w_ref[0].astype(_BF16)) + b_ref[0]


def _mod_call(cvec, w_mod, b_mod):
    depth, d, n = w_mod.shape
    tn = 1536
    return pl.pallas_call(
        _mod_body,
        grid=(depth, n // tn),
        in_specs=[
            pl.BlockSpec((16, d), lambda l, j: (0, 0)),
            pl.BlockSpec((1, d, tn), lambda l, j: (l, 0, j)),
            pl.BlockSpec((1, 1, tn), lambda l, j: (l, 0, j)),
        ],
        out_specs=pl.BlockSpec((1, 16, tn), lambda l, j: (l, 0, j)),
        out_shape=jax.ShapeDtypeStruct((depth, 16, n), _F32),
        compiler_params=_cparams(("arbitrary", "arbitrary")),
        name="mod_vectors",
    )(cvec, w_mod, b_mod.reshape(depth, 1, n))


def _rope(v, c, s):
    w = v.shape[1]
    lane = lax.broadcasted_iota(jnp.int32, v.shape, 1)
    first = (lane & 16) == 0
    sw = jnp.where(first, pltpu.roll(v, w - 16, 1), pltpu.roll(v, 16, 1))
    return v * c + sw * s


def _inproj_body(x_ref, g_ref, sh_ref, sc_ref, w_ref, cos_ref, sin_ref, gq_ref, gk_ref, bd_ref,
                 qa_ref, ka_ref, va_ref, qb_ref, kb_ref, vb_ref, qn_ref, kn_ref, vn_ref, gate_ref):
    x = x_ref[0]
    ms = jnp.mean(x * x, axis=-1, keepdims=True)
    h = (x * lax.rsqrt(ms + EPS) * g_ref[...]) * (1.0 + sc_ref[0]) + sh_ref[0]
    hb = h.astype(_BF16)

    def seg(off, width):
        return _dot(hb, w_ref[:, off:off + width])

    c1 = cos_ref[...]
    s1 = sin_ref[...]
    c4 = jnp.concatenate([c1] * 4, axis=1)
    s4 = jnp.concatenate([s1] * 4, axis=1)
    bd = bd_ref[...]

    qa = seg(OFF_QA, BRANCH)
    ssq = _dot((qa * qa).astype(_BF16), bd)
    qa = qa * lax.rsqrt(ssq * (1.0 / HEAD_DIM) + EPS) * gq_ref[...]
    qa_ref[0] = _rope(qa, c4, s4).astype(_BF16)
    ka = seg(OFF_KA, KV_A)
    ssk = _dot((ka * ka).astype(_BF16), bd_ref[:KV_A, :KV_A])
    ka = ka * lax.rsqrt(ssk * (1.0 / HEAD_DIM) + EPS) * gk_ref[...]
    ka_ref[0] = _rope(ka, c1, s1).astype(_BF16)
    va_ref[0] = seg(OFF_VA, KV_A).astype(_BF16)
    qb_ref[0] = _rope(seg(OFF_QB, BRANCH) * ATTN_SCALE, c4, s4).astype(_BF16)
    kb_ref[0] = _rope(seg(OFF_KB, KV_A), c1, s1).astype(_BF16)
    vb_ref[0] = seg(OFF_VB, KV_A).astype(_BF16)
    qn_ref[0] = (seg(OFF_QN, BRANCH) * ATTN_SCALE).astype(_BF16)
    kn_ref[0] = seg(OFF_KN, BRANCH).astype(_BF16)
    vn_ref[0] = seg(OFF_VN, BRANCH).astype(_BF16)
    for n in range(3):
        gate_ref[0, :, n * D_MODEL:(n + 1) * D_MODEL] = seg(OFF_GATE + n * D_MODEL, D_MODEL).astype(_BF16)


def _inproj_call(xc, n_lat_tiles, g1n, sh, sc, w, cos_t, sin_t, gq, gk, bd):
    b, t, d = xc.shape
    nb = sh.shape[0] - 1
    nt = t // TM

    def mod_idx(bi, i):
        return (jnp.where(i < n_lat_tiles, bi, nb), 0, 0)

    def tok(width):
        return pl.BlockSpec((1, TM, width), lambda bi, i: (bi, i, 0))

    widths = [BRANCH, KV_A, KV_A, BRANCH, KV_A, KV_A, BRANCH, BRANCH, BRANCH, 3 * D_MODEL]
    return pl.pallas_call(
        _inproj_body,
        grid=(b, nt),
        in_specs=[
            tok(d),
            pl.BlockSpec((1, d), lambda bi, i: (0, 0)),
            pl.BlockSpec((1, 1, d), mod_idx),
            pl.BlockSpec((1, 1, d), mod_idx),
            pl.BlockSpec((d, IN_COLS), lambda bi, i: (0, 0)),
            pl.BlockSpec((TM, 128), lambda bi, i: (i, 0)),
            pl.BlockSpec((TM, 128), lambda bi, i: (i, 0)),
            pl.BlockSpec((1, BRANCH), lambda bi, i: (0, 0)),
            pl.BlockSpec((1, KV_A), lambda bi, i: (0, 0)),
            pl.BlockSpec((BRANCH, BRANCH), lambda bi, i: (0, 0)),
        ],
        out_specs=[tok(wd) for wd in widths],
        out_shape=[jax.ShapeDtypeStruct((b, t, wd), _BF16) for wd in widths],
        compiler_params=_cparams(("arbitrary", "arbitrary"), VMEM_LIMIT),
        name="inproj",
    )(xc, g1n, sh, sc, w, cos_t, sin_t, gq, gk, bd)


def _stack_heads(q):
    lane = lax.broadcasted_iota(jnp.int32, q.shape, 1)
    lo = lane < HEAD_DIM
    zero = jnp.zeros_like(q)
    return jnp.concatenate([jnp.where(lo, q, zero), jnp.where(lo, zero, q)], axis=0)


def _unstack_heads(o):
    half = o.shape[0] // 2
    lane = lax.broadcasted_iota(jnp.int32, (half, o.shape[1]), 1)
    return jnp.where(lane < HEAD_DIM, o[:half], o[half:])


KA_CHUNK = 512


def _flash_update(lhs, kc, vc, m, l128, acc):
    s = _nt(lhs, kc)
    tiles = [s[:, t * 128:(t + 1) * 128] for t in range(s.shape[1] // 128)]
    m_new = jnp.maximum(m, jnp.max(functools.reduce(jnp.maximum, tiles), axis=1, keepdims=True))
    alpha = jnp.exp(m - m_new)
    ps = [jnp.exp(t - m_new) for t in tiles]
    l128 = alpha * l128 + functools.reduce(lambda a, c: a + c, ps)
    p = jnp.concatenate([x.astype(_BF16) for x in ps], axis=1)
    acc = alpha * acc + _dot(p, vc)
    return m_new, l128, acc


def _attn_a_body(q_ref, k_ref, v_ref, o_ref, *, s_len, l_ctx, with_ctx):
    def run(chunks):
        lhs = _stack_heads(q_ref[0])
        m = jnp.full((2 * TM, 128), NEG_INF, _F32)
        l128 = jnp.zeros((2 * TM, 128), _F32)
        acc = jnp.zeros((2 * TM, 128), _F32)
        for start, size in chunks:
            m, l128, acc = _flash_update(lhs, k_ref[0, start:start + size, :], v_ref[0, start:start + size, :],
                                         m, l128, acc)
        l = jnp.sum(l128, axis=1, keepdims=True)
        o_ref[0] = _unstack_heads(acc / l).astype(_BF16)

    ctx_chunk = (s_len, l_ctx)
    lat_chunks = [(c * KA_CHUNK, KA_CHUNK) for c in range(s_len // KA_CHUNK)] + [ctx_chunk]
    if with_ctx:
        is_ctx = pl.program_id(1) == s_len // TM
        pl.when(jnp.logical_not(is_ctx))(lambda: run(lat_chunks))
        pl.when(is_ctx)(lambda: run([ctx_chunk]))
    else:
        run(lat_chunks)


def _attn_a_call(q, k, v, s_len, with_ctx):
    b, t, _ = q.shape
    assert s_len % KA_CHUNK == 0
    nq = s_len // TM + (1 if with_ctx else 0)
    return pl.pallas_call(
        functools.partial(_attn_a_body, s_len=s_len, l_ctx=t - s_len, with_ctx=with_ctx),
        grid=(b, nq, 4),
        in_specs=[
            pl.BlockSpec((1, TM, 128), lambda bi, i, j: (bi, i, j)),
            pl.BlockSpec((1, t, 128), lambda bi, i, j: (bi, 0, 0)),
            pl.BlockSpec((1, t, 128), lambda bi, i, j: (bi, 0, 0)),
        ],
        out_specs=pl.BlockSpec((1, TM, 128), lambda bi, i, j: (bi, i, j)),
        out_shape=jax.ShapeDtypeStruct((b, nq * TM, BRANCH), _BF16),
        compiler_params=_cparams(("arbitrary", "arbitrary", "arbitrary"), VMEM_LIMIT),
        name="attn_global",
    )(q, k, v)


def _softmax_parts(parts, extra=None):
    m = functools.reduce(jnp.maximum, [jnp.max(s, axis=1, keepdims=True) for s in parts])
    if extra is not None:
        m = jnp.maximum(m, extra)
    ps = [jnp.exp(s - m) for s in parts]
    l = functools.reduce(lambda a, c: a + c, [jnp.sum(p, axis=1, keepdims=True) for p in ps])
    if extra is not None:
        l = l + jnp.exp(extra - m)
    return ps, l


def _attn_b_body(sink_ref, q_ref, k_ref, v_ref, o_ref, *, s_len, l_ctx, with_ctx):
    span = TM + 2 * WINDOW
    i = pl.program_id(1)
    is_hi = lax.broadcasted_iota(jnp.int32, (2 * TM, 1), 0) >= TM

    def finish(j, ps, vs, l):
        o = functools.reduce(lambda a, c: a + c, [_dot(p.astype(_BF16), v) for p, v in zip(ps, vs)]) / l
        o_ref[0, :, j * 128:(j + 1) * 128] = _unstack_heads(o).astype(_BF16)

    def latent():
        start = i * TM
        ws = pl.multiple_of(jnp.clip(start - WINDOW, 0, s_len - span), WINDOW)
        kw = k_ref[0, pl.ds(ws, span), :]
        vw = v_ref[0, pl.ds(ws, span), :]
        kc = k_ref[0, s_len:s_len + l_ctx, :]
        vc = v_ref[0, s_len:s_len + l_ctx, :]
        row = lax.broadcasted_iota(jnp.int32, (2 * TM, span), 0)
        qpos = start + jnp.where(row >= TM, row - TM, row)
        kpos = ws + lax.broadcasted_iota(jnp.int32, (2 * TM, span), 1)
        band = jnp.abs(kpos - qpos) <= WINDOW
        for j in range(4):
            lhs = _stack_heads(q_ref[0, :, j * 128:(j + 1) * 128])
            s_loc = jnp.where(band, _nt(lhs, kw), NEG_INF)
            s_ctx = _nt(lhs, kc)
            sink = jnp.where(is_hi, sink_ref[4 + j], sink_ref[j])
            ps, l = _softmax_parts([s_loc, s_ctx], sink)
            finish(j, ps, (vw, vc), l)

    def context():
        kc = k_ref[0, s_len:s_len + l_ctx, :]
        vc = v_ref[0, s_len:s_len + l_ctx, :]
        for j in range(4):
            lhs = _stack_heads(q_ref[0, :, j * 128:(j + 1) * 128])
            sink = jnp.where(is_hi, sink_ref[4 + j], sink_ref[j])
            ps, l = _softmax_parts([_nt(lhs, kc)], sink)
            finish(j, ps, (vc,), l)

    if with_ctx:
        is_ctx = i == s_len // TM
        pl.when(jnp.logical_not(is_ctx))(latent)
        pl.when(is_ctx)(context)
    else:
        latent()


def _attn_b_call(q, k, v, sink, s_len, with_ctx):
    b, t, _ = q.shape
    nq = s_len // TM + (1 if with_ctx else 0)
    return pl.pallas_call(
        functools.partial(_attn_b_body, s_len=s_len, l_ctx=t - s_len, with_ctx=with_ctx),
        grid=(b, nq),
        in_specs=[
            pl.BlockSpec(memory_space=pltpu.SMEM),
            pl.BlockSpec((1, TM, BRANCH), lambda bi, i: (bi, i, 0)),
            pl.BlockSpec((1, t, 128), lambda bi, i: (bi, 0, 0)),
            pl.BlockSpec((1, t, 128), lambda bi, i: (bi, 0, 0)),
        ],
        out_specs=pl.BlockSpec((1, TM, BRANCH), lambda bi, i: (bi, i, 0)),
        out_shape=jax.ShapeDtypeStruct((b, nq * TM, BRANCH), _BF16),
        compiler_params=_cparams(("arbitrary", "arbitrary"), VMEM_LIMIT),
        name="attn_window",
    )(sink, q, k, v)


def _attn_n_body(q_ref, k_ref, v_ref, bias_ref, o_ref, *, s_len, l_ctx, with_ctx):
    rows = s_len // GRID_W
    span = NA_WIN_ROWS * GRID_W
    r = pl.program_id(1)

    def finish(j, ps, vs, l):
        o = functools.reduce(lambda a, c: a + c, [_dot(p.astype(_BF16), v) for p, v in zip(ps, vs)]) / l
        o_ref[0, :, j * 128:(j + 1) * 128] = _unstack_heads(o).astype(_BF16)

    def latent():
        u0 = jnp.clip(4 * r - 4, 0, rows - NA_WIN_ROWS)
        ws = pl.multiple_of(u0 * GRID_W, GRID_W)
        for j in range(4):
            sl = slice(j * 128, (j + 1) * 128)
            kw = k_ref[0, pl.ds(ws, span), sl]
            vw = v_ref[0, pl.ds(ws, span), sl]
            kc = k_ref[0, s_len:s_len + l_ctx, sl]
            vc = v_ref[0, s_len:s_len + l_ctx, sl]
            lhs = _stack_heads(q_ref[0, :, sl])
            s_loc = _nt(lhs, kw) + bias_ref[0, j]
            s_ctx = _nt(lhs, kc)
            ps, l = _softmax_parts([s_loc, s_ctx])
            finish(j, ps, (vw, vc), l)

    def context():
        for j in range(4):
            sl = slice(j * 128, (j + 1) * 128)
            lhs = _stack_heads(q_ref[0, :, sl])
            ps, l = _softmax_parts([_nt(lhs, k_ref[0, s_len:s_len + l_ctx, sl])])
            finish(j, ps, (v_ref[0, s_len:s_len + l_ctx, sl],), l)

    if with_ctx:
        is_ctx = r == s_len // TM
        pl.when(jnp.logical_not(is_ctx))(latent)
        pl.when(is_ctx)(context)
    else:
        latent()


def _attn_n_call(q, k, v, bias_tbl, s_len, with_ctx):
    b, t, _ = q.shape
    n_r = s_len // TM
    nq = n_r + (1 if with_ctx else 0)
    span = NA_WIN_ROWS * GRID_W

    def case(ri):
        return jnp.where(ri == 0, 0, jnp.where(ri >= n_r - 1, 2, 1))

    return pl.pallas_call(
        functools.partial(_attn_n_body, s_len=s_len, l_ctx=t - s_len, with_ctx=with_ctx),
        grid=(b, nq),
        in_specs=[
            pl.BlockSpec((1, TM, BRANCH), lambda bi, ri: (bi, ri, 0)),
            pl.BlockSpec((1, t, BRANCH), lambda bi, ri: (bi, 0, 0)),
            pl.BlockSpec((1, t, BRANCH), lambda bi, ri: (bi, 0, 0)),
            pl.BlockSpec((1, 4, 2 * TM, span), lambda bi, ri: (case(ri), 0, 0, 0)),
        ],
        out_specs=pl.BlockSpec((1, TM, BRANCH), lambda bi, ri: (bi, ri, 0)),
        out_shape=jax.ShapeDtypeStruct((b, nq * TM, BRANCH), _BF16),
        compiler_params=_cparams(("arbitrary", "arbitrary"), VMEM_LIMIT),
        name="attn_neighbourhood",
    )(q, k, v, bias_tbl)


def _na_tables(rows):
    a = np.arange(4)[:, None, None, None]
    c = np.arange(GRID_W)[None, :, None, None]
    i = np.arange(NA_WIN_ROWS)[None, None, :, None]
    kc = np.arange(GRID_W)[None, None, None, :]
    c0 = np.clip(c - NA_COLS // 2, 0, GRID_W - NA_COLS)
    col_ok = (kc >= c0) & (kc < c0 + NA_COLS)
    cidx = kc - c + (NA_COLS - 1)
    row_ok = [i < NA_ROWS + 0 * a, (i >= a) & (i < a + NA_ROWS), i >= NA_WIN_ROWS - NA_ROWS + 0 * a]
    ridx = [i - a + 7, i - a + 3, i - a - 1]
    shape = (4, GRID_W, NA_WIN_ROWS, GRID_W)
    sel_r, sel_c, ok = [], [], []
    for cs in range(3):
        valid = np.broadcast_to(row_ok[cs] & col_ok, shape)
        ok.append(valid.reshape(4 * GRID_W, NA_WIN_ROWS * GRID_W))
        rsel = (np.broadcast_to(ridx[cs], (4, 1, NA_WIN_ROWS, 1))[:, 0, :, 0][..., None]
                == np.arange(2 * NA_ROWS - 1)[None, None, :])
        sel_r.append(rsel.astype(np.float32))
    csel = (np.broadcast_to(cidx, (1, GRID_W, 1, GRID_W))[0, :, 0, :][..., None]
            == np.arange(2 * NA_COLS - 1)[None, None, :]).astype(np.float32)
    return np.stack(sel_r), csel, np.stack(ok)


def _na_bias_table(rpb, rows):
    sel_r, sel_c, ok = _na_tables(rows)
    hi = lax.Precision.HIGHEST
    t = jnp.einsum("sair,hrg->shaig", jnp.asarray(sel_r), rpb.astype(_F32), precision=hi)
    t = jnp.einsum("shaig,ckg->shacik", t, jnp.asarray(sel_c), precision=hi)
    t = t.reshape(3, N_HEADS, 4 * GRID_W, NA_WIN_ROWS * GRID_W)
    t = jnp.where(jnp.asarray(ok)[:, None], t, NEG_INF)
    return t.reshape(3, 4, 2 * TM, NA_WIN_ROWS * GRID_W)


def _route_rows(sel, sc):
    gs, gi1, gi2, gs1, gs2 = [], [], [], [], []
    for g in range(N_GROUPS):
        v = sel[4 * g:4 * g + 4]
        s = sc[4 * g:4 * g + 4]
        m1 = jnp.maximum(jnp.maximum(v[0], v[1]), jnp.maximum(v[2], v[3]))
        i1 = jnp.where(v[0] == m1, 0, jnp.where(v[1] == m1, 1, jnp.where(v[2] == m1, 2, 3)))
        rest = [jnp.where(i1 == k, NEG_INF, v[k]) for k in range(4)]
        m2 = jnp.maximum(jnp.maximum(rest[0], rest[1]), jnp.maximum(rest[2], rest[3]))
        i2 = jnp.where(rest[0] == m2, 0, jnp.where(rest[1] == m2, 1, jnp.where(rest[2] == m2, 2, 3)))

        def pick(idx, s=s):
            return jnp.where(idx == 0, s[0], jnp.where(idx == 1, s[1], jnp.where(idx == 2, s[2], s[3])))

        gs.append(m1 + m2)
        gi1.append(i1 + 4 * g)
        gi2.append(i2 + 4 * g)
        gs1.append(pick(i1))
        gs2.append(pick(i2))
    best = jnp.maximum(jnp.maximum(gs[0], gs[1]), jnp.maximum(gs[2], gs[3]))
    gsel = jnp.where(gs[0] == best, 0, jnp.where(gs[1] == best, 1, jnp.where(gs[2] == best, 2, 3)))

    def pickg(vals):
        return jnp.where(gsel == 0, vals[0], jnp.where(gsel == 1, vals[1], jnp.where(gsel == 2, vals[2], vals[3])))

    return pickg(gi1), pickg(gi2), pickg(gs1), pickg(gs2)


def _merge_body(oa_ref, ob_ref, on_ref, gate_ref, x_ref, wb_ref, wo_ref, g1_ref, sh2_ref, sc2_ref, n2_ref,
                wrh_ref, wrl_ref, rb_ref,
                x1_ref, h2_ref, idx_ref, rank_ref, wts_ref, cnt_ref, carry_ref):
    first = (pl.program_id(0) == 0) & (pl.program_id(1) == 0)

    @pl.when(first)
    def _():
        carry_ref[...] = jnp.zeros_like(carry_ref)

    outs = (oa_ref, ob_ref, on_ref)
    y = None
    for n in range(3):
        gt = jax.nn.sigmoid(gate_ref[0, :, n * D_MODEL:(n + 1) * D_MODEL].astype(_F32))
        term = gt * _dot(outs[n][0], wb_ref[n])
        y = term if y is None else y + term
    z = _dot(y.astype(_BF16), wo_ref[...])
    x1 = x_ref[0] + g1_ref[0] * z
    x1_ref[0] = x1
    ms = jnp.mean(x1 * x1, axis=-1, keepdims=True)
    h2 = (x1 * lax.rsqrt(ms + EPS) * n2_ref[...]) * (1.0 + sc2_ref[0]) + sh2_ref[0]
    h2_ref[0] = h2

    hb = h2.astype(_BF16)
    hl = (h2 - hb.astype(_F32)).astype(_BF16)
    logits = _nt(wrh_ref[...], hb) + _nt(wrh_ref[...], hl) + _nt(wrl_ref[...], hb)
    scores = jax.nn.sigmoid(logits)
    sel = scores + rb_ref[...]
    sel_rows = [sel[e:e + 1, :] for e in range(N_EXPERTS)]
    sc_rows = [scores[e:e + 1, :] for e in range(N_EXPERTS)]
    e1, e2, s1, s2 = _route_rows(sel_rows, sc_rows)
    den = s1 + s2
    idx_ref[0:1, :] = e1
    idx_ref[1:2, :] = e2
    wts_ref[0:1, :] = s1 / den
    wts_ref[1:2, :] = s2 / den

    eid = lax.broadcasted_iota(jnp.int32, (N_EXPERTS, TM), 0)
    oh1 = eid == e1
    oh2 = eid == e2
    tri = (lax.broadcasted_iota(jnp.int32, (TM, TM), 0) <= lax.broadcasted_iota(jnp.int32, (TM, TM), 1))
    tri = tri.astype(_BF16)
    cum1 = _dot(oh1.astype(_BF16), tri)
    cum2 = _dot(oh2.astype(_BF16), tri)
    tot1 = cum1[:, TM - 1:TM]
    tot2 = cum2[:, TM - 1:TM]
    carry = carry_ref[:, 0:1]
    r1 = jnp.sum(jnp.where(oh1, carry + cum1 - 1.0, 0.0), axis=0, keepdims=True)
    r2 = jnp.sum(jnp.where(oh2, carry + tot1 + cum2 - 1.0, 0.0), axis=0, keepdims=True)
    rank_ref[0:1, :] = r1.astype(jnp.int32)
    rank_ref[1:2, :] = r2.astype(jnp.int32)
    new_carry = carry + tot1 + tot2
    carry_ref[...] = jnp.broadcast_to(new_carry, carry_ref.shape)
    cnt_ref[...] = jnp.broadcast_to(new_carry, cnt_ref.shape).astype(jnp.int32)


def _merge_call(oa, ob, on, gate, xc, wb, wo, g1, sh2, sc2, n2, wrh, wrl, rb, t_out, n_lat_tiles):
    b, t, d = xc.shape
    nb = g1.shape[0] - 1
    nt = t_out // TM
    n_tok = b * t_out

    def mod_idx(bi, i):
        return (jnp.where(i < n_lat_tiles, bi, nb), 0, 0)

    def tok(width):
        return pl.BlockSpec((1, TM, width), lambda bi, i: (bi, i, 0))

    def full(shape):
        return pl.BlockSpec(shape, lambda bi, i: (0,) * len(shape))

    lane_out = pl.BlockSpec((2, TM), lambda bi, i: (0, bi * nt + i))
    return pl.pallas_call(
        _merge_body,
        grid=(b, nt),
        in_specs=[tok(BRANCH), tok(BRANCH), tok(BRANCH), tok(3 * d), tok(d),
                  full((3, BRANCH, d)), full((d, d)),
                  pl.BlockSpec((1, 1, d), mod_idx), pl.BlockSpec((1, 1, d), mod_idx),
                  pl.BlockSpec((1, 1, d), mod_idx), full((1, d)),
                  full((N_EXPERTS, d)), full((N_EXPERTS, d)), full((N_EXPERTS, 1))],
        out_specs=[tok(d), tok(d), lane_out, lane_out, lane_out, full((N_EXPERTS, 128))],
        out_shape=[jax.ShapeDtypeStruct((b, t_out, d), _F32), jax.ShapeDtypeStruct((b, t_out, d), _F32),
                   jax.ShapeDtypeStruct((2, n_tok), jnp.int32), jax.ShapeDtypeStruct((2, n_tok), jnp.int32),
                   jax.ShapeDtypeStruct((2, n_tok), _F32), jax.ShapeDtypeStruct((N_EXPERTS, 128), jnp.int32)],
        scratch_shapes=[pltpu.VMEM((N_EXPERTS, 128), _F32)],
        compiler_params=_cparams(("arbitrary", "arbitrary"), VMEM_LIMIT),
        name="merge_route",
    )(oa, ob, on, gate, xc, wb, wo, g1, sh2, sc2, n2, wrh, wrl, rb)


ROW_DMA_UNROLL = 8


def _dispatch_body(pos_ref, h_ref, xs_in, xs_ref, sem):
    del xs_in

    def row_copy(t, k):
        return pltpu.make_async_copy(h_ref.at[pl.ds(t, 1)], xs_ref.at[pl.ds(pos_ref[0, k, t], 1)], sem)

    def issue(t, carry):
        row_copy(t, 0).start()
        row_copy(t, 1).start()
        return carry

    lax.fori_loop(0, TM, issue, 0, unroll=ROW_DMA_UNROLL)

    def drain(t, carry):
        row_copy(t, 0).wait()
        row_copy(t, 1).wait()
        return carry

    lax.fori_loop(0, TM, drain, 0, unroll=ROW_DMA_UNROLL)


def _dispatch_call(pos3, h2, xs_zero):
    n_tiles = pos3.shape[0]
    anyspec = pl.BlockSpec(memory_space=pl.ANY)
    return pl.pallas_call(
        _dispatch_body,
        grid=(n_tiles,),
        in_specs=[pl.BlockSpec((1, 2, TM), lambda i: (i, 0, 0), memory_space=pltpu.SMEM),
                  pl.BlockSpec((TM, h2.shape[1]), lambda i: (i, 0)), anyspec],
        out_specs=anyspec,
        out_shape=jax.ShapeDtypeStruct(xs_zero.shape, xs_zero.dtype),
        input_output_aliases={2: 0},
        scratch_shapes=[pltpu.SemaphoreType.DMA],
        compiler_params=_cparams(("arbitrary",)),
        name="moe_dispatch",
    )(pos3, h2, xs_zero)


def _moe_body(te_ref, nv_ref, xs_ref, wg_ref, wu_ref, wd_ref, ys_ref):
    del te_ref
    live = pl.program_id(0) < nv_ref[0]

    @pl.when(live)
    def _():
        xb = xs_ref[...].astype(_BF16)
        gate = _dot(xb, wg_ref[0])
        up = _dot(xb, wu_ref[0])
        act = (gate * jax.nn.sigmoid(gate)) * up
        ys_ref[...] = _dot(act.astype(_BF16), wd_ref[0])

    @pl.when(jnp.logical_not(live))
    def _():
        ys_ref[...] = jnp.zeros_like(ys_ref)


def _moe_call(tile_expert, n_valid, xs, wg, wu, wd):
    p_rows, d = xs.shape
    n_tiles = p_rows // TM_E

    def row_idx(i, te, nv):
        return (jnp.minimum(i, nv[0] - 1), 0)

    def out_idx(i, te, nv):
        return (i, 0)

    return pl.pallas_call(
        _moe_body,
        grid_spec=pltpu.PrefetchScalarGridSpec(
            num_scalar_prefetch=2,
            grid=(n_tiles,),
            in_specs=[pl.BlockSpec((TM_E, d), row_idx),
                      pl.BlockSpec((1, d, D_FF), lambda i, te, nv: (te[i], 0, 0)),
                      pl.BlockSpec((1, d, D_FF), lambda i, te, nv: (te[i], 0, 0)),
                      pl.BlockSpec((1, D_FF, d), lambda i, te, nv: (te[i], 0, 0))],
            out_specs=pl.BlockSpec((TM_E, d), out_idx),
        ),
        out_shape=jax.ShapeDtypeStruct((p_rows, d), _F32),
        compiler_params=_cparams(("arbitrary",), VMEM_LIMIT),
        name="moe_experts",
    )(tile_expert, n_valid, xs, wg, wu, wd)


def _col_from_row(row):
    blk = jnp.broadcast_to(row, (128, 128))
    return blk.T[:, 0:1]


def _combine_body(pos_ref, w_ref, x1_ref, g2_ref, fn_ref, ys_ref, o_ref, buf, sem, *, final):
    def row_copy(t, k):
        return pltpu.make_async_copy(ys_ref.at[pl.ds(pos_ref[0, k, t], 1)], buf.at[k, pl.ds(t, 1)], sem)

    def issue(t, carry):
        row_copy(t, 0).start()
        row_copy(t, 1).start()
        return carry

    lax.fori_loop(0, TM, issue, 0, unroll=ROW_DMA_UNROLL)

    def drain(t, carry):
        row_copy(t, 0).wait()
        row_copy(t, 1).wait()
        return carry

    lax.fori_loop(0, TM, drain, 0, unroll=ROW_DMA_UNROLL)

    w = w_ref[...]
    cols = []
    for k in range(2):
        cols.append(jnp.concatenate(
            [_col_from_row(w[k:k + 1, h * 128:(h + 1) * 128]) for h in range(TM // 128)], axis=0))
    y = cols[0] * buf[0] + cols[1] * buf[1]
    x2 = x1_ref[...] + g2_ref[0] * y
    if final:
        ms = jnp.mean(x2 * x2, axis=-1, keepdims=True)
        x2 = x2 * lax.rsqrt(ms + EPS) * fn_ref[...]
    o_ref[...] = x2


def _combine_call(pos3, wts, x1, g2, fnorm, ys, t_out, n_lat_tiles, final):
    n_tok, d = x1.shape
    nb = g2.shape[0] - 1
    nt = t_out // TM

    def mod_idx(i):
        return (jnp.where(i % nt < n_lat_tiles, i // nt, nb), 0, 0)

    return pl.pallas_call(
        functools.partial(_combine_body, final=final),
        grid=(n_tok // TM,),
        in_specs=[pl.BlockSpec((1, 2, TM), lambda i: (i, 0, 0), memory_space=pltpu.SMEM),
                  pl.BlockSpec((2, TM), lambda i: (0, i)),
                  pl.BlockSpec((TM, d), lambda i: (i, 0)),
                  pl.BlockSpec((1, 1, d), mod_idx),
                  pl.BlockSpec((1, d), lambda i: (0, 0)),
                  pl.BlockSpec(memory_space=pl.ANY)],
        out_specs=pl.BlockSpec((TM, d), lambda i: (i, 0)),
        out_shape=jax.ShapeDtypeStruct((n_tok, d), _F32),
        scratch_shapes=[pltpu.VMEM((2, TM, d), _F32), pltpu.SemaphoreType.DMA],
        compiler_params=_cparams(("arbitrary",)),
        name="moe_combine",
    )(pos3, wts, x1, g2, fnorm, ys)


def _rope_tables(s_len, l_ctx):
    t = jnp.arange(s_len, dtype=jnp.int32)
    row = (t // GRID_W).astype(_F32)
    col = (t % GRID_W).astype(_F32)
    half = HEAD_DIM // 2
    inv = ROPE_THETA ** (-jnp.arange(0, half, 2, dtype=_F32) / half)
    ar = row[:, None] * inv
    ac = col[:, None] * inv
    cos64 = jnp.concatenate([jnp.cos(ar), jnp.cos(ar), jnp.cos(ac), jnp.cos(ac)], axis=1)
    sin64 = jnp.concatenate([-jnp.sin(ar), jnp.sin(ar), -jnp.sin(ac), jnp.sin(ac)], axis=1)
    cos_t = jnp.concatenate([jnp.tile(cos64, (1, 2)), jnp.ones((l_ctx, 128), _F32)], axis=0)
    sin_t = jnp.concatenate([jnp.tile(sin64, (1, 2)), jnp.zeros((l_ctx, 128), _F32)], axis=0)
    return cos_t, sin_t


def _block_diag_ones():
    r = np.arange(BRANCH)
    return jnp.asarray((r[:, None] // HEAD_DIM) == (r[None, :] // HEAD_DIM), dtype=_BF16)


def kernel(x, c, ctx, c_ctx, w_mod, b_mod, norm1, norm2, w_in, q_norm_a, k_norm_a, sink_b, rpb_n, w_branch, w_out,
           w_router, router_bias, w_e_gate, w_e_up, w_e_down, final_norm):
    b, s_len, d = x.shape
    l_ctx = ctx.shape[1]
    depth = w_in.shape[0]
    t_all = s_len + l_ctx
    rows = s_len // GRID_W
    assert d == D_MODEL and l_ctx == TM and s_len % TM == 0 and b + 1 <= 16
    assert s_len >= TM + 2 * WINDOW and rows >= NA_WIN_ROWS and rows % 4 == 0
    n_lat_tiles = s_len // TM

    cos_t, sin_t = _rope_tables(s_len, l_ctx)
    bd = _block_diag_ones()
    cvec = jnp.zeros((16, d), _F32).at[:b].set(c).at[b].set(c_ctx)
    mods = _mod_call(cvec, w_mod, b_mod)[:, :b + 1].reshape(depth, b + 1, 6, 1, d)

    wr = w_router.T.astype(_F32)
    wrh = wr.astype(_BF16)
    wrl = (wr - wrh.astype(_F32)).astype(_BF16)
    rb = router_bias.astype(_F32).reshape(N_EXPERTS, 1)

    xc = jnp.concatenate([x, ctx], axis=1)
    for l in range(depth):
        last = l == depth - 1
        sh1, sc1, g1, sh2, sc2, g2 = [mods[l, :, n] for n in range(6)]
        w_l = jnp.take(w_in[l], jnp.asarray(_COL_PERM), axis=1).astype(_BF16)
        gq = (jnp.tile(q_norm_a[l].astype(_F32), N_HEADS) * ATTN_SCALE).reshape(1, BRANCH)
        gk = jnp.tile(k_norm_a[l].astype(_F32), 2).reshape(1, KV_A)
        qa, ka, va, qb, kb, vb, qn, kn, vn, gate = _inproj_call(
            xc, n_lat_tiles, norm1[l].reshape(1, d), sh1, sc1, w_l, cos_t, sin_t, gq, gk, bd)

        sink = sink_b[l].astype(_F32)
        oa = _attn_a_call(qa, ka, va, s_len, not last)
        ob = _attn_b_call(qb, kb, vb, sink, s_len, not last)
        on = _attn_n_call(qn, kn, vn, _na_bias_table(rpb_n[l], rows), s_len, not last)

        t_out = s_len if last else t_all
        wb = jnp.stack([jnp.take(w_branch[l, 0], jnp.asarray(_PERM_Q), axis=0),
                        jnp.take(w_branch[l, 1], jnp.asarray(_PERM_Q), axis=0),
                        w_branch[l, 2]]).astype(_BF16)
        x1, h2, idx, rank, wts, cnt = _merge_call(
            oa, ob, on, gate, xc, wb, w_out[l].astype(_BF16), g1, sh2, sc2, norm2[l].reshape(1, d),
            wrh, wrl, rb, t_out, n_lat_tiles)

        n_tok = b * t_out
        counts = cnt[:, 0]
        padded = ((counts + TM_E - 1) // TM_E) * TM_E
        ends = jnp.cumsum(padded)
        offs = ends - padded
        pos = rank + jnp.sum(jnp.where(idx[None] == jnp.arange(N_EXPERTS)[:, None, None], offs[:, None, None], 0), axis=0)
        pos3 = pos.reshape(2, n_tok // TM, TM).transpose(1, 0, 2)
        p_rows = 2 * n_tok + N_EXPERTS * TM_E
        n_tiles = p_rows // TM_E
        n_valid = (ends[-1] // TM_E).astype(jnp.int32).reshape(1)
        tile_start = jnp.arange(n_tiles, dtype=jnp.int32) * TM_E
        tile_expert = jnp.minimum(jnp.sum(tile_start[:, None] >= ends[None, :], axis=1), N_EXPERTS - 1).astype(jnp.int32)

        xs = _dispatch_call(pos3, h2.reshape(n_tok, d), jnp.zeros((p_rows, d), _F32))
        ys = _moe_call(tile_expert, n_valid, xs, w_e_gate[l].astype(_BF16), w_e_up[l].astype(_BF16),
                       w_e_down[l].astype(_BF16))
        x2 = _combine_call(pos3, wts, x1.reshape(n_tok, d), g2, final_norm.reshape(1, d), ys, t_out, n_lat_tiles, last)
        xc = x2.reshape(b, t_out, d)
    return xc
```

```python
import functools

import numpy as np
import jax
import jax.numpy as jnp
from jax import lax
from jax.experimental import pallas as pl
from jax.experimental.pallas import tpu as pltpu

D_MODEL = 1024
HEAD_DIM = 64
GRID_W = 64
N_HEADS = 8
WINDOW = 128
NA_ROWS = 8
NA_COLS = 16
ROPE_THETA = 10000.0
N_EXPERTS = 16
N_GROUPS = 4
EXPERTS_PER_GROUP = N_EXPERTS // N_GROUPS
D_FF = 512
EPS = 1e-6
NEG_INF = -1e30
ATTN_SCALE = HEAD_DIM ** -0.5
LOG2_E = 1.4426950408889634
BRANCH = N_HEADS * HEAD_DIM
KV_A = 2 * HEAD_DIM
IN_COLS = 6144

OFF_QA, OFF_KA, OFF_VA = 0, 512, 640
OFF_QB, OFF_KB, OFF_VB = 768, 1280, 1408
OFF_QN, OFF_KN, OFF_VN = 1536, 2048, 2560
OFF_GATE = 3072

TM = 256
TM_E = 512
NA_WIN_ROWS = 12
VMEM_LIMIT = 56 * 1024 * 1024

_F32 = jnp.float32
_BF16 = jnp.bfloat16

def _pair_heads(w, axis):
    shape = w.shape
    split = shape[:axis] + (2, 4, HEAD_DIM) + shape[axis + 1:]
    return jnp.swapaxes(w.reshape(split), axis, axis + 1).reshape(shape)


def _nt(a, b):
    return lax.dot_general(a, b, (((1,), (1,)), ((), ())), preferred_element_type=_F32)


def _dot(a, b):
    return jnp.dot(a, b, preferred_element_type=_F32)


def _cparams(sem, vmem=None):
    return pltpu.CompilerParams(dimension_semantics=sem, vmem_limit_bytes=vmem)


def _mod_body(c_ref, w_ref, b_ref, o_ref):
    cv = c_ref[...]
    a = (cv * jax.nn.sigmoid(cv)).astype(_BF16)
    o_ref[0] = _dot(a, w_ref[0].astype(_BF16)) + b_ref[0]


def _mod_call(cvec, w_mod, b_mod):
    depth, d, n = w_mod.shape
    tn = 1536
    return pl.pallas_call(
        _mod_body,
        grid=(depth, n // tn),
        in_specs=[
            pl.BlockSpec((16, d), lambda l, j: (0, 0)),
            pl.BlockSpec((1, d, tn), lambda l, j: (l, 0, j)),
            pl.BlockSpec((1, 1, tn), lambda l, j: (l, 0, j)),
        ],
        out_specs=pl.BlockSpec((1, 16, tn), lambda l, j: (l, 0, j)),
        out_shape=jax.ShapeDtypeStruct((depth, 16, n), _F32),
        compiler_params=_cparams(("arbitrary", "arbitrary")),
        name="mod_vectors",
    )(cvec, w_mod, b_mod.reshape(depth, 1, n))


def _rope(v, c, s):
    w = v.shape[1]
    lane = lax.broadcasted_iota(jnp.int32, v.shape, 1)
    first = (lane & 16) == 0
    sw = jnp.where(first, pltpu.roll(v, w - 16, 1), pltpu.roll(v, 16, 1))
    return v * c + sw * s


def _inproj_body(x_ref, xctx_ref, g_ref, sh_ref, sc_ref, w_ref, cos_ref, sin_ref, gq_ref, gk_ref, bd_ref,
                 qa_ref, ka_ref, va_ref, qb_ref, kb_ref, vb_ref, qn_ref, kn_ref, vn_ref, gate_ref, *, n_lat_tiles):
    x = jnp.where(pl.program_id(1) < n_lat_tiles, x_ref[0], xctx_ref[0])
    ms = jnp.mean(x * x, axis=-1, keepdims=True)
    h = (x * lax.rsqrt(ms + EPS) * g_ref[...]) * (1.0 + sc_ref[0]) + sh_ref[0]
    hb = h.astype(_BF16)

    def seg(off, width):
        return _dot(hb, w_ref[:, off:off + width])

    c1 = cos_ref[...]
    s1 = sin_ref[...]
    c4 = jnp.concatenate([c1] * 4, axis=1)
    s4 = jnp.concatenate([s1] * 4, axis=1)
    bd = bd_ref[...]

    qa = seg(OFF_QA, BRANCH)
    ssq = _dot((qa * qa).astype(_BF16), bd)
    qa = qa * lax.rsqrt(ssq * (1.0 / HEAD_DIM) + EPS) * gq_ref[...]
    qa_ref[0] = _rope(qa, c4, s4).astype(_BF16)
    ka = seg(OFF_KA, KV_A)
    ssk = _dot((ka * ka).astype(_BF16), bd_ref[:KV_A, :KV_A])
    ka = ka * lax.rsqrt(ssk * (1.0 / HEAD_DIM) + EPS) * gk_ref[...]
    ka_ref[0] = _rope(ka, c1, s1).astype(_BF16)
    va_ref[0] = seg(OFF_VA, KV_A).astype(_BF16)
    qb_ref[0] = _rope(seg(OFF_QB, BRANCH) * ATTN_SCALE, c4, s4).astype(_BF16)
    kb_ref[0] = _rope(seg(OFF_KB, KV_A), c1, s1).astype(_BF16)
    vb_ref[0] = seg(OFF_VB, KV_A).astype(_BF16)
    qn_ref[0] = (seg(OFF_QN, BRANCH) * ATTN_SCALE).astype(_BF16)
    kn_ref[0] = seg(OFF_KN, BRANCH).astype(_BF16)
    vn_ref[0] = seg(OFF_VN, BRANCH).astype(_BF16)
    for n in range(3):
        gate_ref[0, :, n * D_MODEL:(n + 1) * D_MODEL] = seg(OFF_GATE + n * D_MODEL, D_MODEL).astype(_BF16)


def _token_specs(n_lat_tiles, ctx_tile, d):
    lat = pl.BlockSpec((1, TM, d), lambda bi, i: (bi, jnp.minimum(i, n_lat_tiles - 1), 0))
    ctx = pl.BlockSpec((1, TM, d), lambda bi, i: (bi, ctx_tile, 0))
    return lat, ctx


def _inproj_call(x_lat, x_ctx, ctx_tile, n_lat_tiles, g1n, sh, sc, w, cos_t, sin_t, gq, gk, bd):
    b, _, d = x_lat.shape
    nb = sh.shape[0] - 1
    nt = n_lat_tiles + 1
    t = nt * TM

    def mod_idx(bi, i):
        return (jnp.where(i < n_lat_tiles, bi, nb), 0, 0)

    def tok(width):
        return pl.BlockSpec((1, TM, width), lambda bi, i: (bi, i, 0))

    widths = [BRANCH, KV_A, KV_A, BRANCH, KV_A, KV_A, BRANCH, BRANCH, BRANCH, 3 * D_MODEL]
    return pl.pallas_call(
        functools.partial(_inproj_body, n_lat_tiles=n_lat_tiles),
        grid=(b, nt),
        in_specs=[
            *_token_specs(n_lat_tiles, ctx_tile, d),
            pl.BlockSpec((1, d), lambda bi, i: (0, 0)),
            pl.BlockSpec((1, 1, d), mod_idx),
            pl.BlockSpec((1, 1, d), mod_idx),
            pl.BlockSpec((d, IN_COLS), lambda bi, i: (0, 0)),
            pl.BlockSpec((TM, 128), lambda bi, i: (i, 0)),
            pl.BlockSpec((TM, 128), lambda bi, i: (i, 0)),
            pl.BlockSpec((1, BRANCH), lambda bi, i: (0, 0)),
            pl.BlockSpec((1, KV_A), lambda bi, i: (0, 0)),
            pl.BlockSpec((BRANCH, BRANCH), lambda bi, i: (0, 0)),
        ],
        out_specs=[tok(wd) for wd in widths],
        out_shape=[jax.ShapeDtypeStruct((b, t, wd), _BF16) for wd in widths],
        compiler_params=_cparams(("arbitrary", "arbitrary"), VMEM_LIMIT),
        name="inproj",
    )(x_lat, x_ctx, g1n, sh, sc, w, cos_t, sin_t, gq, gk, bd)


def _stack_heads(q):
    lane = lax.broadcasted_iota(jnp.int32, q.shape, 1)
    lo = lane < HEAD_DIM
    zero = jnp.zeros_like(q)
    return jnp.concatenate([jnp.where(lo, q, zero), jnp.where(lo, zero, q)], axis=0)


def _unstack_heads(o):
    half = o.shape[0] // 2
    lane = lax.broadcasted_iota(jnp.int32, (half, o.shape[1]), 1)
    return jnp.where(lane < HEAD_DIM, o[:half], o[half:])


KA_CHUNK_MAX = 2048
KA_GROUPS = 2


def _flash_update(lhs, kc, vc, m, l128, acc):
    s = _nt(lhs, kc)
    tiles = [s[:, t * 128:(t + 1) * 128] for t in range(s.shape[1] // 128)]
    m_new = jnp.maximum(m, jnp.max(functools.reduce(jnp.maximum, tiles), axis=1, keepdims=True))
    alpha = jnp.exp2(m - m_new)
    ps = [jnp.exp2(t - m_new) for t in tiles]
    l128 = alpha * l128 + functools.reduce(lambda a, c: a + c, ps)
    p = jnp.concatenate([x.astype(_BF16) for x in ps], axis=1)
    acc = alpha * acc + _dot(p, vc)
    return m_new, l128, acc


def _attn_a_body(q_ref, k_ref, v_ref, o_ref, *, s_len, l_ctx, with_ctx):
    def run(chunks):
        for g in range(KA_GROUPS):
            sl = slice(g * 128, (g + 1) * 128)
            lhs = _stack_heads(q_ref[0, :, sl])
            m = jnp.full((2 * TM, 128), NEG_INF, _F32)
            l128 = jnp.zeros((2 * TM, 128), _F32)
            acc = jnp.zeros((2 * TM, 128), _F32)
            for start, size in chunks:
                m, l128, acc = _flash_update(lhs, k_ref[0, start:start + size, :], v_ref[0, start:start + size, :],
                                             m, l128, acc)
            l = jnp.sum(l128, axis=1, keepdims=True)
            o_ref[0, :, sl] = _unstack_heads(acc / l).astype(_BF16)

    ctx_chunk = (s_len, l_ctx)
    chunk = KA_CHUNK_MAX
    while s_len % chunk:
        chunk //= 2
    lat_chunks = [(c * chunk, chunk) for c in range(s_len // chunk)] + [ctx_chunk]
    if with_ctx:
        is_ctx = pl.program_id(1) == s_len // TM
        pl.when(jnp.logical_not(is_ctx))(lambda: run(lat_chunks))
        pl.when(is_ctx)(lambda: run([ctx_chunk]))
    else:
        run(lat_chunks)


def _attn_a_call(q, k, v, s_len, with_ctx):
    b, t, _ = q.shape
    nq = s_len // TM + (1 if with_ctx else 0)
    return pl.pallas_call(
        functools.partial(_attn_a_body, s_len=s_len, l_ctx=t - s_len, with_ctx=with_ctx),
        grid=(b, nq, 4 // KA_GROUPS),
        in_specs=[
            pl.BlockSpec((1, TM, 128 * KA_GROUPS), lambda bi, i, j: (bi, i, j)),
            pl.BlockSpec((1, t, 128), lambda bi, i, j: (bi, 0, 0)),
            pl.BlockSpec((1, t, 128), lambda bi, i, j: (bi, 0, 0)),
        ],
        out_specs=pl.BlockSpec((1, TM, 128 * KA_GROUPS), lambda bi, i, j: (bi, i, j)),
        out_shape=jax.ShapeDtypeStruct((b, nq * TM, BRANCH), _BF16),
        compiler_params=_cparams(("arbitrary", "arbitrary", "arbitrary"), VMEM_LIMIT),
        name="attn_global",
    )(q, k, v)


def _softmax_parts(parts, extra=None):
    m = functools.reduce(jnp.maximum, [jnp.max(s, axis=1, keepdims=True) for s in parts])
    if extra is not None:
        m = jnp.maximum(m, extra)
    ps = [jnp.exp(s - m) for s in parts]
    l = functools.reduce(lambda a, c: a + c, [jnp.sum(p, axis=1, keepdims=True) for p in ps])
    if extra is not None:
        l = l + jnp.exp(extra - m)
    return ps, l


def _attn_b_body(sink_ref, q_ref, k_ref, v_ref, o_ref, *, s_len, l_ctx, with_ctx):
    span = TM + 2 * WINDOW
    i = pl.program_id(1)
    is_hi = lax.broadcasted_iota(jnp.int32, (2 * TM, 1), 0) >= TM

    def finish(j, ps, vs, l):
        o = functools.reduce(lambda a, c: a + c, [_dot(p.astype(_BF16), v) for p, v in zip(ps, vs)]) / l
        o_ref[0, :, j * 128:(j + 1) * 128] = _unstack_heads(o).astype(_BF16)

    def latent():
        start = i * TM
        ws = pl.multiple_of(jnp.clip(start - WINDOW, 0, s_len - span), WINDOW)
        kw = k_ref[0, pl.ds(ws, span), :]
        vw = v_ref[0, pl.ds(ws, span), :]
        kc = k_ref[0, s_len:s_len + l_ctx, :]
        vc = v_ref[0, s_len:s_len + l_ctx, :]
        row = lax.broadcasted_iota(jnp.int32, (2 * TM, span), 0)
        qpos = start + jnp.where(row >= TM, row - TM, row)
        kpos = ws + lax.broadcasted_iota(jnp.int32, (2 * TM, span), 1)
        band = jnp.abs(kpos - qpos) <= WINDOW
        for j in range(4):
            lhs = _stack_heads(q_ref[0, :, j * 128:(j + 1) * 128])
            s_loc = jnp.where(band, _nt(lhs, kw), NEG_INF)
            s_ctx = _nt(lhs, kc)
            sink = jnp.where(is_hi, sink_ref[4 + j], sink_ref[j])
            ps, l = _softmax_parts([s_loc, s_ctx], sink)
            finish(j, ps, (vw, vc), l)

    def context():
        kc = k_ref[0, s_len:s_len + l_ctx, :]
        vc = v_ref[0, s_len:s_len + l_ctx, :]
        for j in range(4):
            lhs = _stack_heads(q_ref[0, :, j * 128:(j + 1) * 128])
            sink = jnp.where(is_hi, sink_ref[4 + j], sink_ref[j])
            ps, l = _softmax_parts([_nt(lhs, kc)], sink)
            finish(j, ps, (vc,), l)

    if with_ctx:
        is_ctx = i == s_len // TM
        pl.when(jnp.logical_not(is_ctx))(latent)
        pl.when(is_ctx)(context)
    else:
        latent()


def _attn_b_call(q, k, v, sink, s_len, with_ctx):
    b, t, _ = q.shape
    nq = s_len // TM + (1 if with_ctx else 0)
    return pl.pallas_call(
        functools.partial(_attn_b_body, s_len=s_len, l_ctx=t - s_len, with_ctx=with_ctx),
        grid=(b, nq),
        in_specs=[
            pl.BlockSpec(memory_space=pltpu.SMEM),
            pl.BlockSpec((1, TM, BRANCH), lambda bi, i: (bi, i, 0)),
            pl.BlockSpec((1, t, 128), lambda bi, i: (bi, 0, 0)),
            pl.BlockSpec((1, t, 128), lambda bi, i: (bi, 0, 0)),
        ],
        out_specs=pl.BlockSpec((1, TM, BRANCH), lambda bi, i: (bi, i, 0)),
        out_shape=jax.ShapeDtypeStruct((b, nq * TM, BRANCH), _BF16),
        compiler_params=_cparams(("arbitrary", "arbitrary"), VMEM_LIMIT),
        name="attn_window",
    )(sink, q, k, v)


def _attn_n_body(q_ref, k_ref, v_ref, bias_ref, o_ref, *, s_len, l_ctx, with_ctx):
    rows = s_len // GRID_W
    span = NA_WIN_ROWS * GRID_W
    r = pl.program_id(1)

    def finish(j, ps, vs, l):
        o = functools.reduce(lambda a, c: a + c, [_dot(p.astype(_BF16), v) for p, v in zip(ps, vs)]) / l
        o_ref[0, :, j * 128:(j + 1) * 128] = _unstack_heads(o).astype(_BF16)

    def latent():
        u0 = jnp.clip(4 * r - 4, 0, rows - NA_WIN_ROWS)
        ws = pl.multiple_of(u0 * GRID_W, GRID_W)
        for j in range(4):
            sl = slice(j * 128, (j + 1) * 128)
            kw = k_ref[0, pl.ds(ws, span), sl]
            vw = v_ref[0, pl.ds(ws, span), sl]
            kc = k_ref[0, s_len:s_len + l_ctx, sl]
            vc = v_ref[0, s_len:s_len + l_ctx, sl]
            lhs = _stack_heads(q_ref[0, :, sl])
            s_loc = _nt(lhs, kw) + bias_ref[0, j]
            s_ctx = _nt(lhs, kc)
            ps, l = _softmax_parts([s_loc, s_ctx])
            finish(j, ps, (vw, vc), l)

    def context():
        for j in range(4):
            sl = slice(j * 128, (j + 1) * 128)
            lhs = _stack_heads(q_ref[0, :, sl])
            ps, l = _softmax_parts([_nt(lhs, k_ref[0, s_len:s_len + l_ctx, sl])])
            finish(j, ps, (v_ref[0, s_len:s_len + l_ctx, sl],), l)

    if with_ctx:
        is_ctx = r == s_len // TM
        pl.when(jnp.logical_not(is_ctx))(latent)
        pl.when(is_ctx)(context)
    else:
        latent()


def _attn_n_call(q, k, v, bias_tbl, s_len, with_ctx):
    b, t, _ = q.shape
    n_r = s_len // TM
    nq = n_r + (1 if with_ctx else 0)
    span = NA_WIN_ROWS * GRID_W

    def case(ri):
        return jnp.where(ri == 0, 0, jnp.where(ri >= n_r - 1, 2, 1))

    return pl.pallas_call(
        functools.partial(_attn_n_body, s_len=s_len, l_ctx=t - s_len, with_ctx=with_ctx),
        grid=(b, nq),
        in_specs=[
            pl.BlockSpec((1, TM, BRANCH), lambda bi, ri: (bi, ri, 0)),
            pl.BlockSpec((1, t, BRANCH), lambda bi, ri: (bi, 0, 0)),
            pl.BlockSpec((1, t, BRANCH), lambda bi, ri: (bi, 0, 0)),
            pl.BlockSpec((1, 4, 2 * TM, span), lambda bi, ri: (case(ri), 0, 0, 0)),
        ],
        out_specs=pl.BlockSpec((1, TM, BRANCH), lambda bi, ri: (bi, ri, 0)),
        out_shape=jax.ShapeDtypeStruct((b, nq * TM, BRANCH), _BF16),
        compiler_params=_cparams(("arbitrary", "arbitrary"), VMEM_LIMIT),
        name="attn_neighbourhood",
    )(q, k, v, bias_tbl)


def _na_tables(rows):
    a = np.arange(4)[:, None, None, None]
    c = np.arange(GRID_W)[None, :, None, None]
    i = np.arange(NA_WIN_ROWS)[None, None, :, None]
    kc = np.arange(GRID_W)[None, None, None, :]
    c0 = np.clip(c - NA_COLS // 2, 0, GRID_W - NA_COLS)
    col_ok = (kc >= c0) & (kc < c0 + NA_COLS)
    cidx = kc - c + (NA_COLS - 1)
    row_ok = [i < NA_ROWS + 0 * a, (i >= a) & (i < a + NA_ROWS), i >= NA_WIN_ROWS - NA_ROWS + 0 * a]
    ridx = [i - a + 7, i - a + 3, i - a - 1]
    shape = (4, GRID_W, NA_WIN_ROWS, GRID_W)
    sel_r, sel_c, ok = [], [], []
    for cs in range(3):
        valid = np.broadcast_to(row_ok[cs] & col_ok, shape)
        ok.append(valid.reshape(4 * GRID_W, NA_WIN_ROWS * GRID_W))
        rsel = (np.broadcast_to(ridx[cs], (4, 1, NA_WIN_ROWS, 1))[:, 0, :, 0][..., None]
                == np.arange(2 * NA_ROWS - 1)[None, None, :])
        sel_r.append(rsel.astype(np.float32))
    csel = (np.broadcast_to(cidx, (1, GRID_W, 1, GRID_W))[0, :, 0, :][..., None]
            == np.arange(2 * NA_COLS - 1)[None, None, :]).astype(np.float32)
    return np.stack(sel_r), csel, np.stack(ok)


def _na_bias_table(rpb, rows):
    sel_r, sel_c, ok = _na_tables(rows)
    hi = lax.Precision.HIGHEST
    t = jnp.einsum("sair,hrg->shaig", jnp.asarray(sel_r), rpb.astype(_F32), precision=hi)
    t = jnp.einsum("shaig,ckg->shacik", t, jnp.asarray(sel_c), precision=hi)
    t = t.reshape(3, N_HEADS, 4 * GRID_W, NA_WIN_ROWS * GRID_W)
    t = jnp.where(jnp.asarray(ok)[:, None], t, NEG_INF)
    return t.reshape(3, 4, 2 * TM, NA_WIN_ROWS * GRID_W)


def _route_rows(sel, sc):
    gs, gi1, gi2, gs1, gs2 = [], [], [], [], []
    for g in range(N_GROUPS):
        v = sel[4 * g:4 * g + 4]
        s = sc[4 * g:4 * g + 4]
        m1 = jnp.maximum(jnp.maximum(v[0], v[1]), jnp.maximum(v[2], v[3]))
        i1 = jnp.where(v[0] == m1, 0, jnp.where(v[1] == m1, 1, jnp.where(v[2] == m1, 2, 3)))
        rest = [jnp.where(i1 == k, NEG_INF, v[k]) for k in range(4)]
        m2 = jnp.maximum(jnp.maximum(rest[0], rest[1]), jnp.maximum(rest[2], rest[3]))
        i2 = jnp.where(rest[0] == m2, 0, jnp.where(rest[1] == m2, 1, jnp.where(rest[2] == m2, 2, 3)))

        def pick(idx, s=s):
            return jnp.where(idx == 0, s[0], jnp.where(idx == 1, s[1], jnp.where(idx == 2, s[2], s[3])))

        gs.append(m1 + m2)
        gi1.append(i1 + 4 * g)
        gi2.append(i2 + 4 * g)
        gs1.append(pick(i1))
        gs2.append(pick(i2))
    best = jnp.maximum(jnp.maximum(gs[0], gs[1]), jnp.maximum(gs[2], gs[3]))
    gsel = jnp.where(gs[0] == best, 0, jnp.where(gs[1] == best, 1, jnp.where(gs[2] == best, 2, 3)))

    def pickg(vals):
        return jnp.where(gsel == 0, vals[0], jnp.where(gsel == 1, vals[1], jnp.where(gsel == 2, vals[2], vals[3])))

    return pickg(gi1), pickg(gi2), pickg(gs1), pickg(gs2)


def _merge_body(oa_ref, ob_ref, on_ref, gate_ref, x_ref, xctx_ref, wb_ref, wo_ref, g1_ref, sh2_ref, sc2_ref, n2_ref,
                wrh_ref, wrl_ref, rb_ref,
                x1_ref, h2_ref, idx_ref, rank_ref, wts_ref, cnt_ref, carry_ref, *, n_lat_tiles):
    first = (pl.program_id(0) == 0) & (pl.program_id(1) == 0)
    x_res = jnp.where(pl.program_id(1) < n_lat_tiles, x_ref[0], xctx_ref[0])

    @pl.when(first)
    def _():
        carry_ref[...] = jnp.zeros_like(carry_ref)

    outs = (oa_ref, ob_ref, on_ref)
    y = None
    for n in range(3):
        gt = jax.nn.sigmoid(gate_ref[0, :, n * D_MODEL:(n + 1) * D_MODEL].astype(_F32))
        term = gt * _dot(outs[n][0], wb_ref[n])
        y = term if y is None else y + term
    z = _dot(y.astype(_BF16), wo_ref[...])
    x1 = x_res + g1_ref[0] * z
    x1_ref[0] = x1
    ms = jnp.mean(x1 * x1, axis=-1, keepdims=True)
    h2 = (x1 * lax.rsqrt(ms + EPS) * n2_ref[...]) * (1.0 + sc2_ref[0]) + sh2_ref[0]
    h2_ref[0] = h2

    hb = h2.astype(_BF16)
    hl = (h2 - hb.astype(_F32)).astype(_BF16)
    logits = _nt(wrh_ref[...], hb) + _nt(wrh_ref[...], hl) + _nt(wrl_ref[...], hb)
    scores = jax.nn.sigmoid(logits)
    sel = scores + rb_ref[...]
    sel_rows = [sel[e:e + 1, :] for e in range(N_EXPERTS)]
    sc_rows = [scores[e:e + 1, :] for e in range(N_EXPERTS)]
    e1, e2, s1, s2 = _route_rows(sel_rows, sc_rows)
    den = s1 + s2
    idx_ref[0:1, :] = e1
    idx_ref[1:2, :] = e2
    wts_ref[0:1, :] = s1 / den
    wts_ref[1:2, :] = s2 / den

    eid = lax.broadcasted_iota(jnp.int32, (N_EXPERTS, TM), 0)
    oh1 = eid == e1
    oh2 = eid == e2
    tri = (lax.broadcasted_iota(jnp.int32, (TM, TM), 0) <= lax.broadcasted_iota(jnp.int32, (TM, TM), 1))
    tri = tri.astype(_BF16)
    cum1 = _dot(oh1.astype(_BF16), tri)
    cum2 = _dot(oh2.astype(_BF16), tri)
    tot1 = cum1[:, TM - 1:TM]
    tot2 = cum2[:, TM - 1:TM]
    carry = carry_ref[:, 0:1]
    r1 = jnp.sum(jnp.where(oh1, carry + cum1 - 1.0, 0.0), axis=0, keepdims=True)
    r2 = jnp.sum(jnp.where(oh2, carry + tot1 + cum2 - 1.0, 0.0), axis=0, keepdims=True)
    rank_ref[0:1, :] = r1.astype(jnp.int32)
    rank_ref[1:2, :] = r2.astype(jnp.int32)
    new_carry = carry + tot1 + tot2
    carry_ref[...] = jnp.broadcast_to(new_carry, carry_ref.shape)
    cnt_ref[...] = jnp.broadcast_to(new_carry, cnt_ref.shape).astype(jnp.int32)


def _merge_call(oa, ob, on, gate, x_lat, x_ctx, ctx_tile, wb, wo, g1, sh2, sc2, n2, wrh, wrl, rb, t_out,
                n_lat_tiles):
    b, _, d = x_lat.shape
    nb = g1.shape[0] - 1
    nt = t_out // TM
    n_tok = b * t_out

    def mod_idx(bi, i):
        return (jnp.where(i < n_lat_tiles, bi, nb), 0, 0)

    def tok(width):
        return pl.BlockSpec((1, TM, width), lambda bi, i: (bi, i, 0))

    def full(shape):
        return pl.BlockSpec(shape, lambda bi, i: (0,) * len(shape))

    lane_out = pl.BlockSpec((2, TM), lambda bi, i: (0, bi * nt + i))
    return pl.pallas_call(
        functools.partial(_merge_body, n_lat_tiles=n_lat_tiles),
        grid=(b, nt),
        in_specs=[tok(BRANCH), tok(BRANCH), tok(BRANCH), tok(3 * d), *_token_specs(n_lat_tiles, ctx_tile, d),
                  full((3, BRANCH, d)), full((d, d)),
                  pl.BlockSpec((1, 1, d), mod_idx), pl.BlockSpec((1, 1, d), mod_idx),
                  pl.BlockSpec((1, 1, d), mod_idx), full((1, d)),
                  full((N_EXPERTS, d)), full((N_EXPERTS, d)), full((N_EXPERTS, 1))],
        out_specs=[tok(d), tok(d), lane_out, lane_out, lane_out, full((N_EXPERTS, 128))],
        out_shape=[jax.ShapeDtypeStruct((b, t_out, d), _F32), jax.ShapeDtypeStruct((b, t_out, d), _F32),
                   jax.ShapeDtypeStruct((2, n_tok), jnp.int32), jax.ShapeDtypeStruct((2, n_tok), jnp.int32),
                   jax.ShapeDtypeStruct((2, n_tok), _F32), jax.ShapeDtypeStruct((N_EXPERTS, 128), jnp.int32)],
        scratch_shapes=[pltpu.VMEM((N_EXPERTS, 128), _F32)],
        compiler_params=_cparams(("arbitrary", "arbitrary"), VMEM_LIMIT),
        name="merge_route",
    )(oa, ob, on, gate, x_lat, x_ctx, wb, wo, g1, sh2, sc2, n2, wrh, wrl, rb)


ROW_DMA_UNROLL = 8


def _dispatch_body(pad_start_ref, pad_n_ref, nv_ref, pos_ref, h_ref, xs_ref, zbuf, sem, zsem):
    @pl.when(pl.program_id(0) == 0)
    def _():
        zbuf[...] = jnp.zeros_like(zbuf)

        def zero_row(e, r):
            return pltpu.make_async_copy(zbuf.at[pl.ds(0, 1)], xs_ref.at[pl.ds(pad_start_ref[e] + r, 1)], zsem)

        def zero_tile(i):
            return pltpu.make_async_copy(zbuf, xs_ref.at[pl.ds(pl.multiple_of(i * TM_E, TM_E), TM_E)], zsem)

        n_tiles = xs_ref.shape[0] // TM_E
        for e in range(N_EXPERTS):
            lax.fori_loop(0, pad_n_ref[e], lambda r, c, e=e: (zero_row(e, r).start(), c)[1], 0)
        lax.fori_loop(nv_ref[0], n_tiles, lambda i, c: (zero_tile(i).start(), c)[1], 0)
        for e in range(N_EXPERTS):
            lax.fori_loop(0, pad_n_ref[e], lambda r, c, e=e: (zero_row(e, r).wait(), c)[1], 0)
        lax.fori_loop(nv_ref[0], n_tiles, lambda i, c: (zero_tile(i).wait(), c)[1], 0)

    def row_copy(t, k):
        return pltpu.make_async_copy(h_ref.at[pl.ds(t, 1)], xs_ref.at[pl.ds(pos_ref[0, k, t], 1)], sem)

    def issue(t, carry):
        row_copy(t, 0).start()
        row_copy(t, 1).start()
        return carry

    lax.fori_loop(0, TM, issue, 0, unroll=ROW_DMA_UNROLL)

    def drain(t, carry):
        row_copy(t, 0).wait()
        row_copy(t, 1).wait()
        return carry

    lax.fori_loop(0, TM, drain, 0, unroll=ROW_DMA_UNROLL)


def _dispatch_call(pad_start, pad_n, n_valid, pos3, h2, p_rows):
    n_tiles = pos3.shape[0]
    d = h2.shape[1]
    return pl.pallas_call(
        _dispatch_body,
        grid_spec=pltpu.PrefetchScalarGridSpec(
            num_scalar_prefetch=3,
            grid=(n_tiles,),
            in_specs=[pl.BlockSpec((1, 2, TM), lambda i, ps, pn, nv: (i, 0, 0), memory_space=pltpu.SMEM),
                      pl.BlockSpec((TM, d), lambda i, ps, pn, nv: (i, 0))],
            out_specs=pl.BlockSpec(memory_space=pl.ANY),
            scratch_shapes=[pltpu.VMEM((TM_E, d), h2.dtype), pltpu.SemaphoreType.DMA, pltpu.SemaphoreType.DMA],
        ),
        out_shape=jax.ShapeDtypeStruct((p_rows, d), h2.dtype),
        compiler_params=_cparams(("arbitrary",)),
        name="moe_dispatch",
    )(pad_start, pad_n, n_valid, pos3, h2)


def _moe_body(te_ref, nv_ref, xs_ref, wg_ref, wu_ref, wd_ref, ys_ref):
    del te_ref
    live = pl.program_id(0) < nv_ref[0]

    @pl.when(live)
    def _():
        xb = xs_ref[...].astype(_BF16)
        gate = _dot(xb, wg_ref[0])
        up = _dot(xb, wu_ref[0])
        act = (gate * jax.nn.sigmoid(gate)) * up
        ys_ref[...] = _dot(act.astype(_BF16), wd_ref[0])

    @pl.when(jnp.logical_not(live))
    def _():
        ys_ref[...] = jnp.zeros_like(ys_ref)


def _moe_call(tile_expert, n_valid, xs, wg, wu, wd):
    p_rows, d = xs.shape
    n_tiles = p_rows // TM_E

    def row_idx(i, te, nv):
        return (jnp.minimum(i, nv[0] - 1), 0)

    def out_idx(i, te, nv):
        return (i, 0)

    return pl.pallas_call(
        _moe_body,
        grid_spec=pltpu.PrefetchScalarGridSpec(
            num_scalar_prefetch=2,
            grid=(n_tiles,),
            in_specs=[pl.BlockSpec((TM_E, d), row_idx),
                      pl.BlockSpec((1, d, D_FF), lambda i, te, nv: (te[i], 0, 0)),
                      pl.BlockSpec((1, d, D_FF), lambda i, te, nv: (te[i], 0, 0)),
                      pl.BlockSpec((1, D_FF, d), lambda i, te, nv: (te[i], 0, 0))],
            out_specs=pl.BlockSpec((TM_E, d), out_idx),
        ),
        out_shape=jax.ShapeDtypeStruct((p_rows, d), _F32),
        compiler_params=_cparams(("arbitrary",), VMEM_LIMIT),
        name="moe_experts",
    )(tile_expert, n_valid, xs, wg, wu, wd)


def _col_from_row(row):
    blk = jnp.broadcast_to(row, (128, 128))
    return blk.T[:, 0:1]


def _combine_body(pos_ref, w_ref, x1_ref, g2_ref, fn_ref, ys_ref, o_ref, buf, sem, *, final):
    def row_copy(t, k):
        return pltpu.make_async_copy(ys_ref.at[pl.ds(pos_ref[0, k, t], 1)], buf.at[k, pl.ds(t, 1)], sem)

    def issue(t, carry):
        row_copy(t, 0).start()
        row_copy(t, 1).start()
        return carry

    lax.fori_loop(0, TM, issue, 0, unroll=ROW_DMA_UNROLL)

    def drain(t, carry):
        row_copy(t, 0).wait()
        row_copy(t, 1).wait()
        return carry

    lax.fori_loop(0, TM, drain, 0, unroll=ROW_DMA_UNROLL)

    w = w_ref[...]
    cols = []
    for k in range(2):
        cols.append(jnp.concatenate(
            [_col_from_row(w[k:k + 1, h * 128:(h + 1) * 128]) for h in range(TM // 128)], axis=0))
    y = cols[0] * buf[0] + cols[1] * buf[1]
    x2 = x1_ref[...] + g2_ref[0] * y
    if final:
        ms = jnp.mean(x2 * x2, axis=-1, keepdims=True)
        x2 = x2 * lax.rsqrt(ms + EPS) * fn_ref[...]
    o_ref[...] = x2


def _combine_call(pos3, wts, x1, g2, fnorm, ys, t_out, n_lat_tiles, final):
    n_tok, d = x1.shape
    nb = g2.shape[0] - 1
    nt = t_out // TM

    def mod_idx(i):
        return (jnp.where(i % nt < n_lat_tiles, i // nt, nb), 0, 0)

    return pl.pallas_call(
        functools.partial(_combine_body, final=final),
        grid=(n_tok // TM,),
        in_specs=[pl.BlockSpec((1, 2, TM), lambda i: (i, 0, 0), memory_space=pltpu.SMEM),
                  pl.BlockSpec((2, TM), lambda i: (0, i)),
                  pl.BlockSpec((TM, d), lambda i: (i, 0)),
                  pl.BlockSpec((1, 1, d), mod_idx),
                  pl.BlockSpec((1, d), lambda i: (0, 0)),
                  pl.BlockSpec(memory_space=pl.ANY)],
        out_specs=pl.BlockSpec((TM, d), lambda i: (i, 0)),
        out_shape=jax.ShapeDtypeStruct((n_tok, d), _F32),
        scratch_shapes=[pltpu.VMEM((2, TM, d), _F32), pltpu.SemaphoreType.DMA],
        compiler_params=_cparams(("arbitrary",)),
        name="moe_combine",
    )(pos3, wts, x1, g2, fnorm, ys)


def _rope_tables(s_len, l_ctx):
    t = jnp.arange(s_len, dtype=jnp.int32)
    row = (t // GRID_W).astype(_F32)
    col = (t % GRID_W).astype(_F32)
    half = HEAD_DIM // 2
    inv = ROPE_THETA ** (-jnp.arange(0, half, 2, dtype=_F32) / half)
    ar = row[:, None] * inv
    ac = col[:, None] * inv
    cos64 = jnp.concatenate([jnp.cos(ar), jnp.cos(ar), jnp.cos(ac), jnp.cos(ac)], axis=1)
    sin64 = jnp.concatenate([-jnp.sin(ar), jnp.sin(ar), -jnp.sin(ac), jnp.sin(ac)], axis=1)
    cos_t = jnp.concatenate([jnp.tile(cos64, (1, 2)), jnp.ones((l_ctx, 128), _F32)], axis=0)
    sin_t = jnp.concatenate([jnp.tile(sin64, (1, 2)), jnp.zeros((l_ctx, 128), _F32)], axis=0)
    return cos_t, sin_t


def _block_diag_ones():
    r = np.arange(BRANCH)
    return jnp.asarray((r[:, None] // HEAD_DIM) == (r[None, :] // HEAD_DIM), dtype=_BF16)


def kernel(x, c, ctx, c_ctx, w_mod, b_mod, norm1, norm2, w_in, q_norm_a, k_norm_a, sink_b, rpb_n, w_branch, w_out,
           w_router, router_bias, w_e_gate, w_e_up, w_e_down, final_norm):
    b, s_len, d = x.shape
    l_ctx = ctx.shape[1]
    depth = w_in.shape[0]
    t_all = s_len + l_ctx
    rows = s_len // GRID_W
    assert d == D_MODEL and l_ctx == TM and s_len % TM == 0 and b + 1 <= 16
    assert s_len >= TM + 2 * WINDOW and rows >= NA_WIN_ROWS and rows % 4 == 0
    n_lat_tiles = s_len // TM

    cos_t, sin_t = _rope_tables(s_len, l_ctx)
    bd = _block_diag_ones()
    cvec = jnp.zeros((16, d), _F32).at[:b].set(c).at[b].set(c_ctx)
    mods = _mod_call(cvec, w_mod, b_mod)[:, :b + 1].reshape(depth, b + 1, 6, 1, d)

    wr = w_router.T.astype(_F32)
    wrh = wr.astype(_BF16)
    wrl = (wr - wrh.astype(_F32)).astype(_BF16)
    rb = router_bias.astype(_F32).reshape(N_EXPERTS, 1)

    x_lat, x_ctx, ctx_tile = x, ctx, 0
    for l in range(depth):
        last = l == depth - 1
        sh1, sc1, g1, sh2, sc2, g2 = [mods[l, :, n] for n in range(6)]
        w_f = w_in[l]
        w_l = jnp.concatenate([_pair_heads(w_f[:, OFF_QA:OFF_KA], 1), w_f[:, OFF_KA:OFF_QB],
                               _pair_heads(w_f[:, OFF_QB:OFF_KB], 1), w_f[:, OFF_KB:]], axis=1).astype(_BF16)
        gq = (jnp.tile(q_norm_a[l].astype(_F32), N_HEADS) * (ATTN_SCALE * LOG2_E)).reshape(1, BRANCH)
        gk = jnp.tile(k_norm_a[l].astype(_F32), 2).reshape(1, KV_A)
        qa, ka, va, qb, kb, vb, qn, kn, vn, gate = _inproj_call(
            x_lat, x_ctx, ctx_tile, n_lat_tiles, norm1[l].reshape(1, d), sh1, sc1, w_l, cos_t, sin_t, gq, gk, bd)

        sink = sink_b[l].astype(_F32)
        oa = _attn_a_call(qa, ka, va, s_len, not last)
        ob = _attn_b_call(qb, kb, vb, sink, s_len, not last)
        on = _attn_n_call(qn, kn, vn, _na_bias_table(rpb_n[l], rows), s_len, not last)

        t_out = s_len if last else t_all
        wb = jnp.stack([_pair_heads(w_branch[l, 0], 0), _pair_heads(w_branch[l, 1], 0),
                        w_branch[l, 2]]).astype(_BF16)
        x1, h2, idx, rank, wts, cnt = _merge_call(
            oa, ob, on, gate, x_lat, x_ctx, ctx_tile, wb, w_out[l].astype(_BF16), g1, sh2, sc2,
            norm2[l].reshape(1, d), wrh, wrl, rb, t_out, n_lat_tiles)

        n_tok = b * t_out
        counts = cnt[:, 0]
        padded = ((counts + TM_E - 1) // TM_E) * TM_E
        ends = jnp.cumsum(padded)
        offs = ends - padded
        pos = rank + jnp.sum(jnp.where(idx[None] == jnp.arange(N_EXPERTS)[:, None, None], offs[:, None, None], 0), axis=0)
        pos3 = pos.reshape(2, n_tok // TM, TM).transpose(1, 0, 2)
        p_rows = 2 * n_tok + N_EXPERTS * TM_E
        n_tiles = p_rows // TM_E
        n_valid = (ends[-1] // TM_E).astype(jnp.int32).reshape(1)
        tile_start = jnp.arange(n_tiles, dtype=jnp.int32) * TM_E
        tile_expert = jnp.minimum(jnp.sum(tile_start[:, None] >= ends[None, :], axis=1), N_EXPERTS - 1).astype(jnp.int32)

        pad_start = (offs + counts).astype(jnp.int32)
        pad_n = (padded - counts).astype(jnp.int32)
        xs = _dispatch_call(pad_start, pad_n, n_valid, pos3, h2.reshape(n_tok, d), p_rows)
        ys = _moe_call(tile_expert, n_valid, xs, w_e_gate[l].astype(_BF16), w_e_up[l].astype(_BF16),
                       w_e_down[l].astype(_BF16))
        x2 = _combine_call(pos3, wts, x1.reshape(n_tok, d), g2, final_norm.reshape(1, d), ys, t_out, n_lat_tiles, last)
        x_lat = x_ctx = x2.reshape(b, t_out, d)
        ctx_tile = n_lat_tiles
    return x_lat
```

```python
import functools

import numpy as np
import jax
import jax.numpy as jnp
from jax import lax
from jax.experimental import pallas as pl
from jax.experimental.pallas import tpu as pltpu

D_MODEL = 1024
HEAD_DIM = 64
GRID_W = 64
N_HEADS = 8
WINDOW = 128
NA_ROWS = 8
NA_COLS = 16
ROPE_THETA = 10000.0
N_EXPERTS = 16
N_GROUPS = 4
EXPERTS_PER_GROUP = N_EXPERTS // N_GROUPS
D_FF = 512
EPS = 1e-6
NEG_INF = -1e30
ATTN_SCALE = HEAD_DIM ** -0.5
LOG2_E = 1.4426950408889634
BRANCH = N_HEADS * HEAD_DIM
KV_A = 2 * HEAD_DIM
IN_COLS = 6144

OFF_QA, OFF_KA, OFF_VA = 0, 512, 640
OFF_QB, OFF_KB, OFF_VB = 768, 1280, 1408
OFF_QN, OFF_KN, OFF_VN = 1536, 2048, 2560
OFF_GATE = 3072

TM = 256
TM_E = 512
NA_WIN_ROWS = 12
VMEM_LIMIT = 56 * 1024 * 1024

_F32 = jnp.float32
_BF16 = jnp.bfloat16

def _pair_heads(w, axis):
    shape = w.shape
    split = shape[:axis] + (2, 4, HEAD_DIM) + shape[axis + 1:]
    return jnp.swapaxes(w.reshape(split), axis, axis + 1).reshape(shape)


def _nt(a, b):
    return lax.dot_general(a, b, (((1,), (1,)), ((), ())), preferred_element_type=_F32)


def _dot(a, b):
    return jnp.dot(a, b, preferred_element_type=_F32)


def _cparams(sem, vmem=None):
    return pltpu.CompilerParams(dimension_semantics=sem, vmem_limit_bytes=vmem)


def _mod_body(c_ref, w_ref, b_ref, o_ref):
    cv = c_ref[...]
    a = (cv * jax.nn.sigmoid(cv)).astype(_BF16)
    o_ref[0] = _dot(a, w_ref[0].astype(_BF16)) + b_ref[0]


def _mod_call(cvec, w_mod, b_mod):
    depth, d, n = w_mod.shape
    tn = 1536
    return pl.pallas_call(
        _mod_body,
        grid=(depth, n // tn),
        in_specs=[
            pl.BlockSpec((16, d), lambda l, j: (0, 0)),
            pl.BlockSpec((1, d, tn), lambda l, j: (l, 0, j)),
            pl.BlockSpec((1, 1, tn), lambda l, j: (l, 0, j)),
        ],
        out_specs=pl.BlockSpec((1, 16, tn), lambda l, j: (l, 0, j)),
        out_shape=jax.ShapeDtypeStruct((depth, 16, n), _F32),
        compiler_params=_cparams(("arbitrary", "arbitrary")),
        name="mod_vectors",
    )(cvec, w_mod, b_mod.reshape(depth, 1, n))


def _rope(v, c, s):
    w = v.shape[1]
    lane = lax.broadcasted_iota(jnp.int32, v.shape, 1)
    first = (lane & 16) == 0
    sw = jnp.where(first, pltpu.roll(v, w - 16, 1), pltpu.roll(v, 16, 1))
    return v * c + sw * s


def _inproj_body(x_ref, xctx_ref, *refs, n_lat_tiles):
    x = jnp.where(pl.program_id(1) < n_lat_tiles, x_ref[0], xctx_ref[0])
    _inproj_core(x, *refs)


def _col_from_row(row):
    blk = jnp.broadcast_to(row, (128, 128))
    return blk.T[:, 0:1]


def _moe_rows_pipeline(step, last_step, pos0_ref, posn_ref, w_ref, ys_ref, gbuf, sems):
    def row_copy(pos_ref, slot, t, k):
        return pltpu.make_async_copy(ys_ref.at[pl.ds(pos_ref[0, k, t], 1)], gbuf.at[slot, k, pl.ds(t, 1)],
                                     sems.at[slot])

    def gather(pos_ref, slot):
        for t in range(TM):
            row_copy(pos_ref, slot, t, 0).start()
            row_copy(pos_ref, slot, t, 1).start()

    def drain(slot):
        for t in range(TM):
            row_copy(pos0_ref, slot, t, 0).wait()
            row_copy(pos0_ref, slot, t, 1).wait()

    pl.when(step == 0)(lambda: gather(pos0_ref, 0))
    slot = step % 2
    drain(slot)
    w = w_ref[...]
    cols = [jnp.concatenate([_col_from_row(w[k:k + 1, h * 128:(h + 1) * 128]) for h in range(TM // 128)], axis=0)
            for k in range(2)]
    y = cols[0] * gbuf[slot, 0] + cols[1] * gbuf[slot, 1]
    return (y, lambda: gather(posn_ref, (step + 1) % 2),
            lambda: pl.when(step == last_step)(lambda: drain((step + 1) % 2)))


def _inproj_combine_body(pos0_ref, posn_ref, w_ref, x_ref, xctx_ref, g2_ref, ys_ref, *refs, n_lat_tiles):
    core_refs, x2_ref, gbuf, sems = refs[:-3], refs[-3], refs[-2], refs[-1]
    nt = pl.num_programs(1)
    step = pl.program_id(0) * nt + pl.program_id(1)
    y, issue_next, finish = _moe_rows_pipeline(step, pl.num_programs(0) * nt - 1, pos0_ref, posn_ref, w_ref,
                                               ys_ref, gbuf, sems)
    x1 = jnp.where(pl.program_id(1) < n_lat_tiles, x_ref[0], xctx_ref[0])
    x = x1 + g2_ref[0] * y
    x2_ref[0] = x
    issue_next()
    _inproj_core(x, *core_refs)
    finish()


def _inproj_core(x, g_ref, sh_ref, sc_ref, w_ref, cos_ref, sin_ref, gq_ref, gk_ref, bd_ref,
                 qa_ref, ka_ref, va_ref, qb_ref, kb_ref, vb_ref, qn_ref, kn_ref, vn_ref, gate_ref):
    ms = jnp.mean(x * x, axis=-1, keepdims=True)
    h = (x * lax.rsqrt(ms + EPS) * g_ref[...]) * (1.0 + sc_ref[0]) + sh_ref[0]
    hb = h.astype(_BF16)

    def seg(off, width):
        return _dot(hb, w_ref[:, off:off + width])

    c1 = cos_ref[...]
    s1 = sin_ref[...]
    c4 = jnp.concatenate([c1] * 4, axis=1)
    s4 = jnp.concatenate([s1] * 4, axis=1)
    bd = bd_ref[...]

    qa = seg(OFF_QA, BRANCH)
    ssq = _dot((qa * qa).astype(_BF16), bd)
    qa = qa * lax.rsqrt(ssq * (1.0 / HEAD_DIM) + EPS) * gq_ref[...]
    qa_ref[0] = _rope(qa, c4, s4).astype(_BF16)
    ka = seg(OFF_KA, KV_A)
    ssk = _dot((ka * ka).astype(_BF16), bd_ref[:KV_A, :KV_A])
    ka = ka * lax.rsqrt(ssk * (1.0 / HEAD_DIM) + EPS) * gk_ref[...]
    ka_ref[0] = _rope(ka, c1, s1).astype(_BF16)
    va_ref[0] = seg(OFF_VA, KV_A).astype(_BF16)
    qb_ref[0] = _rope(seg(OFF_QB, BRANCH) * ATTN_SCALE, c4, s4).astype(_BF16)
    kb_ref[0] = _rope(seg(OFF_KB, KV_A), c1, s1).astype(_BF16)
    vb_ref[0] = seg(OFF_VB, KV_A).astype(_BF16)
    qn_ref[0] = (seg(OFF_QN, BRANCH) * ATTN_SCALE).astype(_BF16)
    kn_ref[0] = seg(OFF_KN, BRANCH).astype(_BF16)
    vn_ref[0] = seg(OFF_VN, BRANCH).astype(_BF16)
    for n in range(3):
        gate_ref[0, :, n * D_MODEL:(n + 1) * D_MODEL] = seg(OFF_GATE + n * D_MODEL, D_MODEL).astype(_BF16)


def _token_specs(n_lat_tiles, ctx_tile, d):
    lat = pl.BlockSpec((1, TM, d), lambda bi, i: (bi, jnp.minimum(i, n_lat_tiles - 1), 0))
    ctx = pl.BlockSpec((1, TM, d), lambda bi, i: (bi, ctx_tile, 0))
    return lat, ctx


def _inproj_call(x_lat, x_ctx, ctx_tile, n_lat_tiles, g1n, sh, sc, w, cos_t, sin_t, gq, gk, bd, combine=None):
    b, _, d = x_lat.shape
    nb = sh.shape[0] - 1
    nt = n_lat_tiles + 1
    t = nt * TM

    def mod_idx(bi, i):
        return (jnp.where(i < n_lat_tiles, bi, nb), 0, 0)

    def tok(width):
        return pl.BlockSpec((1, TM, width), lambda bi, i: (bi, i, 0))

    widths = [BRANCH, KV_A, KV_A, BRANCH, KV_A, KV_A, BRANCH, BRANCH, BRANCH, 3 * D_MODEL]
    core_specs = [
        pl.BlockSpec((1, d), lambda bi, i: (0, 0)),
        pl.BlockSpec((1, 1, d), mod_idx),
        pl.BlockSpec((1, 1, d), mod_idx),
        pl.BlockSpec((d, IN_COLS), lambda bi, i: (0, 0)),
        pl.BlockSpec((TM, 128), lambda bi, i: (i, 0)),
        pl.BlockSpec((TM, 128), lambda bi, i: (i, 0)),
        pl.BlockSpec((1, BRANCH), lambda bi, i: (0, 0)),
        pl.BlockSpec((1, KV_A), lambda bi, i: (0, 0)),
        pl.BlockSpec((BRANCH, BRANCH), lambda bi, i: (0, 0)),
    ]
    core_args = (g1n, sh, sc, w, cos_t, sin_t, gq, gk, bd)
    out_specs = [tok(wd) for wd in widths]
    out_shape = [jax.ShapeDtypeStruct((b, t, wd), _BF16) for wd in widths]
    if combine is None:
        return pl.pallas_call(
            functools.partial(_inproj_body, n_lat_tiles=n_lat_tiles),
            grid=(b, nt),
            in_specs=[*_token_specs(n_lat_tiles, ctx_tile, d), *core_specs],
            out_specs=out_specs,
            out_shape=out_shape,
            compiler_params=_cparams(("arbitrary", "arbitrary"), VMEM_LIMIT),
            name="inproj",
        )(x_lat, x_ctx, *core_args)

    pos3, wts, g2, ys = combine
    last_tile = b * nt - 1
    smem_tile = functools.partial(pl.BlockSpec, (1, 2, TM), memory_space=pltpu.SMEM)
    return pl.pallas_call(
        functools.partial(_inproj_combine_body, n_lat_tiles=n_lat_tiles),
        grid=(b, nt),
        in_specs=[
            smem_tile(lambda bi, i: (0, 0, 0)),
            smem_tile(lambda bi, i: (jnp.minimum(bi * nt + i + 1, last_tile), 0, 0)),
            pl.BlockSpec((2, TM), lambda bi, i: (0, bi * nt + i)),
            *_token_specs(n_lat_tiles, ctx_tile, d),
            pl.BlockSpec((1, 1, d), mod_idx),
            pl.BlockSpec(memory_space=pl.ANY),
            *core_specs,
        ],
        out_specs=out_specs + [tok(d)],
        out_shape=out_shape + [jax.ShapeDtypeStruct((b, t, d), _F32)],
        scratch_shapes=[pltpu.VMEM((2, 2, TM, d), _F32), pltpu.SemaphoreType.DMA((2,))],
        compiler_params=_cparams(("arbitrary", "arbitrary"), VMEM_LIMIT),
        name="inproj_combine",
    )(pos3, pos3, wts, x_lat, x_ctx, g2, ys, *core_args)


def _stack_heads(q):
    lane = lax.broadcasted_iota(jnp.int32, q.shape, 1)
    lo = lane < HEAD_DIM
    zero = jnp.zeros_like(q)
    return jnp.concatenate([jnp.where(lo, q, zero), jnp.where(lo, zero, q)], axis=0)


def _unstack_heads(o):
    half = o.shape[0] // 2
    lane = lax.broadcasted_iota(jnp.int32, (half, o.shape[1]), 1)
    return jnp.where(lane < HEAD_DIM, o[:half], o[half:])


KA_CHUNK_MAX = 2048
KA_GROUPS = 2


def _flash_update(lhs, kc, vc, m, l128, acc):
    s = _nt(lhs, kc)
    tiles = [s[:, t * 128:(t + 1) * 128] for t in range(s.shape[1] // 128)]
    m_new = jnp.maximum(m, jnp.max(functools.reduce(jnp.maximum, tiles), axis=1, keepdims=True))
    alpha = jnp.exp2(m - m_new)
    ps = [jnp.exp2(t - m_new) for t in tiles]
    l128 = alpha * l128 + functools.reduce(lambda a, c: a + c, ps)
    p = jnp.concatenate([x.astype(_BF16) for x in ps], axis=1)
    acc = alpha * acc + _dot(p, vc)
    return m_new, l128, acc


def _attn_a_body(q_ref, k_ref, v_ref, o_ref, *, s_len, l_ctx, with_ctx):
    def run(chunks):
        for g in range(KA_GROUPS):
            sl = slice(g * 128, (g + 1) * 128)
            lhs = _stack_heads(q_ref[0, :, sl])
            m = jnp.full((2 * TM, 128), NEG_INF, _F32)
            l128 = jnp.zeros((2 * TM, 128), _F32)
            acc = jnp.zeros((2 * TM, 128), _F32)
            for start, size in chunks:
                m, l128, acc = _flash_update(lhs, k_ref[0, start:start + size, :], v_ref[0, start:start + size, :],
                                             m, l128, acc)
            l = jnp.sum(l128, axis=1, keepdims=True)
            o_ref[0, :, sl] = _unstack_heads(acc / l).astype(_BF16)

    ctx_chunk = (s_len, l_ctx)
    chunk = KA_CHUNK_MAX
    while s_len % chunk:
        chunk //= 2
    lat_chunks = [(c * chunk, chunk) for c in range(s_len // chunk)] + [ctx_chunk]
    if with_ctx:
        is_ctx = pl.program_id(1) == s_len // TM
        pl.when(jnp.logical_not(is_ctx))(lambda: run(lat_chunks))
        pl.when(is_ctx)(lambda: run([ctx_chunk]))
    else:
        run(lat_chunks)


def _attn_a_call(q, k, v, s_len, with_ctx):
    b, t, _ = q.shape
    nq = s_len // TM + (1 if with_ctx else 0)
    return pl.pallas_call(
        functools.partial(_attn_a_body, s_len=s_len, l_ctx=t - s_len, with_ctx=with_ctx),
        grid=(b, nq, 4 // KA_GROUPS),
        in_specs=[
            pl.BlockSpec((1, TM, 128 * KA_GROUPS), lambda bi, i, j: (bi, i, j)),
            pl.BlockSpec((1, t, 128), lambda bi, i, j: (bi, 0, 0)),
            pl.BlockSpec((1, t, 128), lambda bi, i, j: (bi, 0, 0)),
        ],
        out_specs=pl.BlockSpec((1, TM, 128 * KA_GROUPS), lambda bi, i, j: (bi, i, j)),
        out_shape=jax.ShapeDtypeStruct((b, nq * TM, BRANCH), _BF16),
        compiler_params=_cparams(("arbitrary", "arbitrary", "arbitrary"), VMEM_LIMIT),
        name="attn_global",
    )(q, k, v)


def _softmax_parts(parts, extra=None):
    m = functools.reduce(jnp.maximum, [jnp.max(s, axis=1, keepdims=True) for s in parts])
    if extra is not None:
        m = jnp.maximum(m, extra)
    ps = [jnp.exp(s - m) for s in parts]
    l = functools.reduce(lambda a, c: a + c, [jnp.sum(p, axis=1, keepdims=True) for p in ps])
    if extra is not None:
        l = l + jnp.exp(extra - m)
    return ps, l


def _attn_b_body(sink_ref, q_ref, k_ref, v_ref, o_ref, *, s_len, l_ctx, with_ctx):
    span = TM + 2 * WINDOW
    i = pl.program_id(1)
    is_hi = lax.broadcasted_iota(jnp.int32, (2 * TM, 1), 0) >= TM

    def finish(j, ps, vs, l):
        o = functools.reduce(lambda a, c: a + c, [_dot(p.astype(_BF16), v) for p, v in zip(ps, vs)]) / l
        o_ref[0, :, j * 128:(j + 1) * 128] = _unstack_heads(o).astype(_BF16)

    def latent():
        start = i * TM
        ws = pl.multiple_of(jnp.clip(start - WINDOW, 0, s_len - span), WINDOW)
        kw = k_ref[0, pl.ds(ws, span), :]
        vw = v_ref[0, pl.ds(ws, span), :]
        kc = k_ref[0, s_len:s_len + l_ctx, :]
        vc = v_ref[0, s_len:s_len + l_ctx, :]
        row = lax.broadcasted_iota(jnp.int32, (2 * TM, span), 0)
        qpos = start + jnp.where(row >= TM, row - TM, row)
        kpos = ws + lax.broadcasted_iota(jnp.int32, (2 * TM, span), 1)
        band = jnp.abs(kpos - qpos) <= WINDOW
        for j in range(4):
            lhs = _stack_heads(q_ref[0, :, j * 128:(j + 1) * 128])
            s_loc = jnp.where(band, _nt(lhs, kw), NEG_INF)
            s_ctx = _nt(lhs, kc)
            sink = jnp.where(is_hi, sink_ref[4 + j], sink_ref[j])
            ps, l = _softmax_parts([s_loc, s_ctx], sink)
            finish(j, ps, (vw, vc), l)

    def context():
        kc = k_ref[0, s_len:s_len + l_ctx, :]
        vc = v_ref[0, s_len:s_len + l_ctx, :]
        for j in range(4):
            lhs = _stack_heads(q_ref[0, :, j * 128:(j + 1) * 128])
            sink = jnp.where(is_hi, sink_ref[4 + j], sink_ref[j])
            ps, l = _softmax_parts([_nt(lhs, kc)], sink)
            finish(j, ps, (vc,), l)

    if with_ctx:
        is_ctx = i == s_len // TM
        pl.when(jnp.logical_not(is_ctx))(latent)
        pl.when(is_ctx)(context)
    else:
        latent()


def _attn_b_call(q, k, v, sink, s_len, with_ctx):
    b, t, _ = q.shape
    nq = s_len // TM + (1 if with_ctx else 0)
    return pl.pallas_call(
        functools.partial(_attn_b_body, s_len=s_len, l_ctx=t - s_len, with_ctx=with_ctx),
        grid=(b, nq),
        in_specs=[
            pl.BlockSpec(memory_space=pltpu.SMEM),
            pl.BlockSpec((1, TM, BRANCH), lambda bi, i: (bi, i, 0)),
            pl.BlockSpec((1, t, 128), lambda bi, i: (bi, 0, 0)),
            pl.BlockSpec((1, t, 128), lambda bi, i: (bi, 0, 0)),
        ],
        out_specs=pl.BlockSpec((1, TM, BRANCH), lambda bi, i: (bi, i, 0)),
        out_shape=jax.ShapeDtypeStruct((b, nq * TM, BRANCH), _BF16),
        compiler_params=_cparams(("arbitrary", "arbitrary"), VMEM_LIMIT),
        name="attn_window",
    )(sink, q, k, v)


def _attn_n_body(q_ref, k_ref, v_ref, bias_ref, o_ref, *, s_len, l_ctx, with_ctx):
    rows = s_len // GRID_W
    span = NA_WIN_ROWS * GRID_W
    r = pl.program_id(1)

    def finish(j, ps, vs, l):
        o = functools.reduce(lambda a, c: a + c, [_dot(p.astype(_BF16), v) for p, v in zip(ps, vs)]) / l
        o_ref[0, :, j * 128:(j + 1) * 128] = _unstack_heads(o).astype(_BF16)

    def latent():
        u0 = jnp.clip(4 * r - 4, 0, rows - NA_WIN_ROWS)
        ws = pl.multiple_of(u0 * GRID_W, GRID_W)
        for j in range(4):
            sl = slice(j * 128, (j + 1) * 128)
            kw = k_ref[0, pl.ds(ws, span), sl]
            vw = v_ref[0, pl.ds(ws, span), sl]
            kc = k_ref[0, s_len:s_len + l_ctx, sl]
            vc = v_ref[0, s_len:s_len + l_ctx, sl]
            lhs = _stack_heads(q_ref[0, :, sl])
            s_loc = _nt(lhs, kw) + bias_ref[0, j]
            s_ctx = _nt(lhs, kc)
            ps, l = _softmax_parts([s_loc, s_ctx])
            finish(j, ps, (vw, vc), l)

    def context():
        for j in range(4):
            sl = slice(j * 128, (j + 1) * 128)
            lhs = _stack_heads(q_ref[0, :, sl])
            ps, l = _softmax_parts([_nt(lhs, k_ref[0, s_len:s_len + l_ctx, sl])])
            finish(j, ps, (v_ref[0, s_len:s_len + l_ctx, sl],), l)

    if with_ctx:
        is_ctx = r == s_len // TM
        pl.when(jnp.logical_not(is_ctx))(latent)
        pl.when(is_ctx)(context)
    else:
        latent()


def _attn_n_call(q, k, v, bias_tbl, s_len, with_ctx):
    b, t, _ = q.shape
    n_r = s_len // TM
    nq = n_r + (1 if with_ctx else 0)
    span = NA_WIN_ROWS * GRID_W

    def case(ri):
        return jnp.where(ri == 0, 0, jnp.where(ri >= n_r - 1, 2, 1))

    return pl.pallas_call(
        functools.partial(_attn_n_body, s_len=s_len, l_ctx=t - s_len, with_ctx=with_ctx),
        grid=(b, nq),
        in_specs=[
            pl.BlockSpec((1, TM, BRANCH), lambda bi, ri: (bi, ri, 0)),
            pl.BlockSpec((1, t, BRANCH), lambda bi, ri: (bi, 0, 0)),
            pl.BlockSpec((1, t, BRANCH), lambda bi, ri: (bi, 0, 0)),
            pl.BlockSpec((1, 4, 2 * TM, span), lambda bi, ri: (case(ri), 0, 0, 0)),
        ],
        out_specs=pl.BlockSpec((1, TM, BRANCH), lambda bi, ri: (bi, ri, 0)),
        out_shape=jax.ShapeDtypeStruct((b, nq * TM, BRANCH), _BF16),
        compiler_params=_cparams(("arbitrary", "arbitrary"), VMEM_LIMIT),
        name="attn_neighbourhood",
    )(q, k, v, bias_tbl)


def _na_tables(rows):
    a = np.arange(4)[:, None, None, None]
    c = np.arange(GRID_W)[None, :, None, None]
    i = np.arange(NA_WIN_ROWS)[None, None, :, None]
    kc = np.arange(GRID_W)[None, None, None, :]
    c0 = np.clip(c - NA_COLS // 2, 0, GRID_W - NA_COLS)
    col_ok = (kc >= c0) & (kc < c0 + NA_COLS)
    cidx = kc - c + (NA_COLS - 1)
    row_ok = [i < NA_ROWS + 0 * a, (i >= a) & (i < a + NA_ROWS), i >= NA_WIN_ROWS - NA_ROWS + 0 * a]
    ridx = [i - a + 7, i - a + 3, i - a - 1]
    shape = (4, GRID_W, NA_WIN_ROWS, GRID_W)
    sel_r, sel_c, ok = [], [], []
    for cs in range(3):
        valid = np.broadcast_to(row_ok[cs] & col_ok, shape)
        ok.append(valid.reshape(4 * GRID_W, NA_WIN_ROWS * GRID_W))
        rsel = (np.broadcast_to(ridx[cs], (4, 1, NA_WIN_ROWS, 1))[:, 0, :, 0][..., None]
                == np.arange(2 * NA_ROWS - 1)[None, None, :])
        sel_r.append(rsel.astype(np.float32))
    csel = (np.broadcast_to(cidx, (1, GRID_W, 1, GRID_W))[0, :, 0, :][..., None]
            == np.arange(2 * NA_COLS - 1)[None, None, :]).astype(np.float32)
    return np.stack(sel_r), csel, np.stack(ok)


def _na_bias_table(rpb, rows):
    sel_r, sel_c, ok = _na_tables(rows)
    hi = lax.Precision.HIGHEST
    t = jnp.einsum("sair,hrg->shaig", jnp.asarray(sel_r), rpb.astype(_F32), precision=hi)
    t = jnp.einsum("shaig,ckg->shacik", t, jnp.asarray(sel_c), precision=hi)
    t = t.reshape(3, N_HEADS, 4 * GRID_W, NA_WIN_ROWS * GRID_W)
    t = jnp.where(jnp.asarray(ok)[:, None], t, NEG_INF)
    return t.reshape(3, 4, 2 * TM, NA_WIN_ROWS * GRID_W)


def _route_rows(sel, sc):
    gs, gi1, gi2, gs1, gs2 = [], [], [], [], []
    for g in range(N_GROUPS):
        v = sel[4 * g:4 * g + 4]
        s = sc[4 * g:4 * g + 4]
        m1 = jnp.maximum(jnp.maximum(v[0], v[1]), jnp.maximum(v[2], v[3]))
        i1 = jnp.where(v[0] == m1, 0, jnp.where(v[1] == m1, 1, jnp.where(v[2] == m1, 2, 3)))
        rest = [jnp.where(i1 == k, NEG_INF, v[k]) for k in range(4)]
        m2 = jnp.maximum(jnp.maximum(rest[0], rest[1]), jnp.maximum(rest[2], rest[3]))
        i2 = jnp.where(rest[0] == m2, 0, jnp.where(rest[1] == m2, 1, jnp.where(rest[2] == m2, 2, 3)))

        def pick(idx, s=s):
            return jnp.where(idx == 0, s[0], jnp.where(idx == 1, s[1], jnp.where(idx == 2, s[2], s[3])))

        gs.append(m1 + m2)
        gi1.append(i1 + 4 * g)
        gi2.append(i2 + 4 * g)
        gs1.append(pick(i1))
        gs2.append(pick(i2))
    best = jnp.maximum(jnp.maximum(gs[0], gs[1]), jnp.maximum(gs[2], gs[3]))
    gsel = jnp.where(gs[0] == best, 0, jnp.where(gs[1] == best, 1, jnp.where(gs[2] == best, 2, 3)))

    def pickg(vals):
        return jnp.where(gsel == 0, vals[0], jnp.where(gsel == 1, vals[1], jnp.where(gsel == 2, vals[2], vals[3])))

    return pickg(gi1), pickg(gi2), pickg(gs1), pickg(gs2)


def _merge_body(oa_ref, ob_ref, on_ref, gate_ref, x_ref, xctx_ref, wb_ref, wo_ref, g1_ref, sh2_ref, sc2_ref, n2_ref,
                wrh_ref, wrl_ref, rb_ref,
                x1_ref, h2_ref, idx_ref, rank_ref, wts_ref, cnt_ref, carry_ref, *, n_lat_tiles):
    first = (pl.program_id(0) == 0) & (pl.program_id(1) == 0)
    x_res = jnp.where(pl.program_id(1) < n_lat_tiles, x_ref[0], xctx_ref[0])

    @pl.when(first)
    def _():
        carry_ref[...] = jnp.zeros_like(carry_ref)

    outs = (oa_ref, ob_ref, on_ref)
    y = None
    for n in range(3):
        gt = jax.nn.sigmoid(gate_ref[0, :, n * D_MODEL:(n + 1) * D_MODEL].astype(_F32))
        term = gt * _dot(outs[n][0], wb_ref[n])
        y = term if y is None else y + term
    z = _dot(y.astype(_BF16), wo_ref[...])
    x1 = x_res + g1_ref[0] * z
    x1_ref[0] = x1
    ms = jnp.mean(x1 * x1, axis=-1, keepdims=True)
    h2 = (x1 * lax.rsqrt(ms + EPS) * n2_ref[...]) * (1.0 + sc2_ref[0]) + sh2_ref[0]
    h2_ref[0] = h2

    hb = h2.astype(_BF16)
    hl = (h2 - hb.astype(_F32)).astype(_BF16)
    logits = _nt(wrh_ref[...], hb) + _nt(wrh_ref[...], hl) + _nt(wrl_ref[...], hb)
    scores = jax.nn.sigmoid(logits)
    sel = scores + rb_ref[...]
    sel_rows = [sel[e:e + 1, :] for e in range(N_EXPERTS)]
    sc_rows = [scores[e:e + 1, :] for e in range(N_EXPERTS)]
    e1, e2, s1, s2 = _route_rows(sel_rows, sc_rows)
    den = s1 + s2
    idx_ref[0:1, :] = e1
    idx_ref[1:2, :] = e2
    wts_ref[0:1, :] = s1 / den
    wts_ref[1:2, :] = s2 / den

    eid = lax.broadcasted_iota(jnp.int32, (N_EXPERTS, TM), 0)
    oh1 = eid == e1
    oh2 = eid == e2
    tri = (lax.broadcasted_iota(jnp.int32, (TM, TM), 0) <= lax.broadcasted_iota(jnp.int32, (TM, TM), 1))
    tri = tri.astype(_BF16)
    cum1 = _dot(oh1.astype(_BF16), tri)
    cum2 = _dot(oh2.astype(_BF16), tri)
    tot1 = cum1[:, TM - 1:TM]
    tot2 = cum2[:, TM - 1:TM]
    carry = carry_ref[:, 0:1]
    r1 = jnp.sum(jnp.where(oh1, carry + cum1 - 1.0, 0.0), axis=0, keepdims=True)
    r2 = jnp.sum(jnp.where(oh2, carry + tot1 + cum2 - 1.0, 0.0), axis=0, keepdims=True)
    rank_ref[0:1, :] = r1.astype(jnp.int32)
    rank_ref[1:2, :] = r2.astype(jnp.int32)
    new_carry = carry + tot1 + tot2
    carry_ref[...] = jnp.broadcast_to(new_carry, carry_ref.shape)
    cnt_ref[...] = jnp.broadcast_to(new_carry, cnt_ref.shape).astype(jnp.int32)


def _merge_call(oa, ob, on, gate, x_lat, x_ctx, ctx_tile, wb, wo, g1, sh2, sc2, n2, wrh, wrl, rb, t_out,
                n_lat_tiles):
    b, _, d = x_lat.shape
    nb = g1.shape[0] - 1
    nt = t_out // TM
    n_tok = b * t_out

    def mod_idx(bi, i):
        return (jnp.where(i < n_lat_tiles, bi, nb), 0, 0)

    def tok(width):
        return pl.BlockSpec((1, TM, width), lambda bi, i: (bi, i, 0))

    def full(shape):
        return pl.BlockSpec(shape, lambda bi, i: (0,) * len(shape))

    lane_out = pl.BlockSpec((2, TM), lambda bi, i: (0, bi * nt + i))
    return pl.pallas_call(
        functools.partial(_merge_body, n_lat_tiles=n_lat_tiles),
        grid=(b, nt),
        in_specs=[tok(BRANCH), tok(BRANCH), tok(BRANCH), tok(3 * d), *_token_specs(n_lat_tiles, ctx_tile, d),
                  full((3, BRANCH, d)), full((d, d)),
                  pl.BlockSpec((1, 1, d), mod_idx), pl.BlockSpec((1, 1, d), mod_idx),
                  pl.BlockSpec((1, 1, d), mod_idx), full((1, d)),
                  full((N_EXPERTS, d)), full((N_EXPERTS, d)), full((N_EXPERTS, 1))],
        out_specs=[tok(d), tok(d), lane_out, lane_out, lane_out, full((N_EXPERTS, 128))],
        out_shape=[jax.ShapeDtypeStruct((b, t_out, d), _F32), jax.ShapeDtypeStruct((b, t_out, d), _F32),
                   jax.ShapeDtypeStruct((2, n_tok), jnp.int32), jax.ShapeDtypeStruct((2, n_tok), jnp.int32),
                   jax.ShapeDtypeStruct((2, n_tok), _F32), jax.ShapeDtypeStruct((N_EXPERTS, 128), jnp.int32)],
        scratch_shapes=[pltpu.VMEM((N_EXPERTS, 128), _F32)],
        compiler_params=_cparams(("arbitrary", "arbitrary"), VMEM_LIMIT),
        name="merge_route",
    )(oa, ob, on, gate, x_lat, x_ctx, wb, wo, g1, sh2, sc2, n2, wrh, wrl, rb)


ROW_DMA_UNROLL = 8


def _dispatch_body(pad_start_ref, pad_n_ref, nv_ref, pos_ref, h_ref, xs_ref, zbuf, sem, zsem):
    @pl.when(pl.program_id(0) == 0)
    def _():
        zbuf[...] = jnp.zeros_like(zbuf)

        def zero_row(e, r):
            return pltpu.make_async_copy(zbuf.at[pl.ds(0, 1)], xs_ref.at[pl.ds(pad_start_ref[e] + r, 1)], zsem)

        def zero_tile(i):
            return pltpu.make_async_copy(zbuf, xs_ref.at[pl.ds(pl.multiple_of(i * TM_E, TM_E), TM_E)], zsem)

        n_tiles = xs_ref.shape[0] // TM_E
        for e in range(N_EXPERTS):
            lax.fori_loop(0, pad_n_ref[e], lambda r, c, e=e: (zero_row(e, r).start(), c)[1], 0)
        lax.fori_loop(nv_ref[0], n_tiles, lambda i, c: (zero_tile(i).start(), c)[1], 0)
        for e in range(N_EXPERTS):
            lax.fori_loop(0, pad_n_ref[e], lambda r, c, e=e: (zero_row(e, r).wait(), c)[1], 0)
        lax.fori_loop(nv_ref[0], n_tiles, lambda i, c: (zero_tile(i).wait(), c)[1], 0)

    def row_copy(t, k):
        return pltpu.make_async_copy(h_ref.at[pl.ds(t, 1)], xs_ref.at[pl.ds(pos_ref[0, k, t], 1)], sem)

    def issue(t, carry):
        row_copy(t, 0).start()
        row_copy(t, 1).start()
        return carry

    lax.fori_loop(0, TM, issue, 0, unroll=ROW_DMA_UNROLL)

    def drain(t, carry):
        row_copy(t, 0).wait()
        row_copy(t, 1).wait()
        return carry

    lax.fori_loop(0, TM, drain, 0, unroll=ROW_DMA_UNROLL)


def _dispatch_call(pad_start, pad_n, n_valid, pos3, h2, p_rows):
    n_tiles = pos3.shape[0]
    d = h2.shape[1]
    return pl.pallas_call(
        _dispatch_body,
        grid_spec=pltpu.PrefetchScalarGridSpec(
            num_scalar_prefetch=3,
            grid=(n_tiles,),
            in_specs=[pl.BlockSpec((1, 2, TM), lambda i, ps, pn, nv: (i, 0, 0), memory_space=pltpu.SMEM),
                      pl.BlockSpec((TM, d), lambda i, ps, pn, nv: (i, 0))],
            out_specs=pl.BlockSpec(memory_space=pl.ANY),
            scratch_shapes=[pltpu.VMEM((TM_E, d), h2.dtype), pltpu.SemaphoreType.DMA, pltpu.SemaphoreType.DMA],
        ),
        out_shape=jax.ShapeDtypeStruct((p_rows, d), h2.dtype),
        compiler_params=_cparams(("arbitrary",)),
        name="moe_dispatch",
    )(pad_start, pad_n, n_valid, pos3, h2)


def _moe_body(te_ref, nv_ref, xs_ref, wg_ref, wu_ref, wd_ref, ys_ref):
    del te_ref
    live = pl.program_id(0) < nv_ref[0]

    @pl.when(live)
    def _():
        xb = xs_ref[...].astype(_BF16)
        gate = _dot(xb, wg_ref[0])
        up = _dot(xb, wu_ref[0])
        act = (gate * jax.nn.sigmoid(gate)) * up
        ys_ref[...] = _dot(act.astype(_BF16), wd_ref[0])

    @pl.when(jnp.logical_not(live))
    def _():
        ys_ref[...] = jnp.zeros_like(ys_ref)


def _moe_call(tile_expert, n_valid, xs, wg, wu, wd):
    p_rows, d = xs.shape
    n_tiles = p_rows // TM_E

    def row_idx(i, te, nv):
        return (jnp.minimum(i, nv[0] - 1), 0)

    def out_idx(i, te, nv):
        return (i, 0)

    return pl.pallas_call(
        _moe_body,
        grid_spec=pltpu.PrefetchScalarGridSpec(
            num_scalar_prefetch=2,
            grid=(n_tiles,),
            in_specs=[pl.BlockSpec((TM_E, d), row_idx),
                      pl.BlockSpec((1, d, D_FF), lambda i, te, nv: (te[i], 0, 0)),
                      pl.BlockSpec((1, d, D_FF), lambda i, te, nv: (te[i], 0, 0)),
                      pl.BlockSpec((1, D_FF, d), lambda i, te, nv: (te[i], 0, 0))],
            out_specs=pl.BlockSpec((TM_E, d), out_idx),
        ),
        out_shape=jax.ShapeDtypeStruct((p_rows, d), _F32),
        compiler_params=_cparams(("arbitrary",), VMEM_LIMIT),
        name="moe_experts",
    )(tile_expert, n_valid, xs, wg, wu, wd)


def _combine_body(pos0_ref, posn_ref, w_ref, x1_ref, g2_ref, fn_ref, ys_ref, o_ref, gbuf, sems):
    step = pl.program_id(0)
    y, issue_next, finish = _moe_rows_pipeline(step, pl.num_programs(0) - 1, pos0_ref, posn_ref, w_ref, ys_ref,
                                               gbuf, sems)
    x2 = x1_ref[...] + g2_ref[0] * y
    issue_next()
    ms = jnp.mean(x2 * x2, axis=-1, keepdims=True)
    o_ref[...] = x2 * lax.rsqrt(ms + EPS) * fn_ref[...]
    finish()


def _combine_call(pos3, wts, x1, g2, fnorm, ys, t_out, n_lat_tiles):
    n_tok, d = x1.shape
    nb = g2.shape[0] - 1
    nt = t_out // TM
    n_tiles = n_tok // TM

    def mod_idx(i):
        return (jnp.where(i % nt < n_lat_tiles, i // nt, nb), 0, 0)

    smem_tile = functools.partial(pl.BlockSpec, (1, 2, TM), memory_space=pltpu.SMEM)
    return pl.pallas_call(
        _combine_body,
        grid=(n_tiles,),
        in_specs=[smem_tile(lambda i: (0, 0, 0)),
                  smem_tile(lambda i: (jnp.minimum(i + 1, n_tiles - 1), 0, 0)),
                  pl.BlockSpec((2, TM), lambda i: (0, i)),
                  pl.BlockSpec((TM, d), lambda i: (i, 0)),
                  pl.BlockSpec((1, 1, d), mod_idx),
                  pl.BlockSpec((1, d), lambda i: (0, 0)),
                  pl.BlockSpec(memory_space=pl.ANY)],
        out_specs=pl.BlockSpec((TM, d), lambda i: (i, 0)),
        out_shape=jax.ShapeDtypeStruct((n_tok, d), _F32),
        scratch_shapes=[pltpu.VMEM((2, 2, TM, d), _F32), pltpu.SemaphoreType.DMA((2,))],
        compiler_params=_cparams(("arbitrary",)),
        name="moe_combine",
    )(pos3, pos3, wts, x1, g2, fnorm, ys)


def _rope_tables(s_len, l_ctx):
    t = jnp.arange(s_len, dtype=jnp.int32)
    row = (t // GRID_W).astype(_F32)
    col = (t % GRID_W).astype(_F32)
    half = HEAD_DIM // 2
    inv = ROPE_THETA ** (-jnp.arange(0, half, 2, dtype=_F32) / half)
    ar = row[:, None] * inv
    ac = col[:, None] * inv
    cos64 = jnp.concatenate([jnp.cos(ar), jnp.cos(ar), jnp.cos(ac), jnp.cos(ac)], axis=1)
    sin64 = jnp.concatenate([-jnp.sin(ar), jnp.sin(ar), -jnp.sin(ac), jnp.sin(ac)], axis=1)
    cos_t = jnp.concatenate([jnp.tile(cos64, (1, 2)), jnp.ones((l_ctx, 128), _F32)], axis=0)
    sin_t = jnp.concatenate([jnp.tile(sin64, (1, 2)), jnp.zeros((l_ctx, 128), _F32)], axis=0)
    return cos_t, sin_t


def _block_diag_ones():
    r = np.arange(BRANCH)
    return jnp.asarray((r[:, None] // HEAD_DIM) == (r[None, :] // HEAD_DIM), dtype=_BF16)


def kernel(x, c, ctx, c_ctx, w_mod, b_mod, norm1, norm2, w_in, q_norm_a, k_norm_a, sink_b, rpb_n, w_branch, w_out,
           w_router, router_bias, w_e_gate, w_e_up, w_e_down, final_norm):
    b, s_len, d = x.shape
    l_ctx = ctx.shape[1]
    depth = w_in.shape[0]
    t_all = s_len + l_ctx
    rows = s_len // GRID_W
    assert d == D_MODEL and l_ctx == TM and s_len % TM == 0 and b + 1 <= 16
    assert s_len >= TM + 2 * WINDOW and rows >= NA_WIN_ROWS and rows % 4 == 0
    n_lat_tiles = s_len // TM

    cos_t, sin_t = _rope_tables(s_len, l_ctx)
    bd = _block_diag_ones()
    cvec = jnp.zeros((16, d), _F32).at[:b].set(c).at[b].set(c_ctx)
    mods = _mod_call(cvec, w_mod, b_mod)[:, :b + 1].reshape(depth, b + 1, 6, 1, d)

    wr = w_router.T.astype(_F32)
    wrh = wr.astype(_BF16)
    wrl = (wr - wrh.astype(_F32)).astype(_BF16)
    rb = router_bias.astype(_F32).reshape(N_EXPERTS, 1)

    x_lat, x_ctx, ctx_tile = x, ctx, 0
    pending = None
    for l in range(depth):
        last = l == depth - 1
        sh1, sc1, g1, sh2, sc2, g2 = [mods[l, :, n] for n in range(6)]
        w_f = w_in[l]
        w_l = jnp.concatenate([_pair_heads(w_f[:, OFF_QA:OFF_KA], 1), w_f[:, OFF_KA:OFF_QB],
                               _pair_heads(w_f[:, OFF_QB:OFF_KB], 1), w_f[:, OFF_KB:]], axis=1).astype(_BF16)
        gq = (jnp.tile(q_norm_a[l].astype(_F32), N_HEADS) * (ATTN_SCALE * LOG2_E)).reshape(1, BRANCH)
        gk = jnp.tile(k_norm_a[l].astype(_F32), 2).reshape(1, KV_A)
        outs = _inproj_call(x_lat, x_ctx, ctx_tile, n_lat_tiles, norm1[l].reshape(1, d), sh1, sc1, w_l, cos_t, sin_t,
                            gq, gk, bd, combine=pending)
        qa, ka, va, qb, kb, vb, qn, kn, vn, gate = outs[:10]
        if pending is not None:
            x_lat = x_ctx = outs[10]

        sink = sink_b[l].astype(_F32)
        oa = _attn_a_call(qa, ka, va, s_len, not last)
        ob = _attn_b_call(qb, kb, vb, sink, s_len, not last)
        on = _attn_n_call(qn, kn, vn, _na_bias_table(rpb_n[l], rows), s_len, not last)

        t_out = s_len if last else t_all
        wb = jnp.stack([_pair_heads(w_branch[l, 0], 0), _pair_heads(w_branch[l, 1], 0),
                        w_branch[l, 2]]).astype(_BF16)
        x1, h2, idx, rank, wts, cnt = _merge_call(
            oa, ob, on, gate, x_lat, x_ctx, ctx_tile, wb, w_out[l].astype(_BF16), g1, sh2, sc2,
            norm2[l].reshape(1, d), wrh, wrl, rb, t_out, n_lat_tiles)

        n_tok = b * t_out
        counts = cnt[:, 0]
        padded = ((counts + TM_E - 1) // TM_E) * TM_E
        ends = jnp.cumsum(padded)
        offs = ends - padded
        pos = rank + jnp.sum(jnp.where(idx[None] == jnp.arange(N_EXPERTS)[:, None, None], offs[:, None, None], 0), axis=0)
        pos3 = pos.reshape(2, n_tok // TM, TM).transpose(1, 0, 2)
        p_rows = 2 * n_tok + N_EXPERTS * TM_E
        n_tiles = p_rows // TM_E
        n_valid = (ends[-1] // TM_E).astype(jnp.int32).reshape(1)
        tile_start = jnp.arange(n_tiles, dtype=jnp.int32) * TM_E
        tile_expert = jnp.minimum(jnp.sum(tile_start[:, None] >= ends[None, :], axis=1), N_EXPERTS - 1).astype(jnp.int32)

        pad_start = (offs + counts).astype(jnp.int32)
        pad_n = (padded - counts).astype(jnp.int32)
        xs = _dispatch_call(pad_start, pad_n, n_valid, pos3, h2.reshape(n_tok, d), p_rows)
        ys = _moe_call(tile_expert, n_valid, xs, w_e_gate[l].astype(_BF16), w_e_up[l].astype(_BF16),
                       w_e_down[l].astype(_BF16))
        if last:
            out = _combine_call(pos3, wts, x1.reshape(n_tok, d), g2, final_norm.reshape(1, d), ys, t_out,
                                n_lat_tiles)
            return out.reshape(b, t_out, d)
        pending = (pos3, wts, g2, ys)
        x_lat = x_ctx = x1
        ctx_tile = n_lat_tiles
```

```python
import functools

import numpy as np
import jax
import jax.numpy as jnp
from jax import lax
from jax.experimental import pallas as pl
from jax.experimental.pallas import tpu as pltpu

D_MODEL = 1024
HEAD_DIM = 64
GRID_W = 64
N_HEADS = 8
WINDOW = 128
NA_ROWS = 8
NA_COLS = 16
ROPE_THETA = 10000.0
N_EXPERTS = 16
N_GROUPS = 4
EXPERTS_PER_GROUP = N_EXPERTS // N_GROUPS
D_FF = 512
EPS = 1e-6
NEG_INF = -1e30
ATTN_SCALE = HEAD_DIM ** -0.5
LOG2_E = 1.4426950408889634
Q_SCALE = ATTN_SCALE * LOG2_E
BRANCH = N_HEADS * HEAD_DIM
KV_A = 2 * HEAD_DIM
IN_COLS = 6144

OFF_QA, OFF_KA, OFF_VA = 0, 512, 640
OFF_QB, OFF_KB, OFF_VB = 768, 1280, 1408
OFF_QN, OFF_KN, OFF_VN = 1536, 2048, 2560
OFF_GATE = 3072

TM = 256
TM_E = 512
NA_WIN_ROWS = 12
VMEM_LIMIT = 56 * 1024 * 1024

_F32 = jnp.float32
_BF16 = jnp.bfloat16

def _pair_heads(w, axis):
    shape = w.shape
    split = shape[:axis] + (2, 4, HEAD_DIM) + shape[axis + 1:]
    return jnp.swapaxes(w.reshape(split), axis, axis + 1).reshape(shape)


def _nt(a, b):
    return lax.dot_general(a, b, (((1,), (1,)), ((), ())), preferred_element_type=_F32)


def _dot(a, b):
    return jnp.dot(a, b, preferred_element_type=_F32)


def _cparams(sem, vmem=None):
    return pltpu.CompilerParams(dimension_semantics=sem, vmem_limit_bytes=vmem)


def _mod_body(c_ref, w_ref, b_ref, o_ref):
    cv = c_ref[...]
    a = (cv * jax.nn.sigmoid(cv)).astype(_BF16)
    o_ref[0] = _dot(a, w_ref[0].astype(_BF16)) + b_ref[0]


def _mod_call(cvec, w_mod, b_mod):
    depth, d, n = w_mod.shape
    tn = 1536
    return pl.pallas_call(
        _mod_body,
        grid=(depth, n // tn),
        in_specs=[
            pl.BlockSpec((16, d), lambda l, j: (0, 0)),
            pl.BlockSpec((1, d, tn), lambda l, j: (l, 0, j)),
            pl.BlockSpec((1, 1, tn), lambda l, j: (l, 0, j)),
        ],
        out_specs=pl.BlockSpec((1, 16, tn), lambda l, j: (l, 0, j)),
        out_shape=jax.ShapeDtypeStruct((depth, 16, n), _F32),
        compiler_params=_cparams(("arbitrary", "arbitrary")),
        name="mod_vectors",
    )(cvec, w_mod, b_mod.reshape(depth, 1, n))


def _rope(v, c, s):
    w = v.shape[1]
    lane = lax.broadcasted_iota(jnp.int32, v.shape, 1)
    first = (lane & 16) == 0
    sw = jnp.where(first, pltpu.roll(v, w - 16, 1), pltpu.roll(v, 16, 1))
    return v * c + sw * s


def _inproj_body(x_ref, xctx_ref, *refs, n_lat_tiles):
    x = jnp.where(pl.program_id(1) < n_lat_tiles, x_ref[0], xctx_ref[0])
    _inproj_core(x, *refs)


def _col_from_row(row):
    blk = jnp.broadcast_to(row, (128, 128))
    return blk.T[:, 0:1]


def _moe_rows_pipeline(step, last_step, pos0_ref, posn_ref, w_ref, ys_ref, gbuf, sems):
    def row_copy(pos_ref, slot, t, k):
        return pltpu.make_async_copy(ys_ref.at[pl.ds(pos_ref[0, k, t], 1)], gbuf.at[slot, k, pl.ds(t, 1)],
                                     sems.at[slot])

    def gather(pos_ref, slot):
        for t in range(TM):
            row_copy(pos_ref, slot, t, 0).start()
            row_copy(pos_ref, slot, t, 1).start()

    def drain(slot):
        for t in range(TM):
            row_copy(pos0_ref, slot, t, 0).wait()
            row_copy(pos0_ref, slot, t, 1).wait()

    pl.when(step == 0)(lambda: gather(pos0_ref, 0))
    slot = step % 2
    drain(slot)
    w = w_ref[...]
    cols = [jnp.concatenate([_col_from_row(w[k:k + 1, h * 128:(h + 1) * 128]) for h in range(TM // 128)], axis=0)
            for k in range(2)]
    y = cols[0] * gbuf[slot, 0] + cols[1] * gbuf[slot, 1]
    return (y, lambda: gather(posn_ref, (step + 1) % 2),
            lambda: pl.when(step == last_step)(lambda: drain((step + 1) % 2)))


def _inproj_combine_body(pos0_ref, posn_ref, w_ref, x_ref, xctx_ref, g2_ref, ys_ref, *refs, n_lat_tiles):
    core_refs, x2_ref, gbuf, sems = refs[:-3], refs[-3], refs[-2], refs[-1]
    nt = pl.num_programs(1)
    step = pl.program_id(0) * nt + pl.program_id(1)
    y, issue_next, finish = _moe_rows_pipeline(step, pl.num_programs(0) * nt - 1, pos0_ref, posn_ref, w_ref,
                                               ys_ref, gbuf, sems)
    x1 = jnp.where(pl.program_id(1) < n_lat_tiles, x_ref[0], xctx_ref[0])
    x = x1 + g2_ref[0] * y
    x2_ref[0] = x
    issue_next()
    _inproj_core(x, *core_refs)
    finish()


def _inproj_core(x, g_ref, sh_ref, sc_ref, w_ref, cos_ref, sin_ref, gq_ref, gk_ref, bd_ref,
                 qa_ref, ka_ref, va_ref, qb_ref, kb_ref, vb_ref, qn_ref, kn_ref, vn_ref, gate_ref):
    ms = jnp.mean(x * x, axis=-1, keepdims=True)
    h = (x * lax.rsqrt(ms + EPS) * g_ref[...]) * (1.0 + sc_ref[0]) + sh_ref[0]
    hb = h.astype(_BF16)

    def seg(off, width):
        return _dot(hb, w_ref[:, off:off + width])

    c1 = cos_ref[...]
    s1 = sin_ref[...]
    c4 = jnp.concatenate([c1] * 4, axis=1)
    s4 = jnp.concatenate([s1] * 4, axis=1)
    bd = bd_ref[...]

    qa = seg(OFF_QA, BRANCH)
    ssq = _dot((qa * qa).astype(_BF16), bd)
    qa = qa * lax.rsqrt(ssq * (1.0 / HEAD_DIM) + EPS) * gq_ref[...]
    qa_ref[0] = _rope(qa, c4, s4).astype(_BF16)
    ka = seg(OFF_KA, KV_A)
    ssk = _dot((ka * ka).astype(_BF16), bd_ref[:KV_A, :KV_A])
    ka = ka * lax.rsqrt(ssk * (1.0 / HEAD_DIM) + EPS) * gk_ref[...]
    ka_ref[0] = _rope(ka, c1, s1).astype(_BF16)
    va_ref[0] = seg(OFF_VA, KV_A).astype(_BF16)
    qb_ref[0] = _rope(seg(OFF_QB, BRANCH) * Q_SCALE, c4, s4).astype(_BF16)
    kb_ref[0] = _rope(seg(OFF_KB, KV_A), c1, s1).astype(_BF16)
    vb_ref[0] = seg(OFF_VB, KV_A).astype(_BF16)
    qn_ref[0] = (seg(OFF_QN, BRANCH) * Q_SCALE).astype(_BF16)
    kn_ref[0] = seg(OFF_KN, BRANCH).astype(_BF16)
    vn_ref[0] = seg(OFF_VN, BRANCH).astype(_BF16)
    for n in range(3):
        gate_ref[0, :, n * D_MODEL:(n + 1) * D_MODEL] = seg(OFF_GATE + n * D_MODEL, D_MODEL).astype(_BF16)


def _token_specs(n_lat_tiles, ctx_tile, d):
    lat = pl.BlockSpec((1, TM, d), lambda bi, i: (bi, jnp.minimum(i, n_lat_tiles - 1), 0))
    ctx = pl.BlockSpec((1, TM, d), lambda bi, i: (bi, ctx_tile, 0))
    return lat, ctx


def _inproj_call(x_lat, x_ctx, ctx_tile, n_lat_tiles, g1n, sh, sc, w, cos_t, sin_t, gq, gk, bd, combine=None):
    b, _, d = x_lat.shape
    nb = sh.shape[0] - 1
    nt = n_lat_tiles + 1
    t = nt * TM

    def mod_idx(bi, i):
        return (jnp.where(i < n_lat_tiles, bi, nb), 0, 0)

    def tok(width):
        return pl.BlockSpec((1, TM, width), lambda bi, i: (bi, i, 0))

    widths = [BRANCH, KV_A, KV_A, BRANCH, KV_A, KV_A, BRANCH, BRANCH, BRANCH, 3 * D_MODEL]
    core_specs = [
        pl.BlockSpec((1, d), lambda bi, i: (0, 0)),
        pl.BlockSpec((1, 1, d), mod_idx),
        pl.BlockSpec((1, 1, d), mod_idx),
        pl.BlockSpec((d, IN_COLS), lambda bi, i: (0, 0)),
        pl.BlockSpec((TM, 128), lambda bi, i: (i, 0)),
        pl.BlockSpec((TM, 128), lambda bi, i: (i, 0)),
        pl.BlockSpec((1, BRANCH), lambda bi, i: (0, 0)),
        pl.BlockSpec((1, KV_A), lambda bi, i: (0, 0)),
        pl.BlockSpec((BRANCH, BRANCH), lambda bi, i: (0, 0)),
    ]
    core_args = (g1n, sh, sc, w, cos_t, sin_t, gq, gk, bd)
    out_specs = [tok(wd) for wd in widths]
    out_shape = [jax.ShapeDtypeStruct((b, t, wd), _BF16) for wd in widths]
    if combine is None:
        return pl.pallas_call(
            functools.partial(_inproj_body, n_lat_tiles=n_lat_tiles),
            grid=(b, nt),
            in_specs=[*_token_specs(n_lat_tiles, ctx_tile, d), *core_specs],
            out_specs=out_specs,
            out_shape=out_shape,
            compiler_params=_cparams(("arbitrary", "arbitrary"), VMEM_LIMIT),
            name="inproj",
        )(x_lat, x_ctx, *core_args)

    pos3, wts, g2, ys = combine
    last_tile = b * nt - 1
    smem_tile = functools.partial(pl.BlockSpec, (1, 2, TM), memory_space=pltpu.SMEM)
    return pl.pallas_call(
        functools.partial(_inproj_combine_body, n_lat_tiles=n_lat_tiles),
        grid=(b, nt),
        in_specs=[
            smem_tile(lambda bi, i: (0, 0, 0)),
            smem_tile(lambda bi, i: (jnp.minimum(bi * nt + i + 1, last_tile), 0, 0)),
            pl.BlockSpec((2, TM), lambda bi, i: (0, bi * nt + i)),
            *_token_specs(n_lat_tiles, ctx_tile, d),
            pl.BlockSpec((1, 1, d), mod_idx),
            pl.BlockSpec(memory_space=pl.ANY),
            *core_specs,
        ],
        out_specs=out_specs + [tok(d)],
        out_shape=out_shape + [jax.ShapeDtypeStruct((b, t, d), _F32)],
        scratch_shapes=[pltpu.VMEM((2, 2, TM, d), _F32), pltpu.SemaphoreType.DMA((2,))],
        compiler_params=_cparams(("arbitrary", "arbitrary"), VMEM_LIMIT),
        name="inproj_combine",
    )(pos3, pos3, wts, x_lat, x_ctx, g2, ys, *core_args)


def _stack_heads(q):
    lane = lax.broadcasted_iota(jnp.int32, q.shape, 1)
    lo = lane < HEAD_DIM
    zero = jnp.zeros_like(q)
    return jnp.concatenate([jnp.where(lo, q, zero), jnp.where(lo, zero, q)], axis=0)


def _unstack_heads(o):
    half = o.shape[0] // 2
    lane = lax.broadcasted_iota(jnp.int32, (half, o.shape[1]), 1)
    return jnp.where(lane < HEAD_DIM, o[:half], o[half:])


KA_CHUNK_MAX = 2048
KA_GROUPS = 4


def _flash_update(lhs, kc, vc, m, l128, acc):
    s = _nt(lhs, kc)
    tiles = [s[:, t * 128:(t + 1) * 128] for t in range(s.shape[1] // 128)]
    m_new = jnp.maximum(m, jnp.max(functools.reduce(jnp.maximum, tiles), axis=1, keepdims=True))
    alpha = jnp.exp2(m - m_new)
    ps = [jnp.exp2(t - m_new) for t in tiles]
    l128 = alpha * l128 + functools.reduce(lambda a, c: a + c, ps)
    p = jnp.concatenate([x.astype(_BF16) for x in ps], axis=1)
    acc = alpha * acc + _dot(p, vc)
    return m_new, l128, acc


def _attn_a_body(q_ref, k_ref, v_ref, o_ref, *, s_len, l_ctx, with_ctx):
    def run(chunks):
        for g in range(KA_GROUPS):
            sl = slice(g * 128, (g + 1) * 128)
            lhs = _stack_heads(q_ref[0, :, sl])
            m = jnp.full((2 * TM, 128), NEG_INF, _F32)
            l128 = jnp.zeros((2 * TM, 128), _F32)
            acc = jnp.zeros((2 * TM, 128), _F32)
            for start, size in chunks:
                m, l128, acc = _flash_update(lhs, k_ref[0, start:start + size, :], v_ref[0, start:start + size, :],
                                             m, l128, acc)
            l = jnp.sum(l128, axis=1, keepdims=True)
            o_ref[0, :, sl] = _unstack_heads(acc / l).astype(_BF16)

    ctx_chunk = (s_len, l_ctx)
    chunk = KA_CHUNK_MAX
    while s_len % chunk:
        chunk //= 2
    lat_chunks = [(c * chunk, chunk) for c in range(s_len // chunk)] + [ctx_chunk]
    if with_ctx:
        is_ctx = pl.program_id(1) == s_len // TM
        pl.when(jnp.logical_not(is_ctx))(lambda: run(lat_chunks))
        pl.when(is_ctx)(lambda: run([ctx_chunk]))
    else:
        run(lat_chunks)


def _attn_a_call(q, k, v, s_len, with_ctx):
    b, t, _ = q.shape
    nq = s_len // TM + (1 if with_ctx else 0)
    return pl.pallas_call(
        functools.partial(_attn_a_body, s_len=s_len, l_ctx=t - s_len, with_ctx=with_ctx),
        grid=(b, nq, 4 // KA_GROUPS),
        in_specs=[
            pl.BlockSpec((1, TM, 128 * KA_GROUPS), lambda bi, i, j: (bi, i, j)),
            pl.BlockSpec((1, t, 128), lambda bi, i, j: (bi, 0, 0)),
            pl.BlockSpec((1, t, 128), lambda bi, i, j: (bi, 0, 0)),
        ],
        out_specs=pl.BlockSpec((1, TM, 128 * KA_GROUPS), lambda bi, i, j: (bi, i, j)),
        out_shape=jax.ShapeDtypeStruct((b, nq * TM, BRANCH), _BF16),
        compiler_params=_cparams(("arbitrary", "arbitrary", "arbitrary"), VMEM_LIMIT),
        name="attn_global",
    )(q, k, v)


def _softmax_parts(parts, extra=None):
    m = functools.reduce(jnp.maximum, [jnp.max(s, axis=1, keepdims=True) for s in parts])
    if extra is not None:
        m = jnp.maximum(m, extra)
    ps = [jnp.exp2(s - m) for s in parts]
    l = functools.reduce(lambda a, c: a + c, [jnp.sum(p, axis=1, keepdims=True) for p in ps])
    if extra is not None:
        l = l + jnp.exp2(extra - m)
    return ps, l


def _attn_b_body(sink_ref, q_ref, k_ref, v_ref, o_ref, *, s_len, l_ctx, with_ctx):
    span = TM + 2 * WINDOW
    i = pl.program_id(1)
    is_hi = lax.broadcasted_iota(jnp.int32, (2 * TM, 1), 0) >= TM

    def finish(j, ps, vs, l):
        o = functools.reduce(lambda a, c: a + c, [_dot(p.astype(_BF16), v) for p, v in zip(ps, vs)]) / l
        o_ref[0, :, j * 128:(j + 1) * 128] = _unstack_heads(o).astype(_BF16)

    def latent():
        start = i * TM
        ws = pl.multiple_of(jnp.clip(start - WINDOW, 0, s_len - span), WINDOW)
        kw = k_ref[0, pl.ds(ws, span), :]
        vw = v_ref[0, pl.ds(ws, span), :]
        kc = k_ref[0, s_len:s_len + l_ctx, :]
        vc = v_ref[0, s_len:s_len + l_ctx, :]
        row = lax.broadcasted_iota(jnp.int32, (2 * TM, span), 0)
        qpos = start + jnp.where(row >= TM, row - TM, row)
        kpos = ws + lax.broadcasted_iota(jnp.int32, (2 * TM, span), 1)
        band = jnp.abs(kpos - qpos) <= WINDOW
        scores = []
        for j in range(4):
            lhs = _stack_heads(q_ref[0, :, j * 128:(j + 1) * 128])
            scores.append((_nt(lhs, kw), _nt(lhs, kc)))
        for j in range(4):
            s_loc = jnp.where(band, scores[j][0], NEG_INF)
            sink = jnp.where(is_hi, sink_ref[4 + j], sink_ref[j])
            ps, l = _softmax_parts([s_loc, scores[j][1]], sink)
            finish(j, ps, (vw, vc), l)

    def context():
        kc = k_ref[0, s_len:s_len + l_ctx, :]
        vc = v_ref[0, s_len:s_len + l_ctx, :]
        for j in range(4):
            lhs = _stack_heads(q_ref[0, :, j * 128:(j + 1) * 128])
            sink = jnp.where(is_hi, sink_ref[4 + j], sink_ref[j])
            ps, l = _softmax_parts([_nt(lhs, kc)], sink)
            finish(j, ps, (vc,), l)

    if with_ctx:
        is_ctx = i == s_len // TM
        pl.when(jnp.logical_not(is_ctx))(latent)
        pl.when(is_ctx)(context)
    else:
        latent()


def _attn_b_call(q, k, v, sink, s_len, with_ctx):
    b, t, _ = q.shape
    nq = s_len // TM + (1 if with_ctx else 0)
    return pl.pallas_call(
        functools.partial(_attn_b_body, s_len=s_len, l_ctx=t - s_len, with_ctx=with_ctx),
        grid=(b, nq),
        in_specs=[
            pl.BlockSpec(memory_space=pltpu.SMEM),
            pl.BlockSpec((1, TM, BRANCH), lambda bi, i: (bi, i, 0)),
            pl.BlockSpec((1, t, 128), lambda bi, i: (bi, 0, 0)),
            pl.BlockSpec((1, t, 128), lambda bi, i: (bi, 0, 0)),
        ],
        out_specs=pl.BlockSpec((1, TM, BRANCH), lambda bi, i: (bi, i, 0)),
        out_shape=jax.ShapeDtypeStruct((b, nq * TM, BRANCH), _BF16),
        compiler_params=_cparams(("arbitrary", "arbitrary"), VMEM_LIMIT),
        name="attn_window",
    )(sink, q, k, v)


def _attn_n_body(q_ref, k_ref, v_ref, bias_ref, o_ref, *, s_len, l_ctx, with_ctx):
    rows = s_len // GRID_W
    span = NA_WIN_ROWS * GRID_W
    r = pl.program_id(1)

    def finish(j, ps, vs, l):
        o = functools.reduce(lambda a, c: a + c, [_dot(p.astype(_BF16), v) for p, v in zip(ps, vs)]) / l
        o_ref[0, :, j * 128:(j + 1) * 128] = _unstack_heads(o).astype(_BF16)

    def latent():
        u0 = jnp.clip(4 * r - 4, 0, rows - NA_WIN_ROWS)
        ws = pl.multiple_of(u0 * GRID_W, GRID_W)
        scores = []
        for j in range(4):
            sl = slice(j * 128, (j + 1) * 128)
            lhs = _stack_heads(q_ref[0, :, sl])
            scores.append((_nt(lhs, k_ref[0, pl.ds(ws, span), sl]), _nt(lhs, k_ref[0, s_len:s_len + l_ctx, sl])))
        for j in range(4):
            sl = slice(j * 128, (j + 1) * 128)
            ps, l = _softmax_parts([scores[j][0] + bias_ref[0, j], scores[j][1]])
            finish(j, ps, (v_ref[0, pl.ds(ws, span), sl], v_ref[0, s_len:s_len + l_ctx, sl]), l)

    def context():
        for j in range(4):
            sl = slice(j * 128, (j + 1) * 128)
            lhs = _stack_heads(q_ref[0, :, sl])
            ps, l = _softmax_parts([_nt(lhs, k_ref[0, s_len:s_len + l_ctx, sl])])
            finish(j, ps, (v_ref[0, s_len:s_len + l_ctx, sl],), l)

    if with_ctx:
        is_ctx = r == s_len // TM
        pl.when(jnp.logical_not(is_ctx))(latent)
        pl.when(is_ctx)(context)
    else:
        latent()


def _attn_n_call(q, k, v, bias_tbl, s_len, with_ctx):
    b, t, _ = q.shape
    n_r = s_len // TM
    nq = n_r + (1 if with_ctx else 0)
    span = NA_WIN_ROWS * GRID_W

    def case(ri):
        return jnp.where(ri == 0, 0, jnp.where(ri >= n_r - 1, 2, 1))

    return pl.pallas_call(
        functools.partial(_attn_n_body, s_len=s_len, l_ctx=t - s_len, with_ctx=with_ctx),
        grid=(b, nq),
        in_specs=[
            pl.BlockSpec((1, TM, BRANCH), lambda bi, ri: (bi, ri, 0)),
            pl.BlockSpec((1, t, BRANCH), lambda bi, ri: (bi, 0, 0)),
            pl.BlockSpec((1, t, BRANCH), lambda bi, ri: (bi, 0, 0)),
            pl.BlockSpec((1, 4, 2 * TM, span), lambda bi, ri: (case(ri), 0, 0, 0)),
        ],
        out_specs=pl.BlockSpec((1, TM, BRANCH), lambda bi, ri: (bi, ri, 0)),
        out_shape=jax.ShapeDtypeStruct((b, nq * TM, BRANCH), _BF16),
        compiler_params=_cparams(("arbitrary", "arbitrary"), VMEM_LIMIT),
        name="attn_neighbourhood",
    )(q, k, v, bias_tbl)


def _na_tables(rows):
    a = np.arange(4)[:, None, None, None]
    c = np.arange(GRID_W)[None, :, None, None]
    i = np.arange(NA_WIN_ROWS)[None, None, :, None]
    kc = np.arange(GRID_W)[None, None, None, :]
    c0 = np.clip(c - NA_COLS // 2, 0, GRID_W - NA_COLS)
    col_ok = (kc >= c0) & (kc < c0 + NA_COLS)
    cidx = kc - c + (NA_COLS - 1)
    row_ok = [i < NA_ROWS + 0 * a, (i >= a) & (i < a + NA_ROWS), i >= NA_WIN_ROWS - NA_ROWS + 0 * a]
    ridx = [i - a + 7, i - a + 3, i - a - 1]
    shape = (4, GRID_W, NA_WIN_ROWS, GRID_W)
    sel_r, sel_c, ok = [], [], []
    for cs in range(3):
        valid = np.broadcast_to(row_ok[cs] & col_ok, shape)
        ok.append(valid.reshape(4 * GRID_W, NA_WIN_ROWS * GRID_W))
        rsel = (np.broadcast_to(ridx[cs], (4, 1, NA_WIN_ROWS, 1))[:, 0, :, 0][..., None]
                == np.arange(2 * NA_ROWS - 1)[None, None, :])
        sel_r.append(rsel.astype(np.float32))
    csel = (np.broadcast_to(cidx, (1, GRID_W, 1, GRID_W))[0, :, 0, :][..., None]
            == np.arange(2 * NA_COLS - 1)[None, None, :]).astype(np.float32)
    return np.stack(sel_r), csel, np.stack(ok)


def _na_bias_table(rpb, rows):
    sel_r, sel_c, ok = _na_tables(rows)
    hi = lax.Precision.HIGHEST
    t = jnp.einsum("sair,hrg->shaig", jnp.asarray(sel_r), rpb.astype(_F32), precision=hi)
    t = jnp.einsum("shaig,ckg->shacik", t, jnp.asarray(sel_c), precision=hi)
    t = t.reshape(3, N_HEADS, 4 * GRID_W, NA_WIN_ROWS * GRID_W)
    t = jnp.where(jnp.asarray(ok)[:, None], t * LOG2_E, NEG_INF)
    return t.reshape(3, 4, 2 * TM, NA_WIN_ROWS * GRID_W)


def _route_rows(sel, sc):
    gs, gi1, gi2, gs1, gs2 = [], [], [], [], []
    for g in range(N_GROUPS):
        v = sel[4 * g:4 * g + 4]
        s = sc[4 * g:4 * g + 4]
        m1 = jnp.maximum(jnp.maximum(v[0], v[1]), jnp.maximum(v[2], v[3]))
        i1 = jnp.where(v[0] == m1, 0, jnp.where(v[1] == m1, 1, jnp.where(v[2] == m1, 2, 3)))
        rest = [jnp.where(i1 == k, NEG_INF, v[k]) for k in range(4)]
        m2 = jnp.maximum(jnp.maximum(rest[0], rest[1]), jnp.maximum(rest[2], rest[3]))
        i2 = jnp.where(rest[0] == m2, 0, jnp.where(rest[1] == m2, 1, jnp.where(rest[2] == m2, 2, 3)))

        def pick(idx, s=s):
            return jnp.where(idx == 0, s[0], jnp.where(idx == 1, s[1], jnp.where(idx == 2, s[2], s[3])))

        gs.append(m1 + m2)
        gi1.append(i1 + 4 * g)
        gi2.append(i2 + 4 * g)
        gs1.append(pick(i1))
        gs2.append(pick(i2))
    best = jnp.maximum(jnp.maximum(gs[0], gs[1]), jnp.maximum(gs[2], gs[3]))
    gsel = jnp.where(gs[0] == best, 0, jnp.where(gs[1] == best, 1, jnp.where(gs[2] == best, 2, 3)))

    def pickg(vals):
        return jnp.where(gsel == 0, vals[0], jnp.where(gsel == 1, vals[1], jnp.where(gsel == 2, vals[2], vals[3])))

    return pickg(gi1), pickg(gi2), pickg(gs1), pickg(gs2)


def _merge_body(oa_ref, ob_ref, on_ref, gate_ref, x_ref, xctx_ref, wb_ref, wo_ref, g1_ref, sh2_ref, sc2_ref, n2_ref,
                wrh_ref, wrl_ref, rb_ref,
                x1_ref, h2_ref, idx_ref, rank_ref, wts_ref, cnt_ref, carry_ref, *, n_lat_tiles):
    first = (pl.program_id(0) == 0) & (pl.program_id(1) == 0)
    x_res = jnp.where(pl.program_id(1) < n_lat_tiles, x_ref[0], xctx_ref[0])

    @pl.when(first)
    def _():
        carry_ref[...] = jnp.zeros_like(carry_ref)

    outs = (oa_ref, ob_ref, on_ref)
    y = None
    for n in range(3):
        gt = jax.nn.sigmoid(gate_ref[0, :, n * D_MODEL:(n + 1) * D_MODEL].astype(_F32))
        term = gt * _dot(outs[n][0], wb_ref[n])
        y = term if y is None else y + term
    z = _dot(y.astype(_BF16), wo_ref[...])
    x1 = x_res + g1_ref[0] * z
    x1_ref[0] = x1
    ms = jnp.mean(x1 * x1, axis=-1, keepdims=True)
    h2 = (x1 * lax.rsqrt(ms + EPS) * n2_ref[...]) * (1.0 + sc2_ref[0]) + sh2_ref[0]
    h2_ref[0] = h2

    hb = h2.astype(_BF16)
    hl = (h2 - hb.astype(_F32)).astype(_BF16)
    logits = _nt(wrh_ref[...], hb) + _nt(wrh_ref[...], hl) + _nt(wrl_ref[...], hb)
    scores = jax.nn.sigmoid(logits)
    sel = scores + rb_ref[...]
    sel_rows = [sel[e:e + 1, :] for e in range(N_EXPERTS)]
    sc_rows = [scores[e:e + 1, :] for e in range(N_EXPERTS)]
    e1, e2, s1, s2 = _route_rows(sel_rows, sc_rows)
    den = s1 + s2
    idx_ref[0:1, :] = e1
    idx_ref[1:2, :] = e2
    wts_ref[0:1, :] = s1 / den
    wts_ref[1:2, :] = s2 / den

    eid = lax.broadcasted_iota(jnp.int32, (N_EXPERTS, TM), 0)
    oh1 = eid == e1
    oh2 = eid == e2
    tri = (lax.broadcasted_iota(jnp.int32, (TM, TM), 0) <= lax.broadcasted_iota(jnp.int32, (TM, TM), 1))
    tri = tri.astype(_BF16)
    cum1 = _dot(oh1.astype(_BF16), tri)
    cum2 = _dot(oh2.astype(_BF16), tri)
    tot1 = cum1[:, TM - 1:TM]
    tot2 = cum2[:, TM - 1:TM]
    carry = carry_ref[:, 0:1]
    r1 = jnp.sum(jnp.where(oh1, carry + cum1 - 1.0, 0.0), axis=0, keepdims=True)
    r2 = jnp.sum(jnp.where(oh2, carry + tot1 + cum2 - 1.0, 0.0), axis=0, keepdims=True)
    rank_ref[0:1, :] = r1.astype(jnp.int32)
    rank_ref[1:2, :] = r2.astype(jnp.int32)
    new_carry = carry + tot1 + tot2
    carry_ref[...] = jnp.broadcast_to(new_carry, carry_ref.shape)
    cnt_ref[...] = jnp.broadcast_to(new_carry, cnt_ref.shape).astype(jnp.int32)


def _merge_call(oa, ob, on, gate, x_lat, x_ctx, ctx_tile, wb, wo, g1, sh2, sc2, n2, wrh, wrl, rb, t_out,
                n_lat_tiles):
    b, _, d = x_lat.shape
    nb = g1.shape[0] - 1
    nt = t_out // TM
    n_tok = b * t_out

    def mod_idx(bi, i):
        return (jnp.where(i < n_lat_tiles, bi, nb), 0, 0)

    def tok(width):
        return pl.BlockSpec((1, TM, width), lambda bi, i: (bi, i, 0))

    def full(shape):
        return pl.BlockSpec(shape, lambda bi, i: (0,) * len(shape))

    lane_out = pl.BlockSpec((2, TM), lambda bi, i: (0, bi * nt + i))
    return pl.pallas_call(
        functools.partial(_merge_body, n_lat_tiles=n_lat_tiles),
        grid=(b, nt),
        in_specs=[tok(BRANCH), tok(BRANCH), tok(BRANCH), tok(3 * d), *_token_specs(n_lat_tiles, ctx_tile, d),
                  full((3, BRANCH, d)), full((d, d)),
                  pl.BlockSpec((1, 1, d), mod_idx), pl.BlockSpec((1, 1, d), mod_idx),
                  pl.BlockSpec((1, 1, d), mod_idx), full((1, d)),
                  full((N_EXPERTS, d)), full((N_EXPERTS, d)), full((N_EXPERTS, 1))],
        out_specs=[tok(d), tok(d), lane_out, lane_out, lane_out, full((N_EXPERTS, 128))],
        out_shape=[jax.ShapeDtypeStruct((b, t_out, d), _F32), jax.ShapeDtypeStruct((b, t_out, d), _F32),
                   jax.ShapeDtypeStruct((2, n_tok), jnp.int32), jax.ShapeDtypeStruct((2, n_tok), jnp.int32),
                   jax.ShapeDtypeStruct((2, n_tok), _F32), jax.ShapeDtypeStruct((N_EXPERTS, 128), jnp.int32)],
        scratch_shapes=[pltpu.VMEM((N_EXPERTS, 128), _F32)],
        compiler_params=_cparams(("arbitrary", "arbitrary"), VMEM_LIMIT),
        name="merge_route",
    )(oa, ob, on, gate, x_lat, x_ctx, wb, wo, g1, sh2, sc2, n2, wrh, wrl, rb)


ROW_DMA_UNROLL = 8


def _dispatch_body(pad_start_ref, pad_n_ref, nv_ref, pos_ref, h_ref, xs_ref, zbuf, sem, zsem):
    @pl.when(pl.program_id(0) == 0)
    def _():
        zbuf[...] = jnp.zeros_like(zbuf)

        def zero_row(e, r):
            return pltpu.make_async_copy(zbuf.at[pl.ds(0, 1)], xs_ref.at[pl.ds(pad_start_ref[e] + r, 1)], zsem)

        def zero_tile(i):
            return pltpu.make_async_copy(zbuf, xs_ref.at[pl.ds(pl.multiple_of(i * TM_E, TM_E), TM_E)], zsem)

        n_tiles = xs_ref.shape[0] // TM_E
        for e in range(N_EXPERTS):
            lax.fori_loop(0, pad_n_ref[e], lambda r, c, e=e: (zero_row(e, r).start(), c)[1], 0)
        lax.fori_loop(nv_ref[0], n_tiles, lambda i, c: (zero_tile(i).start(), c)[1], 0)
        for e in range(N_EXPERTS):
            lax.fori_loop(0, pad_n_ref[e], lambda r, c, e=e: (zero_row(e, r).wait(), c)[1], 0)
        lax.fori_loop(nv_ref[0], n_tiles, lambda i, c: (zero_tile(i).wait(), c)[1], 0)

    def row_copy(t, k):
        return pltpu.make_async_copy(h_ref.at[pl.ds(t, 1)], xs_ref.at[pl.ds(pos_ref[0, k, t], 1)], sem)

    def issue(t, carry):
        row_copy(t, 0).start()
        row_copy(t, 1).start()
        return carry

    lax.fori_loop(0, TM, issue, 0, unroll=ROW_DMA_UNROLL)

    def drain(t, carry):
        row_copy(t, 0).wait()
        row_copy(t, 1).wait()
        return carry

    lax.fori_loop(0, TM, drain, 0, unroll=ROW_DMA_UNROLL)


def _dispatch_call(pad_start, pad_n, n_valid, pos3, h2, p_rows):
    n_tiles = pos3.shape[0]
    d = h2.shape[1]
    return pl.pallas_call(
        _dispatch_body,
        grid_spec=pltpu.PrefetchScalarGridSpec(
            num_scalar_prefetch=3,
            grid=(n_tiles,),
            in_specs=[pl.BlockSpec((1, 2, TM), lambda i, ps, pn, nv: (i, 0, 0), memory_space=pltpu.SMEM),
                      pl.BlockSpec((TM, d), lambda i, ps, pn, nv: (i, 0))],
            out_specs=pl.BlockSpec(memory_space=pl.ANY),
            scratch_shapes=[pltpu.VMEM((TM_E, d), h2.dtype), pltpu.SemaphoreType.DMA, pltpu.SemaphoreType.DMA],
        ),
        out_shape=jax.ShapeDtypeStruct((p_rows, d), h2.dtype),
        compiler_params=_cparams(("arbitrary",)),
        name="moe_dispatch",
    )(pad_start, pad_n, n_valid, pos3, h2)


def _moe_body(te_ref, nv_ref, xs_ref, wg_ref, wu_ref, wd_ref, ys_ref):
    del te_ref
    live = pl.program_id(0) < nv_ref[0]

    @pl.when(live)
    def _():
        xb = xs_ref[...].astype(_BF16)
        gate = _dot(xb, wg_ref[0].astype(_BF16))
        up = _dot(xb, wu_ref[0].astype(_BF16))
        act = (gate * jax.nn.sigmoid(gate)) * up
        ys_ref[...] = _dot(act.astype(_BF16), wd_ref[0].astype(_BF16))

    @pl.when(jnp.logical_not(live))
    def _():
        ys_ref[...] = jnp.zeros_like(ys_ref)


def _moe_call(tile_expert, n_valid, xs, wg, wu, wd, layer):
    p_rows, d = xs.shape
    n_tiles = p_rows // TM_E

    def row_idx(i, te, nv):
        return (jnp.minimum(i, nv[0] - 1), 0)

    def out_idx(i, te, nv):
        return (i, 0)

    def w_idx(i, te, nv):
        return (layer, te[i], 0, 0)

    return pl.pallas_call(
        _moe_body,
        grid_spec=pltpu.PrefetchScalarGridSpec(
            num_scalar_prefetch=2,
            grid=(n_tiles,),
            in_specs=[pl.BlockSpec((TM_E, d), row_idx),
                      pl.BlockSpec((None, 1, d, D_FF), w_idx),
                      pl.BlockSpec((None, 1, d, D_FF), w_idx),
                      pl.BlockSpec((None, 1, D_FF, d), w_idx)],
            out_specs=pl.BlockSpec((TM_E, d), out_idx),
        ),
        out_shape=jax.ShapeDtypeStruct((p_rows, d), _F32),
        compiler_params=_cparams(("arbitrary",), VMEM_LIMIT),
        name="moe_experts",
    )(tile_expert, n_valid, xs, wg, wu, wd)


def _combine_body(pos0_ref, posn_ref, w_ref, x1_ref, g2_ref, fn_ref, ys_ref, o_ref, gbuf, sems):
    step = pl.program_id(0)
    y, issue_next, finish = _moe_rows_pipeline(step, pl.num_programs(0) - 1, pos0_ref, posn_ref, w_ref, ys_ref,
                                               gbuf, sems)
    x2 = x1_ref[...] + g2_ref[0] * y
    issue_next()
    ms = jnp.mean(x2 * x2, axis=-1, keepdims=True)
    o_ref[...] = x2 * lax.rsqrt(ms + EPS) * fn_ref[...]
    finish()


def _combine_call(pos3, wts, x1, g2, fnorm, ys, t_out, n_lat_tiles):
    n_tok, d = x1.shape
    nb = g2.shape[0] - 1
    nt = t_out // TM
    n_tiles = n_tok // TM

    def mod_idx(i):
        return (jnp.where(i % nt < n_lat_tiles, i // nt, nb), 0, 0)

    smem_tile = functools.partial(pl.BlockSpec, (1, 2, TM), memory_space=pltpu.SMEM)
    return pl.pallas_call(
        _combine_body,
        grid=(n_tiles,),
        in_specs=[smem_tile(lambda i: (0, 0, 0)),
                  smem_tile(lambda i: (jnp.minimum(i + 1, n_tiles - 1), 0, 0)),
                  pl.BlockSpec((2, TM), lambda i: (0, i)),
                  pl.BlockSpec((TM, d), lambda i: (i, 0)),
                  pl.BlockSpec((1, 1, d), mod_idx),
                  pl.BlockSpec((1, d), lambda i: (0, 0)),
                  pl.BlockSpec(memory_space=pl.ANY)],
        out_specs=pl.BlockSpec((TM, d), lambda i: (i, 0)),
        out_shape=jax.ShapeDtypeStruct((n_tok, d), _F32),
        scratch_shapes=[pltpu.VMEM((2, 2, TM, d), _F32), pltpu.SemaphoreType.DMA((2,))],
        compiler_params=_cparams(("arbitrary",)),
        name="moe_combine",
    )(pos3, pos3, wts, x1, g2, fnorm, ys)


def _rope_tables(s_len, l_ctx):
    t = jnp.arange(s_len, dtype=jnp.int32)
    row = (t // GRID_W).astype(_F32)
    col = (t % GRID_W).astype(_F32)
    half = HEAD_DIM // 2
    inv = ROPE_THETA ** (-jnp.arange(0, half, 2, dtype=_F32) / half)
    ar = row[:, None] * inv
    ac = col[:, None] * inv
    cos64 = jnp.concatenate([jnp.cos(ar), jnp.cos(ar), jnp.cos(ac), jnp.cos(ac)], axis=1)
    sin64 = jnp.concatenate([-jnp.sin(ar), jnp.sin(ar), -jnp.sin(ac), jnp.sin(ac)], axis=1)
    cos_t = jnp.concatenate([jnp.tile(cos64, (1, 2)), jnp.ones((l_ctx, 128), _F32)], axis=0)
    sin_t = jnp.concatenate([jnp.tile(sin64, (1, 2)), jnp.zeros((l_ctx, 128), _F32)], axis=0)
    return cos_t, sin_t


def _block_diag_ones():
    r = np.arange(BRANCH)
    return jnp.asarray((r[:, None] // HEAD_DIM) == (r[None, :] // HEAD_DIM), dtype=_BF16)


def kernel(x, c, ctx, c_ctx, w_mod, b_mod, norm1, norm2, w_in, q_norm_a, k_norm_a, sink_b, rpb_n, w_branch, w_out,
           w_router, router_bias, w_e_gate, w_e_up, w_e_down, final_norm):
    b, s_len, d = x.shape
    l_ctx = ctx.shape[1]
    depth = w_in.shape[0]
    t_all = s_len + l_ctx
    rows = s_len // GRID_W
    assert d == D_MODEL and l_ctx == TM and s_len % TM == 0 and b + 1 <= 16
    assert s_len >= TM + 2 * WINDOW and rows >= NA_WIN_ROWS and rows % 4 == 0
    n_lat_tiles = s_len // TM

    cos_t, sin_t = _rope_tables(s_len, l_ctx)
    bd = _block_diag_ones()
    cvec = jnp.zeros((16, d), _F32).at[:b].set(c).at[b].set(c_ctx)
    mods = _mod_call(cvec, w_mod, b_mod)[:, :b + 1].reshape(depth, b + 1, 6, 1, d)

    wr = w_router.T.astype(_F32)
    wrh = wr.astype(_BF16)
    wrl = (wr - wrh.astype(_F32)).astype(_BF16)
    rb = router_bias.astype(_F32).reshape(N_EXPERTS, 1)

    x_lat, x_ctx, ctx_tile = x, ctx, 0
    pending = None
    for l in range(depth):
        last = l == depth - 1
        sh1, sc1, g1, sh2, sc2, g2 = [mods[l, :, n] for n in range(6)]
        w_f = w_in[l]
        w_l = jnp.concatenate([_pair_heads(w_f[:, OFF_QA:OFF_KA], 1), w_f[:, OFF_KA:OFF_QB],
                               _pair_heads(w_f[:, OFF_QB:OFF_KB], 1), w_f[:, OFF_KB:]], axis=1).astype(_BF16)
        gq = (jnp.tile(q_norm_a[l].astype(_F32), N_HEADS) * Q_SCALE).reshape(1, BRANCH)
        gk = jnp.tile(k_norm_a[l].astype(_F32), 2).reshape(1, KV_A)
        outs = _inproj_call(x_lat, x_ctx, ctx_tile, n_lat_tiles, norm1[l].reshape(1, d), sh1, sc1, w_l, cos_t, sin_t,
                            gq, gk, bd, combine=pending)
        qa, ka, va, qb, kb, vb, qn, kn, vn, gate = outs[:10]
        if pending is not None:
            x_lat = x_ctx = outs[10]

        sink = sink_b[l].astype(_F32) * LOG2_E
        oa = _attn_a_call(qa, ka, va, s_len, not last)
        ob = _attn_b_call(qb, kb, vb, sink, s_len, not last)
        on = _attn_n_call(qn, kn, vn, _na_bias_table(rpb_n[l], rows), s_len, not last)

        t_out = s_len if last else t_all
        wb = jnp.stack([_pair_heads(w_branch[l, 0], 0), _pair_heads(w_branch[l, 1], 0),
                        w_branch[l, 2]]).astype(_BF16)
        x1, h2, idx, rank, wts, cnt = _merge_call(
            oa, ob, on, gate, x_lat, x_ctx, ctx_tile, wb, w_out[l].astype(_BF16), g1, sh2, sc2,
            norm2[l].reshape(1, d), wrh, wrl, rb, t_out, n_lat_tiles)

        n_tok = b * t_out
        counts = cnt[:, 0]
        padded = ((counts + TM_E - 1) // TM_E) * TM_E
        ends = jnp.cumsum(padded)
        offs = ends - padded
        pos = rank + jnp.sum(jnp.where(idx[None] == jnp.arange(N_EXPERTS)[:, None, None], offs[:, None, None], 0), axis=0)
        pos3 = pos.reshape(2, n_tok // TM, TM).transpose(1, 0, 2)
        p_rows = 2 * n_tok + N_EXPERTS * TM_E
        n_tiles = p_rows // TM_E
        n_valid = (ends[-1] // TM_E).astype(jnp.int32).reshape(1)
        tile_start = jnp.arange(n_tiles, dtype=jnp.int32) * TM_E
        tile_expert = jnp.minimum(jnp.sum(tile_start[:, None] >= ends[None, :], axis=1), N_EXPERTS - 1).astype(jnp.int32)

        pad_start = (offs + counts).astype(jnp.int32)
        pad_n = (padded - counts).astype(jnp.int32)
        xs = _dispatch_call(pad_start, pad_n, n_valid, pos3, h2.reshape(n_tok, d), p_rows)
        ys = _moe_call(tile_expert, n_valid, xs, w_e_gate, w_e_up, w_e_down, l)
        if last:
            out = _combine_call(pos3, wts, x1.reshape(n_tok, d), g2, final_norm.reshape(1, d), ys, t_out,
                                n_lat_tiles)
            return out.reshape(b, t_out, d)
        pending = (pos3, wts, g2, ys)
        x_lat = x_ctx = x1
        ctx_tile = n_lat_tiles
```

```python
import functools

import numpy as np
import jax
import jax.numpy as jnp
from jax import lax
from jax.experimental import pallas as pl
from jax.experimental.pallas import tpu as pltpu

D_MODEL = 1024
HEAD_DIM = 64
GRID_W = 64
N_HEADS = 8
WINDOW = 128
NA_ROWS = 8
NA_COLS = 16
ROPE_THETA = 10000.0
N_EXPERTS = 16
N_GROUPS = 4
EXPERTS_PER_GROUP = N_EXPERTS // N_GROUPS
D_FF = 512
EPS = 1e-6
NEG_INF = -1e30
ATTN_SCALE = HEAD_DIM ** -0.5
LOG2_E = 1.4426950408889634
Q_SCALE = ATTN_SCALE * LOG2_E
BRANCH = N_HEADS * HEAD_DIM
KV_A = 2 * HEAD_DIM
IN_COLS = 6144

OFF_QA, OFF_KA, OFF_VA = 0, 512, 640
OFF_QB, OFF_KB, OFF_VB = 768, 1280, 1408
OFF_QN, OFF_KN, OFF_VN = 1536, 2048, 2560
OFF_GATE = 3072

TM = 256
TM_E = 512
N_BUCKETS = N_GROUPS * 6
BUCKET_ROWS = 32
ROW_W = D_MODEL + 128
_PAIR_LO = np.array([0, 0, 0, 1, 1, 2])
_PAIR_HI = np.array([1, 2, 3, 2, 3, 3])
NA_WIN_ROWS = 12
VMEM_LIMIT = 56 * 1024 * 1024

_F32 = jnp.float32
_BF16 = jnp.bfloat16

def _pair_heads(w, axis):
    shape = w.shape
    split = shape[:axis] + (2, 4, HEAD_DIM) + shape[axis + 1:]
    return jnp.swapaxes(w.reshape(split), axis, axis + 1).reshape(shape)


def _nt(a, b):
    return lax.dot_general(a, b, (((1,), (1,)), ((), ())), preferred_element_type=_F32)


def _dot(a, b):
    return jnp.dot(a, b, preferred_element_type=_F32)


def _cparams(sem, vmem=None):
    return pltpu.CompilerParams(dimension_semantics=sem, vmem_limit_bytes=vmem)


def _mod_body(c_ref, w_ref, b_ref, o_ref):
    cv = c_ref[...]
    a = (cv * jax.nn.sigmoid(cv)).astype(_BF16)
    o_ref[0] = _dot(a, w_ref[0].astype(_BF16)) + b_ref[0]


def _mod_call(cvec, w_mod, b_mod):
    depth, d, n = w_mod.shape
    tn = 1536
    return pl.pallas_call(
        _mod_body,
        grid=(depth, n // tn),
        in_specs=[
            pl.BlockSpec((16, d), lambda l, j: (0, 0)),
            pl.BlockSpec((1, d, tn), lambda l, j: (l, 0, j)),
            pl.BlockSpec((1, 1, tn), lambda l, j: (l, 0, j)),
        ],
        out_specs=pl.BlockSpec((1, 16, tn), lambda l, j: (l, 0, j)),
        out_shape=jax.ShapeDtypeStruct((depth, 16, n), _F32),
        compiler_params=_cparams(("arbitrary", "arbitrary")),
        name="mod_vectors",
    )(cvec, w_mod, b_mod.reshape(depth, 1, n))


def _rope(v, c, s):
    w = v.shape[1]
    lane = lax.broadcasted_iota(jnp.int32, v.shape, 1)
    first = (lane & 16) == 0
    sw = jnp.where(first, pltpu.roll(v, w - 16, 1), pltpu.roll(v, 16, 1))
    return v * c + sw * s


def _inproj_body(x_ref, xctx_ref, *refs, n_lat_tiles):
    x = jnp.where(pl.program_id(1) < n_lat_tiles, x_ref[0], xctx_ref[0])
    _inproj_core(x, *refs)


def _col_from_row(row):
    blk = jnp.broadcast_to(row, (128, 128))
    return blk.T[:, 0:1]


def _moe_rows_pipeline(step, last_step, pos0_ref, posn_ref, ys_ref, gbuf, sems):
    def row_copy(pos_ref, slot, t):
        return pltpu.make_async_copy(ys_ref.at[pl.ds(pos_ref[0, 0, t], 1)], gbuf.at[slot, pl.ds(t, 1)],
                                     sems.at[slot])

    def gather(pos_ref, slot):
        for t in range(TM):
            row_copy(pos_ref, slot, t).start()

    def drain(slot):
        for t in range(TM):
            row_copy(pos0_ref, slot, t).wait()

    pl.when(step == 0)(lambda: gather(pos0_ref, 0))
    slot = step % 2
    drain(slot)
    y = gbuf[slot]
    return (y, lambda: gather(posn_ref, (step + 1) % 2),
            lambda: pl.when(step == last_step)(lambda: drain((step + 1) % 2)))


def _inproj_combine_body(pos0_ref, posn_ref, x_ref, xctx_ref, g2_ref, ys_ref, *refs, n_lat_tiles):
    core_refs, x2_ref, gbuf, sems = refs[:-3], refs[-3], refs[-2], refs[-1]
    nt = pl.num_programs(1)
    step = pl.program_id(0) * nt + pl.program_id(1)
    y, issue_next, finish = _moe_rows_pipeline(step, pl.num_programs(0) * nt - 1, pos0_ref, posn_ref,
                                               ys_ref, gbuf, sems)
    x1 = jnp.where(pl.program_id(1) < n_lat_tiles, x_ref[0], xctx_ref[0])
    x = x1 + g2_ref[0] * y
    x2_ref[0] = x
    issue_next()
    _inproj_core(x, *core_refs)
    finish()


def _inproj_core(x, g_ref, sh_ref, sc_ref, w_ref, cos_ref, sin_ref, gq_ref, gk_ref, bd_ref,
                 qa_ref, ka_ref, va_ref, qb_ref, kb_ref, vb_ref, qn_ref, kn_ref, vn_ref, gate_ref):
    ms = jnp.mean(x * x, axis=-1, keepdims=True)
    h = (x * lax.rsqrt(ms + EPS) * g_ref[...]) * (1.0 + sc_ref[0]) + sh_ref[0]
    hb = h.astype(_BF16)

    def seg(off, width):
        return _dot(hb, w_ref[:, off:off + width])

    c1 = cos_ref[...]
    s1 = sin_ref[...]
    c4 = jnp.concatenate([c1] * 4, axis=1)
    s4 = jnp.concatenate([s1] * 4, axis=1)
    bd = bd_ref[...]

    qa = seg(OFF_QA, BRANCH)
    ssq = _dot((qa * qa).astype(_BF16), bd)
    qa = qa * lax.rsqrt(ssq * (1.0 / HEAD_DIM) + EPS) * gq_ref[...]
    qa_ref[0] = _rope(qa, c4, s4).astype(_BF16)
    ka = seg(OFF_KA, KV_A)
    ssk = _dot((ka * ka).astype(_BF16), bd_ref[:KV_A, :KV_A])
    ka = ka * lax.rsqrt(ssk * (1.0 / HEAD_DIM) + EPS) * gk_ref[...]
    ka_ref[0] = _rope(ka, c1, s1).astype(_BF16)
    va_ref[0] = seg(OFF_VA, KV_A).astype(_BF16)
    qb_ref[0] = _rope(seg(OFF_QB, BRANCH) * Q_SCALE, c4, s4).astype(_BF16)
    kb_ref[0] = _rope(seg(OFF_KB, KV_A), c1, s1).astype(_BF16)
    vb_ref[0] = seg(OFF_VB, KV_A).astype(_BF16)
    qn_ref[0] = (seg(OFF_QN, BRANCH) * Q_SCALE).astype(_BF16)
    kn_ref[0] = seg(OFF_KN, BRANCH).astype(_BF16)
    vn_ref[0] = seg(OFF_VN, BRANCH).astype(_BF16)
    for n in range(3):
        gate_ref[0, :, n * D_MODEL:(n + 1) * D_MODEL] = seg(OFF_GATE + n * D_MODEL, D_MODEL).astype(_BF16)


def _token_specs(n_lat_tiles, ctx_tile, d):
    lat = pl.BlockSpec((1, TM, d), lambda bi, i: (bi, jnp.minimum(i, n_lat_tiles - 1), 0))
    ctx = pl.BlockSpec((1, TM, d), lambda bi, i: (bi, ctx_tile, 0))
    return lat, ctx


def _inproj_call(x_lat, x_ctx, ctx_tile, n_lat_tiles, g1n, sh, sc, w, cos_t, sin_t, gq, gk, bd, combine=None):
    b, _, d = x_lat.shape
    nb = sh.shape[0] - 1
    nt = n_lat_tiles + 1
    t = nt * TM

    def mod_idx(bi, i):
        return (jnp.where(i < n_lat_tiles, bi, nb), 0, 0)

    def tok(width):
        return pl.BlockSpec((1, TM, width), lambda bi, i: (bi, i, 0))

    widths = [BRANCH, KV_A, KV_A, BRANCH, KV_A, KV_A, BRANCH, BRANCH, BRANCH, 3 * D_MODEL]
    core_specs = [
        pl.BlockSpec((1, d), lambda bi, i: (0, 0)),
        pl.BlockSpec((1, 1, d), mod_idx),
        pl.BlockSpec((1, 1, d), mod_idx),
        pl.BlockSpec((d, IN_COLS), lambda bi, i: (0, 0)),
        pl.BlockSpec((TM, 128), lambda bi, i: (i, 0)),
        pl.BlockSpec((TM, 128), lambda bi, i: (i, 0)),
        pl.BlockSpec((1, BRANCH), lambda bi, i: (0, 0)),
        pl.BlockSpec((1, KV_A), lambda bi, i: (0, 0)),
        pl.BlockSpec((BRANCH, BRANCH), lambda bi, i: (0, 0)),
    ]
    core_args = (g1n, sh, sc, w, cos_t, sin_t, gq, gk, bd)
    out_specs = [tok(wd) for wd in widths]
    out_shape = [jax.ShapeDtypeStruct((b, t, wd), _BF16) for wd in widths]
    if combine is None:
        return pl.pallas_call(
            functools.partial(_inproj_body, n_lat_tiles=n_lat_tiles),
            grid=(b, nt),
            in_specs=[*_token_specs(n_lat_tiles, ctx_tile, d), *core_specs],
            out_specs=out_specs,
            out_shape=out_shape,
            compiler_params=_cparams(("arbitrary", "arbitrary"), VMEM_LIMIT),
            name="inproj",
        )(x_lat, x_ctx, *core_args)

    pos3, g2, ys = combine
    last_tile = b * nt - 1
    smem_tile = functools.partial(pl.BlockSpec, (1, 1, TM), memory_space=pltpu.SMEM)
    return pl.pallas_call(
        functools.partial(_inproj_combine_body, n_lat_tiles=n_lat_tiles),
        grid=(b, nt),
        in_specs=[
            smem_tile(lambda bi, i: (0, 0, 0)),
            smem_tile(lambda bi, i: (jnp.minimum(bi * nt + i + 1, last_tile), 0, 0)),
            *_token_specs(n_lat_tiles, ctx_tile, d),
            pl.BlockSpec((1, 1, d), mod_idx),
            pl.BlockSpec(memory_space=pl.ANY),
            *core_specs,
        ],
        out_specs=out_specs + [tok(d)],
        out_shape=out_shape + [jax.ShapeDtypeStruct((b, t, d), _F32)],
        scratch_shapes=[pltpu.VMEM((2, TM, d), _F32), pltpu.SemaphoreType.DMA((2,))],
        compiler_params=_cparams(("arbitrary", "arbitrary"), VMEM_LIMIT),
        name="inproj_combine",
    )(pos3, pos3, x_lat, x_ctx, g2, ys, *core_args)


def _stack_heads(q):
    lane = lax.broadcasted_iota(jnp.int32, q.shape, 1)
    lo = lane < HEAD_DIM
    zero = jnp.zeros_like(q)
    return jnp.concatenate([jnp.where(lo, q, zero), jnp.where(lo, zero, q)], axis=0)


def _unstack_heads(o):
    half = o.shape[0] // 2
    lane = lax.broadcasted_iota(jnp.int32, (half, o.shape[1]), 1)
    return jnp.where(lane < HEAD_DIM, o[:half], o[half:])


KA_CHUNK_MAX = 2048
KA_GROUPS = 4


def _flash_update(lhs, kc, vc, m, l128, acc):
    s = _nt(lhs, kc)
    tiles = [s[:, t * 128:(t + 1) * 128] for t in range(s.shape[1] // 128)]
    m_new = jnp.maximum(m, jnp.max(functools.reduce(jnp.maximum, tiles), axis=1, keepdims=True))
    alpha = jnp.exp2(m - m_new)
    ps = [jnp.exp2(t - m_new) for t in tiles]
    l128 = alpha * l128 + functools.reduce(lambda a, c: a + c, ps)
    p = jnp.concatenate([x.astype(_BF16) for x in ps], axis=1)
    acc = alpha * acc + _dot(p, vc)
    return m_new, l128, acc


def _attn_a_body(q_ref, k_ref, v_ref, o_ref, *, s_len, l_ctx, with_ctx):
    def run(chunks):
        for g in range(KA_GROUPS):
            sl = slice(g * 128, (g + 1) * 128)
            lhs = _stack_heads(q_ref[0, :, sl])
            m = jnp.full((2 * TM, 128), NEG_INF, _F32)
            l128 = jnp.zeros((2 * TM, 128), _F32)
            acc = jnp.zeros((2 * TM, 128), _F32)
            for start, size in chunks:
                m, l128, acc = _flash_update(lhs, k_ref[0, start:start + size, :], v_ref[0, start:start + size, :],
                                             m, l128, acc)
            l = jnp.sum(l128, axis=1, keepdims=True)
            o_ref[0, :, sl] = _unstack_heads(acc / l).astype(_BF16)

    ctx_chunk = (s_len, l_ctx)
    chunk = KA_CHUNK_MAX
    while s_len % chunk:
        chunk //= 2
    lat_chunks = [(c * chunk, chunk) for c in range(s_len // chunk)] + [ctx_chunk]
    if with_ctx:
        is_ctx = pl.program_id(1) == s_len // TM
        pl.when(jnp.logical_not(is_ctx))(lambda: run(lat_chunks))
        pl.when(is_ctx)(lambda: run([ctx_chunk]))
    else:
        run(lat_chunks)


def _attn_a_call(q, k, v, s_len, with_ctx):
    b, t, _ = q.shape
    nq = s_len // TM + (1 if with_ctx else 0)
    return pl.pallas_call(
        functools.partial(_attn_a_body, s_len=s_len, l_ctx=t - s_len, with_ctx=with_ctx),
        grid=(b, nq, 4 // KA_GROUPS),
        in_specs=[
            pl.BlockSpec((1, TM, 128 * KA_GROUPS), lambda bi, i, j: (bi, i, j)),
            pl.BlockSpec((1, t, 128), lambda bi, i, j: (bi, 0, 0)),
            pl.BlockSpec((1, t, 128), lambda bi, i, j: (bi, 0, 0)),
        ],
        out_specs=pl.BlockSpec((1, TM, 128 * KA_GROUPS), lambda bi, i, j: (bi, i, j)),
        out_shape=jax.ShapeDtypeStruct((b, nq * TM, BRANCH), _BF16),
        compiler_params=_cparams(("arbitrary", "arbitrary", "arbitrary"), VMEM_LIMIT),
        name="attn_global",
    )(q, k, v)


def _softmax_parts(parts, extra=None):
    m = functools.reduce(jnp.maximum, [jnp.max(s, axis=1, keepdims=True) for s in parts])
    if extra is not None:
        m = jnp.maximum(m, extra)
    ps = [jnp.exp2(s - m) for s in parts]
    l = functools.reduce(lambda a, c: a + c, [jnp.sum(p, axis=1, keepdims=True) for p in ps])
    if extra is not None:
        l = l + jnp.exp2(extra - m)
    return ps, l


def _attn_b_body(sink_ref, q_ref, k_ref, v_ref, o_ref, *, s_len, l_ctx, with_ctx):
    span = TM + 2 * WINDOW
    i = pl.program_id(1)
    is_hi = lax.broadcasted_iota(jnp.int32, (2 * TM, 1), 0) >= TM

    def finish(j, ps, vs, l):
        o = functools.reduce(lambda a, c: a + c, [_dot(p.astype(_BF16), v) for p, v in zip(ps, vs)]) / l
        o_ref[0, :, j * 128:(j + 1) * 128] = _unstack_heads(o).astype(_BF16)

    def latent():
        start = i * TM
        ws = pl.multiple_of(jnp.clip(start - WINDOW, 0, s_len - span), WINDOW)
        kw = k_ref[0, pl.ds(ws, span), :]
        vw = v_ref[0, pl.ds(ws, span), :]
        kc = k_ref[0, s_len:s_len + l_ctx, :]
        vc = v_ref[0, s_len:s_len + l_ctx, :]
        row = lax.broadcasted_iota(jnp.int32, (2 * TM, span), 0)
        qpos = start + jnp.where(row >= TM, row - TM, row)
        kpos = ws + lax.broadcasted_iota(jnp.int32, (2 * TM, span), 1)
        band = jnp.abs(kpos - qpos) <= WINDOW
        scores = []
        for j in range(4):
            lhs = _stack_heads(q_ref[0, :, j * 128:(j + 1) * 128])
            scores.append((_nt(lhs, kw), _nt(lhs, kc)))
        for j in range(4):
            s_loc = jnp.where(band, scores[j][0], NEG_INF)
            sink = jnp.where(is_hi, sink_ref[4 + j], sink_ref[j])
            ps, l = _softmax_parts([s_loc, scores[j][1]], sink)
            finish(j, ps, (vw, vc), l)

    def context():
        kc = k_ref[0, s_len:s_len + l_ctx, :]
        vc = v_ref[0, s_len:s_len + l_ctx, :]
        for j in range(4):
            lhs = _stack_heads(q_ref[0, :, j * 128:(j + 1) * 128])
            sink = jnp.where(is_hi, sink_ref[4 + j], sink_ref[j])
            ps, l = _softmax_parts([_nt(lhs, kc)], sink)
            finish(j, ps, (vc,), l)

    if with_ctx:
        is_ctx = i == s_len // TM
        pl.when(jnp.logical_not(is_ctx))(latent)
        pl.when(is_ctx)(context)
    else:
        latent()


def _attn_b_call(q, k, v, sink, s_len, with_ctx):
    b, t, _ = q.shape
    nq = s_len // TM + (1 if with_ctx else 0)
    return pl.pallas_call(
        functools.partial(_attn_b_body, s_len=s_len, l_ctx=t - s_len, with_ctx=with_ctx),
        grid=(b, nq),
        in_specs=[
            pl.BlockSpec(memory_space=pltpu.SMEM),
            pl.BlockSpec((1, TM, BRANCH), lambda bi, i: (bi, i, 0)),
            pl.BlockSpec((1, t, 128), lambda bi, i: (bi, 0, 0)),
            pl.BlockSpec((1, t, 128), lambda bi, i: (bi, 0, 0)),
        ],
        out_specs=pl.BlockSpec((1, TM, BRANCH), lambda bi, i: (bi, i, 0)),
        out_shape=jax.ShapeDtypeStruct((b, nq * TM, BRANCH), _BF16),
        compiler_params=_cparams(("arbitrary", "arbitrary"), VMEM_LIMIT),
        name="attn_window",
    )(sink, q, k, v)


def _attn_n_body(q_ref, k_ref, v_ref, bias_ref, o_ref, *, s_len, l_ctx, with_ctx):
    rows = s_len // GRID_W
    span = NA_WIN_ROWS * GRID_W
    r = pl.program_id(1)

    def finish(j, ps, vs, l):
        o = functools.reduce(lambda a, c: a + c, [_dot(p.astype(_BF16), v) for p, v in zip(ps, vs)]) / l
        o_ref[0, :, j * 128:(j + 1) * 128] = _unstack_heads(o).astype(_BF16)

    def latent():
        u0 = jnp.clip(4 * r - 4, 0, rows - NA_WIN_ROWS)
        ws = pl.multiple_of(u0 * GRID_W, GRID_W)
        scores = []
        for j in range(4):
            sl = slice(j * 128, (j + 1) * 128)
            lhs = _stack_heads(q_ref[0, :, sl])
            scores.append((_nt(lhs, k_ref[0, pl.ds(ws, span), sl]), _nt(lhs, k_ref[0, s_len:s_len + l_ctx, sl])))
        for j in range(4):
            sl = slice(j * 128, (j + 1) * 128)
            ps, l = _softmax_parts([scores[j][0] + bias_ref[0, j], scores[j][1]])
            finish(j, ps, (v_ref[0, pl.ds(ws, span), sl], v_ref[0, s_len:s_len + l_ctx, sl]), l)

    def context():
        for j in range(4):
            sl = slice(j * 128, (j + 1) * 128)
            lhs = _stack_heads(q_ref[0, :, sl])
            ps, l = _softmax_parts([_nt(lhs, k_ref[0, s_len:s_len + l_ctx, sl])])
            finish(j, ps, (v_ref[0, s_len:s_len + l_ctx, sl],), l)

    if with_ctx:
        is_ctx = r == s_len // TM
        pl.when(jnp.logical_not(is_ctx))(latent)
        pl.when(is_ctx)(context)
    else:
        latent()


def _attn_n_call(q, k, v, bias_tbl, s_len, with_ctx):
    b, t, _ = q.shape
    n_r = s_len // TM
    nq = n_r + (1 if with_ctx else 0)
    span = NA_WIN_ROWS * GRID_W

    def case(ri):
        return jnp.where(ri == 0, 0, jnp.where(ri >= n_r - 1, 2, 1))

    return pl.pallas_call(
        functools.partial(_attn_n_body, s_len=s_len, l_ctx=t - s_len, with_ctx=with_ctx),
        grid=(b, nq),
        in_specs=[
            pl.BlockSpec((1, TM, BRANCH), lambda bi, ri: (bi, ri, 0)),
            pl.BlockSpec((1, t, BRANCH), lambda bi, ri: (bi, 0, 0)),
            pl.BlockSpec((1, t, BRANCH), lambda bi, ri: (bi, 0, 0)),
            pl.BlockSpec((1, 4, 2 * TM, span), lambda bi, ri: (case(ri), 0, 0, 0)),
        ],
        out_specs=pl.BlockSpec((1, TM, BRANCH), lambda bi, ri: (bi, ri, 0)),
        out_shape=jax.ShapeDtypeStruct((b, nq * TM, BRANCH), _BF16),
        compiler_params=_cparams(("arbitrary", "arbitrary"), VMEM_LIMIT),
        name="attn_neighbourhood",
    )(q, k, v, bias_tbl)


def _na_tables(rows):
    a = np.arange(4)[:, None, None, None]
    c = np.arange(GRID_W)[None, :, None, None]
    i = np.arange(NA_WIN_ROWS)[None, None, :, None]
    kc = np.arange(GRID_W)[None, None, None, :]
    c0 = np.clip(c - NA_COLS // 2, 0, GRID_W - NA_COLS)
    col_ok = (kc >= c0) & (kc < c0 + NA_COLS)
    cidx = kc - c + (NA_COLS - 1)
    row_ok = [i < NA_ROWS + 0 * a, (i >= a) & (i < a + NA_ROWS), i >= NA_WIN_ROWS - NA_ROWS + 0 * a]
    ridx = [i - a + 7, i - a + 3, i - a - 1]
    shape = (4, GRID_W, NA_WIN_ROWS, GRID_W)
    sel_r, sel_c, ok = [], [], []
    for cs in range(3):
        valid = np.broadcast_to(row_ok[cs] & col_ok, shape)
        ok.append(valid.reshape(4 * GRID_W, NA_WIN_ROWS * GRID_W))
        rsel = (np.broadcast_to(ridx[cs], (4, 1, NA_WIN_ROWS, 1))[:, 0, :, 0][..., None]
                == np.arange(2 * NA_ROWS - 1)[None, None, :])
        sel_r.append(rsel.astype(np.float32))
    csel = (np.broadcast_to(cidx, (1, GRID_W, 1, GRID_W))[0, :, 0, :][..., None]
            == np.arange(2 * NA_COLS - 1)[None, None, :]).astype(np.float32)
    return np.stack(sel_r), csel, np.stack(ok)


def _na_bias_table(rpb, rows):
    sel_r, sel_c, ok = _na_tables(rows)
    hi = lax.Precision.HIGHEST
    t = jnp.einsum("sair,hrg->shaig", jnp.asarray(sel_r), rpb.astype(_F32), precision=hi)
    t = jnp.einsum("shaig,ckg->shacik", t, jnp.asarray(sel_c), precision=hi)
    t = t.reshape(3, N_HEADS, 4 * GRID_W, NA_WIN_ROWS * GRID_W)
    t = jnp.where(jnp.asarray(ok)[:, None], t * LOG2_E, NEG_INF)
    return t.reshape(3, 4, 2 * TM, NA_WIN_ROWS * GRID_W)


def _route_rows(sel, sc):
    gs, gi1, gi2, gs1, gs2 = [], [], [], [], []
    for g in range(N_GROUPS):
        v = sel[4 * g:4 * g + 4]
        s = sc[4 * g:4 * g + 4]
        m1 = jnp.maximum(jnp.maximum(v[0], v[1]), jnp.maximum(v[2], v[3]))
        i1 = jnp.where(v[0] == m1, 0, jnp.where(v[1] == m1, 1, jnp.where(v[2] == m1, 2, 3)))
        rest = [jnp.where(i1 == k, NEG_INF, v[k]) for k in range(4)]
        m2 = jnp.maximum(jnp.maximum(rest[0], rest[1]), jnp.maximum(rest[2], rest[3]))
        i2 = jnp.where(rest[0] == m2, 0, jnp.where(rest[1] == m2, 1, jnp.where(rest[2] == m2, 2, 3)))

        def pick(idx, s=s):
            return jnp.where(idx == 0, s[0], jnp.where(idx == 1, s[1], jnp.where(idx == 2, s[2], s[3])))

        gs.append(m1 + m2)
        gi1.append(i1 + 4 * g)
        gi2.append(i2 + 4 * g)
        gs1.append(pick(i1))
        gs2.append(pick(i2))
    best = jnp.maximum(jnp.maximum(gs[0], gs[1]), jnp.maximum(gs[2], gs[3]))
    gsel = jnp.where(gs[0] == best, 0, jnp.where(gs[1] == best, 1, jnp.where(gs[2] == best, 2, 3)))

    def pickg(vals):
        return jnp.where(gsel == 0, vals[0], jnp.where(gsel == 1, vals[1], jnp.where(gsel == 2, vals[2], vals[3])))

    return pickg(gi1), pickg(gi2), pickg(gs1), pickg(gs2)


def _merge_body(oa_ref, ob_ref, on_ref, gate_ref, x_ref, xctx_ref, wb_ref, wo_ref, g1_ref, sh2_ref, sc2_ref, n2_ref,
                wrh_ref, wrl_ref, rb_ref,
                x1_ref, h2_ref, bkt_ref, rank_ref, cnt_ref, carry_ref, *, n_lat_tiles):
    first = (pl.program_id(0) == 0) & (pl.program_id(1) == 0)
    x_res = jnp.where(pl.program_id(1) < n_lat_tiles, x_ref[0], xctx_ref[0])

    @pl.when(first)
    def _():
        carry_ref[...] = jnp.zeros_like(carry_ref)

    outs = (oa_ref, ob_ref, on_ref)
    y = None
    for n in range(3):
        gt = jax.nn.sigmoid(gate_ref[0, :, n * D_MODEL:(n + 1) * D_MODEL].astype(_F32))
        term = gt * _dot(outs[n][0], wb_ref[n])
        y = term if y is None else y + term
    z = _dot(y.astype(_BF16), wo_ref[...])
    x1 = x_res + g1_ref[0] * z
    x1_ref[0] = x1
    ms = jnp.mean(x1 * x1, axis=-1, keepdims=True)
    h2 = (x1 * lax.rsqrt(ms + EPS) * n2_ref[...]) * (1.0 + sc2_ref[0]) + sh2_ref[0]
    h2_ref[0, :, :D_MODEL] = h2

    hb = h2.astype(_BF16)
    hl = (h2 - hb.astype(_F32)).astype(_BF16)
    logits = _nt(wrh_ref[...], hb) + _nt(wrh_ref[...], hl) + _nt(wrl_ref[...], hb)
    scores = jax.nn.sigmoid(logits)
    sel = scores + rb_ref[...]
    sel_rows = [sel[e:e + 1, :] for e in range(N_EXPERTS)]
    sc_rows = [scores[e:e + 1, :] for e in range(N_EXPERTS)]
    e1, e2, s1, s2 = _route_rows(sel_rows, sc_rows)
    den = s1 + s2
    first_lo = e1 < e2
    e_lo = jnp.where(first_lo, e1, e2)
    e_hi = jnp.where(first_lo, e2, e1)
    w_lo = jnp.where(first_lo, s1, s2) / den
    w_hi = jnp.where(first_lo, s2, s1) / den
    a = e_lo & 3
    b = e_hi & 3
    bkt = (e_lo >> 2) * 6 + jnp.where(a == 0, 0, jnp.where(a == 1, 3, 5)) + (b - a - 1)
    bkt_ref[...] = bkt

    bid = lax.broadcasted_iota(jnp.int32, (BUCKET_ROWS, TM), 0)
    oh = bid == bkt
    tri = (lax.broadcasted_iota(jnp.int32, (TM, TM), 0) <= lax.broadcasted_iota(jnp.int32, (TM, TM), 1))
    cum = _dot(oh.astype(_BF16), tri.astype(_BF16))
    carry = carry_ref[:, 0:1]
    rank_ref[...] = jnp.sum(jnp.where(oh, carry + cum - 1.0, 0.0), axis=0, keepdims=True).astype(jnp.int32)
    new_carry = carry + cum[:, TM - 1:TM]
    carry_ref[...] = jnp.broadcast_to(new_carry, carry_ref.shape)
    cnt_ref[...] = jnp.broadcast_to(new_carry, cnt_ref.shape).astype(jnp.int32)

    lane = lax.broadcasted_iota(jnp.int32, (TM, 128), 1)

    def col(row):
        return jnp.concatenate([_col_from_row(row[:, h * 128:(h + 1) * 128]) for h in range(TM // 128)], axis=0)

    h2_ref[0, :, D_MODEL:] = jnp.where(lane == 0, col(w_lo), jnp.where(lane == 1, col(w_hi), 0.0))


def _merge_call(oa, ob, on, gate, x_lat, x_ctx, ctx_tile, wb, wo, g1, sh2, sc2, n2, wrh, wrl, rb, t_out,
                n_lat_tiles):
    b, _, d = x_lat.shape
    nb = g1.shape[0] - 1
    nt = t_out // TM
    n_tok = b * t_out

    def mod_idx(bi, i):
        return (jnp.where(i < n_lat_tiles, bi, nb), 0, 0)

    def tok(width):
        return pl.BlockSpec((1, TM, width), lambda bi, i: (bi, i, 0))

    def full(shape):
        return pl.BlockSpec(shape, lambda bi, i: (0,) * len(shape))

    lane_out = pl.BlockSpec((1, TM), lambda bi, i: (0, bi * nt + i))
    return pl.pallas_call(
        functools.partial(_merge_body, n_lat_tiles=n_lat_tiles),
        grid=(b, nt),
        in_specs=[tok(BRANCH), tok(BRANCH), tok(BRANCH), tok(3 * d), *_token_specs(n_lat_tiles, ctx_tile, d),
                  full((3, BRANCH, d)), full((d, d)),
                  pl.BlockSpec((1, 1, d), mod_idx), pl.BlockSpec((1, 1, d), mod_idx),
                  pl.BlockSpec((1, 1, d), mod_idx), full((1, d)),
                  full((N_EXPERTS, d)), full((N_EXPERTS, d)), full((N_EXPERTS, 1))],
        out_specs=[tok(d), tok(ROW_W), lane_out, lane_out, full((BUCKET_ROWS, 128))],
        out_shape=[jax.ShapeDtypeStruct((b, t_out, d), _F32), jax.ShapeDtypeStruct((b, t_out, ROW_W), _F32),
                   jax.ShapeDtypeStruct((1, n_tok), jnp.int32), jax.ShapeDtypeStruct((1, n_tok), jnp.int32),
                   jax.ShapeDtypeStruct((BUCKET_ROWS, 128), jnp.int32)],
        scratch_shapes=[pltpu.VMEM((BUCKET_ROWS, 128), _F32)],
        compiler_params=_cparams(("arbitrary", "arbitrary"), VMEM_LIMIT),
        name="merge_route",
    )(oa, ob, on, gate, x_lat, x_ctx, wb, wo, g1, sh2, sc2, n2, wrh, wrl, rb)


ROW_DMA_UNROLL = 8


def _dispatch_body(pad_start_ref, pad_n_ref, nv_ref, pos_ref, h_ref, xs_ref, zbuf, sem, zsem):
    @pl.when(pl.program_id(0) == 0)
    def _():
        zbuf[...] = jnp.zeros_like(zbuf)

        def zero_row(e, r):
            return pltpu.make_async_copy(zbuf.at[pl.ds(0, 1)], xs_ref.at[pl.ds(pad_start_ref[e] + r, 1)], zsem)

        def zero_tile(i):
            return pltpu.make_async_copy(zbuf, xs_ref.at[pl.ds(pl.multiple_of(i * TM_E, TM_E), TM_E)], zsem)

        n_tiles = xs_ref.shape[0] // TM_E
        for e in range(N_BUCKETS):
            lax.fori_loop(0, pad_n_ref[e], lambda r, c, e=e: (zero_row(e, r).start(), c)[1], 0)
        lax.fori_loop(nv_ref[0], n_tiles, lambda i, c: (zero_tile(i).start(), c)[1], 0)
        for e in range(N_BUCKETS):
            lax.fori_loop(0, pad_n_ref[e], lambda r, c, e=e: (zero_row(e, r).wait(), c)[1], 0)
        lax.fori_loop(nv_ref[0], n_tiles, lambda i, c: (zero_tile(i).wait(), c)[1], 0)

    def row_copy(t):
        return pltpu.make_async_copy(h_ref.at[pl.ds(t, 1)], xs_ref.at[pl.ds(pos_ref[0, 0, t], 1)], sem)

    lax.fori_loop(0, TM, lambda t, c: (row_copy(t).start(), c)[1], 0, unroll=ROW_DMA_UNROLL)
    lax.fori_loop(0, TM, lambda t, c: (row_copy(t).wait(), c)[1], 0, unroll=ROW_DMA_UNROLL)


def _dispatch_call(pad_start, pad_n, n_valid, pos3, h2, p_rows):
    n_tiles = pos3.shape[0]
    d = h2.shape[1]
    return pl.pallas_call(
        _dispatch_body,
        grid_spec=pltpu.PrefetchScalarGridSpec(
            num_scalar_prefetch=3,
            grid=(n_tiles,),
            in_specs=[pl.BlockSpec((1, 1, TM), lambda i, ps, pn, nv: (i, 0, 0), memory_space=pltpu.SMEM),
                      pl.BlockSpec((TM, d), lambda i, ps, pn, nv: (i, 0))],
            out_specs=pl.BlockSpec(memory_space=pl.ANY),
            scratch_shapes=[pltpu.VMEM((TM_E, d), h2.dtype), pltpu.SemaphoreType.DMA, pltpu.SemaphoreType.DMA],
        ),
        out_shape=jax.ShapeDtypeStruct((p_rows, d), h2.dtype),
        compiler_params=_cparams(("arbitrary",)),
        name="moe_dispatch",
    )(pad_start, pad_n, n_valid, pos3, h2)


def _moe_body(elo_ref, ehi_ref, nv_ref, xs_ref, wg_lo, wu_lo, wd_lo, wg_hi, wu_hi, wd_hi, ys_ref):
    del elo_ref, ehi_ref
    live = pl.program_id(0) < nv_ref[0]

    @pl.when(live)
    def _():
        xb = xs_ref[:, :D_MODEL].astype(_BF16)
        wts = xs_ref[:, D_MODEL:]

        def expert(wg_ref, wu_ref, wd_ref):
            gate = _dot(xb, wg_ref[0].astype(_BF16))
            up = _dot(xb, wu_ref[0].astype(_BF16))
            act = (gate * jax.nn.sigmoid(gate)) * up
            return _dot(act.astype(_BF16), wd_ref[0].astype(_BF16))

        ys_ref[...] = wts[:, 0:1] * expert(wg_lo, wu_lo, wd_lo) + wts[:, 1:2] * expert(wg_hi, wu_hi, wd_hi)

    @pl.when(jnp.logical_not(live))
    def _():
        ys_ref[...] = jnp.zeros_like(ys_ref)


def _moe_call(tile_elo, tile_ehi, n_valid, xs, wg, wu, wd, layer):
    p_rows = xs.shape[0]
    d = D_MODEL
    n_tiles = p_rows // TM_E

    def row_idx(i, elo, ehi, nv):
        return (jnp.minimum(i, nv[0] - 1), 0)

    def lo_idx(i, elo, ehi, nv):
        return (layer, elo[i], 0, 0)

    def hi_idx(i, elo, ehi, nv):
        return (layer, ehi[i], 0, 0)

    def w_specs(idx):
        return [pl.BlockSpec((None, 1, d, D_FF), idx), pl.BlockSpec((None, 1, d, D_FF), idx),
                pl.BlockSpec((None, 1, D_FF, d), idx)]

    return pl.pallas_call(
        _moe_body,
        grid_spec=pltpu.PrefetchScalarGridSpec(
            num_scalar_prefetch=3,
            grid=(n_tiles,),
            in_specs=[pl.BlockSpec((TM_E, ROW_W), row_idx), *w_specs(lo_idx), *w_specs(hi_idx)],
            out_specs=pl.BlockSpec((TM_E, d), lambda i, elo, ehi, nv: (i, 0)),
        ),
        out_shape=jax.ShapeDtypeStruct((p_rows, d), _F32),
        compiler_params=_cparams(("arbitrary",), VMEM_LIMIT),
        name="moe_experts",
    )(tile_elo, tile_ehi, n_valid, xs, wg, wu, wd, wg, wu, wd)


def _combine_body(pos0_ref, posn_ref, x1_ref, g2_ref, fn_ref, ys_ref, o_ref, gbuf, sems):
    step = pl.program_id(0)
    y, issue_next, finish = _moe_rows_pipeline(step, pl.num_programs(0) - 1, pos0_ref, posn_ref, ys_ref,
                                               gbuf, sems)
    x2 = x1_ref[...] + g2_ref[0] * y
    issue_next()
    ms = jnp.mean(x2 * x2, axis=-1, keepdims=True)
    o_ref[...] = x2 * lax.rsqrt(ms + EPS) * fn_ref[...]
    finish()


def _combine_call(pos3, x1, g2, fnorm, ys, t_out, n_lat_tiles):
    n_tok, d = x1.shape
    nb = g2.shape[0] - 1
    nt = t_out // TM
    n_tiles = n_tok // TM

    def mod_idx(i):
        return (jnp.where(i % nt < n_lat_tiles, i // nt, nb), 0, 0)

    smem_tile = functools.partial(pl.BlockSpec, (1, 1, TM), memory_space=pltpu.SMEM)
    return pl.pallas_call(
        _combine_body,
        grid=(n_tiles,),
        in_specs=[smem_tile(lambda i: (0, 0, 0)),
                  smem_tile(lambda i: (jnp.minimum(i + 1, n_tiles - 1), 0, 0)),
                  pl.BlockSpec((TM, d), lambda i: (i, 0)),
                  pl.BlockSpec((1, 1, d), mod_idx),
                  pl.BlockSpec((1, d), lambda i: (0, 0)),
                  pl.BlockSpec(memory_space=pl.ANY)],
        out_specs=pl.BlockSpec((TM, d), lambda i: (i, 0)),
        out_shape=jax.ShapeDtypeStruct((n_tok, d), _F32),
        scratch_shapes=[pltpu.VMEM((2, TM, d), _F32), pltpu.SemaphoreType.DMA((2,))],
        compiler_params=_cparams(("arbitrary",)),
        name="moe_combine",
    )(pos3, pos3, x1, g2, fnorm, ys)


def _rope_tables(s_len, l_ctx):
    t = jnp.arange(s_len, dtype=jnp.int32)
    row = (t // GRID_W).astype(_F32)
    col = (t % GRID_W).astype(_F32)
    half = HEAD_DIM // 2
    inv = ROPE_THETA ** (-jnp.arange(0, half, 2, dtype=_F32) / half)
    ar = row[:, None] * inv
    ac = col[:, None] * inv
    cos64 = jnp.concatenate([jnp.cos(ar), jnp.cos(ar), jnp.cos(ac), jnp.cos(ac)], axis=1)
    sin64 = jnp.concatenate([-jnp.sin(ar), jnp.sin(ar), -jnp.sin(ac), jnp.sin(ac)], axis=1)
    cos_t = jnp.concatenate([jnp.tile(cos64, (1, 2)), jnp.ones((l_ctx, 128), _F32)], axis=0)
    sin_t = jnp.concatenate([jnp.tile(sin64, (1, 2)), jnp.zeros((l_ctx, 128), _F32)], axis=0)
    return cos_t, sin_t


def _block_diag_ones():
    r = np.arange(BRANCH)
    return jnp.asarray((r[:, None] // HEAD_DIM) == (r[None, :] // HEAD_DIM), dtype=_BF16)


def kernel(x, c, ctx, c_ctx, w_mod, b_mod, norm1, norm2, w_in, q_norm_a, k_norm_a, sink_b, rpb_n, w_branch, w_out,
           w_router, router_bias, w_e_gate, w_e_up, w_e_down, final_norm):
    b, s_len, d = x.shape
    l_ctx = ctx.shape[1]
    depth = w_in.shape[0]
    t_all = s_len + l_ctx
    rows = s_len // GRID_W
    assert d == D_MODEL and l_ctx == TM and s_len % TM == 0 and b + 1 <= 16
    assert s_len >= TM + 2 * WINDOW and rows >= NA_WIN_ROWS and rows % 4 == 0
    n_lat_tiles = s_len // TM

    cos_t, sin_t = _rope_tables(s_len, l_ctx)
    bd = _block_diag_ones()
    cvec = jnp.zeros((16, d), _F32).at[:b].set(c).at[b].set(c_ctx)
    mods = _mod_call(cvec, w_mod, b_mod)[:, :b + 1].reshape(depth, b + 1, 6, 1, d)

    wr = w_router.T.astype(_F32)
    wrh = wr.astype(_BF16)
    wrl = (wr - wrh.astype(_F32)).astype(_BF16)
    rb = router_bias.astype(_F32).reshape(N_EXPERTS, 1)

    x_lat, x_ctx, ctx_tile = x, ctx, 0
    pending = None
    for l in range(depth):
        last = l == depth - 1
        sh1, sc1, g1, sh2, sc2, g2 = [mods[l, :, n] for n in range(6)]
        w_f = w_in[l]
        w_l = jnp.concatenate([_pair_heads(w_f[:, OFF_QA:OFF_KA], 1), w_f[:, OFF_KA:OFF_QB],
                               _pair_heads(w_f[:, OFF_QB:OFF_KB], 1), w_f[:, OFF_KB:]], axis=1).astype(_BF16)
        gq = (jnp.tile(q_norm_a[l].astype(_F32), N_HEADS) * Q_SCALE).reshape(1, BRANCH)
        gk = jnp.tile(k_norm_a[l].astype(_F32), 2).reshape(1, KV_A)
        outs = _inproj_call(x_lat, x_ctx, ctx_tile, n_lat_tiles, norm1[l].reshape(1, d), sh1, sc1, w_l, cos_t, sin_t,
                            gq, gk, bd, combine=pending)
        qa, ka, va, qb, kb, vb, qn, kn, vn, gate = outs[:10]
        if pending is not None:
            x_lat = x_ctx = outs[10]

        sink = sink_b[l].astype(_F32) * LOG2_E
        oa = _attn_a_call(qa, ka, va, s_len, not last)
        ob = _attn_b_call(qb, kb, vb, sink, s_len, not last)
        on = _attn_n_call(qn, kn, vn, _na_bias_table(rpb_n[l], rows), s_len, not last)

        t_out = s_len if last else t_all
        wb = jnp.stack([_pair_heads(w_branch[l, 0], 0), _pair_heads(w_branch[l, 1], 0),
                        w_branch[l, 2]]).astype(_BF16)
        x1, h2w, bkt, rank, cnt = _merge_call(
            oa, ob, on, gate, x_lat, x_ctx, ctx_tile, wb, w_out[l].astype(_BF16), g1, sh2, sc2,
            norm2[l].reshape(1, d), wrh, wrl, rb, t_out, n_lat_tiles)

        n_tok = b * t_out
        counts = cnt[:N_BUCKETS, 0]
        padded = ((counts + TM_E - 1) // TM_E) * TM_E
        ends = jnp.cumsum(padded)
        offs = ends - padded
        pos = rank[0] + jnp.sum(jnp.where(bkt == jnp.arange(N_BUCKETS)[:, None], offs[:, None], 0), axis=0)
        pos3 = pos.reshape(n_tok // TM, 1, TM)
        p_rows = n_tok + N_BUCKETS * TM_E
        n_tiles = p_rows // TM_E
        n_valid = (ends[-1] // TM_E).astype(jnp.int32).reshape(1)
        tile_start = jnp.arange(n_tiles, dtype=jnp.int32) * TM_E
        tile_bucket = jnp.minimum(jnp.sum(tile_start[:, None] >= ends[None, :], axis=1), N_BUCKETS - 1)
        tile_group = (tile_bucket // 6) * EXPERTS_PER_GROUP
        tile_elo = (tile_group + jnp.asarray(_PAIR_LO)[tile_bucket % 6]).astype(jnp.int32)
        tile_ehi = (tile_group + jnp.asarray(_PAIR_HI)[tile_bucket % 6]).astype(jnp.int32)

        pad_start = (offs + counts).astype(jnp.int32)
        pad_n = (padded - counts).astype(jnp.int32)
        xs = _dispatch_call(pad_start, pad_n, n_valid, pos3, h2w.reshape(n_tok, ROW_W), p_rows)
        ys = _moe_call(tile_elo, tile_ehi, n_valid, xs, w_e_gate, w_e_up, w_e_down, l)
        if last:
            out = _combine_call(pos3, x1.reshape(n_tok, d), g2, final_norm.reshape(1, d), ys, t_out, n_lat_tiles)
            return out.reshape(b, t_out, d)
        pending = (pos3, g2, ys)
        x_lat = x_ctx = x1
        ctx_tile = n_lat_tiles
```

```python
import functools

import numpy as np
import jax
import jax.numpy as jnp
from jax import lax
from jax.experimental import pallas as pl
from jax.experimental.pallas import tpu as pltpu

D_MODEL = 1024
HEAD_DIM = 64
GRID_W = 64
N_HEADS = 8
WINDOW = 128
NA_ROWS = 8
NA_COLS = 16
ROPE_THETA = 10000.0
N_EXPERTS = 16
N_GROUPS = 4
EXPERTS_PER_GROUP = N_EXPERTS // N_GROUPS
D_FF = 512
EPS = 1e-6
NEG_INF = -1e30
ATTN_SCALE = HEAD_DIM ** -0.5
LOG2_E = 1.4426950408889634
Q_SCALE = ATTN_SCALE * LOG2_E
BRANCH = N_HEADS * HEAD_DIM
KV_A = 2 * HEAD_DIM
IN_COLS = 6144

OFF_QA, OFF_KA, OFF_VA = 0, 512, 640
OFF_QB, OFF_KB, OFF_VB = 768, 1280, 1408
OFF_QN, OFF_KN, OFF_VN = 1536, 2048, 2560
OFF_GATE = 3072

TM = 256
TM_E = 512
TM_D = 1024
N_BUCKETS = N_GROUPS * 6
BUCKET_ROWS = 32
ROW_W = D_MODEL + 128
_PAIR_LO = np.array([0, 0, 0, 1, 1, 2])
_PAIR_HI = np.array([1, 2, 3, 2, 3, 3])
NA_WIN_ROWS = 12
VMEM_LIMIT = 56 * 1024 * 1024

_F32 = jnp.float32
_BF16 = jnp.bfloat16

def _pair_heads(w, axis):
    shape = w.shape
    split = shape[:axis] + (2, 4, HEAD_DIM) + shape[axis + 1:]
    return jnp.swapaxes(w.reshape(split), axis, axis + 1).reshape(shape)


def _nt(a, b):
    return lax.dot_general(a, b, (((1,), (1,)), ((), ())), preferred_element_type=_F32)


def _dot(a, b):
    return jnp.dot(a, b, preferred_element_type=_F32)


def _cparams(sem, vmem=None):
    return pltpu.CompilerParams(dimension_semantics=sem, vmem_limit_bytes=vmem)


def _mod_body(c_ref, w_ref, b_ref, o_ref):
    cv = c_ref[...]
    a = (cv * jax.nn.sigmoid(cv)).astype(_BF16)
    o_ref[0] = _dot(a, w_ref[0].astype(_BF16)) + b_ref[0]


def _mod_call(cvec, w_mod, b_mod):
    depth, d, n = w_mod.shape
    tn = 1536
    return pl.pallas_call(
        _mod_body,
        grid=(depth, n // tn),
        in_specs=[
            pl.BlockSpec((16, d), lambda l, j: (0, 0)),
            pl.BlockSpec((1, d, tn), lambda l, j: (l, 0, j)),
            pl.BlockSpec((1, 1, tn), lambda l, j: (l, 0, j)),
        ],
        out_specs=pl.BlockSpec((1, 16, tn), lambda l, j: (l, 0, j)),
        out_shape=jax.ShapeDtypeStruct((depth, 16, n), _F32),
        compiler_params=_cparams(("arbitrary", "arbitrary")),
        name="mod_vectors",
    )(cvec, w_mod, b_mod.reshape(depth, 1, n))


def _rope(v, c, s):
    w = v.shape[1]
    lane = lax.broadcasted_iota(jnp.int32, v.shape, 1)
    first = (lane & 16) == 0
    sw = jnp.where(first, pltpu.roll(v, w - 16, 1), pltpu.roll(v, 16, 1))
    return v * c + sw * s


def _inproj_body(x_ref, xctx_ref, *refs, n_lat_tiles):
    x = jnp.where(pl.program_id(1) < n_lat_tiles, x_ref[0], xctx_ref[0])
    _inproj_core(x, *refs)


def _col_from_row(row):
    blk = jnp.broadcast_to(row, (128, 128))
    return blk.T[:, 0:1]


def _moe_rows_pipeline(step, last_step, pos0_ref, posn_ref, ys_ref, gbuf, sems):
    def row_copy(pos_ref, slot, t):
        return pltpu.make_async_copy(ys_ref.at[pl.ds(pos_ref[0, 0, t], 1)], gbuf.at[slot, pl.ds(t, 1)],
                                     sems.at[slot])

    def gather(pos_ref, slot):
        for t in range(TM):
            row_copy(pos_ref, slot, t).start()

    def drain(slot):
        for t in range(TM):
            row_copy(pos0_ref, slot, t).wait()

    pl.when(step == 0)(lambda: gather(pos0_ref, 0))
    slot = step % 2
    drain(slot)
    y = gbuf[slot]
    return (y, lambda: gather(posn_ref, (step + 1) % 2),
            lambda: pl.when(step == last_step)(lambda: drain((step + 1) % 2)))


def _inproj_combine_body(pos0_ref, posn_ref, x_ref, xctx_ref, g2_ref, ys_ref, *refs, n_lat_tiles):
    core_refs, x2_ref, gbuf, sems = refs[:-3], refs[-3], refs[-2], refs[-1]
    nt = pl.num_programs(1)
    step = pl.program_id(0) * nt + pl.program_id(1)
    y, issue_next, finish = _moe_rows_pipeline(step, pl.num_programs(0) * nt - 1, pos0_ref, posn_ref,
                                               ys_ref, gbuf, sems)
    x1 = jnp.where(pl.program_id(1) < n_lat_tiles, x_ref[0], xctx_ref[0])
    x = x1 + g2_ref[0] * y
    x2_ref[0] = x
    issue_next()
    _inproj_core(x, *core_refs)
    finish()


def _inproj_core(x, g_ref, sh_ref, sc_ref, w_ref, cos_ref, sin_ref, gq_ref, gk_ref, bd_ref,
                 qa_ref, ka_ref, va_ref, qb_ref, kb_ref, vb_ref, qn_ref, kn_ref, vn_ref, gate_ref):
    ms = jnp.mean(x * x, axis=-1, keepdims=True)
    h = (x * lax.rsqrt(ms + EPS) * g_ref[...]) * (1.0 + sc_ref[0]) + sh_ref[0]
    hb = h.astype(_BF16)

    def seg(off, width):
        return _dot(hb, w_ref[:, off:off + width])

    c1 = cos_ref[...]
    s1 = sin_ref[...]
    c4 = jnp.concatenate([c1] * 4, axis=1)
    s4 = jnp.concatenate([s1] * 4, axis=1)
    bd = bd_ref[...]

    qa = seg(OFF_QA, BRANCH)
    ssq = _dot((qa * qa).astype(_BF16), bd)
    qa = qa * lax.rsqrt(ssq * (1.0 / HEAD_DIM) + EPS) * gq_ref[...]
    qa_ref[0] = _rope(qa, c4, s4).astype(_BF16)
    ka = seg(OFF_KA, KV_A)
    ssk = _dot((ka * ka).astype(_BF16), bd_ref[:KV_A, :KV_A])
    ka = ka * lax.rsqrt(ssk * (1.0 / HEAD_DIM) + EPS) * gk_ref[...]
    ka_ref[0] = _rope(ka, c1, s1).astype(_BF16)
    va_ref[0] = seg(OFF_VA, KV_A).astype(_BF16)
    qb_ref[0] = _rope(seg(OFF_QB, BRANCH) * Q_SCALE, c4, s4).astype(_BF16)
    kb_ref[0] = _rope(seg(OFF_KB, KV_A), c1, s1).astype(_BF16)
    vb_ref[0] = seg(OFF_VB, KV_A).astype(_BF16)
    qn_ref[0] = (seg(OFF_QN, BRANCH) * Q_SCALE).astype(_BF16)
    kn_ref[0] = seg(OFF_KN, BRANCH).astype(_BF16)
    vn_ref[0] = seg(OFF_VN, BRANCH).astype(_BF16)
    for n in range(3):
        gate_ref[0, :, n * D_MODEL:(n + 1) * D_MODEL] = seg(OFF_GATE + n * D_MODEL, D_MODEL).astype(_BF16)


def _token_specs(n_lat_tiles, ctx_tile, d):
    lat = pl.BlockSpec((1, TM, d), lambda bi, i: (bi, jnp.minimum(i, n_lat_tiles - 1), 0))
    ctx = pl.BlockSpec((1, TM, d), lambda bi, i: (bi, ctx_tile, 0))
    return lat, ctx


def _inproj_call(x_lat, x_ctx, ctx_tile, n_lat_tiles, g1n, sh, sc, w, cos_t, sin_t, gq, gk, bd, combine=None):
    b, _, d = x_lat.shape
    nb = sh.shape[0] - 1
    nt = n_lat_tiles + 1
    t = nt * TM

    def mod_idx(bi, i):
        return (jnp.where(i < n_lat_tiles, bi, nb), 0, 0)

    def tok(width):
        return pl.BlockSpec((1, TM, width), lambda bi, i: (bi, i, 0))

    widths = [BRANCH, KV_A, KV_A, BRANCH, KV_A, KV_A, BRANCH, BRANCH, BRANCH, 3 * D_MODEL]
    core_specs = [
        pl.BlockSpec((1, d), lambda bi, i: (0, 0)),
        pl.BlockSpec((1, 1, d), mod_idx),
        pl.BlockSpec((1, 1, d), mod_idx),
        pl.BlockSpec((d, IN_COLS), lambda bi, i: (0, 0)),
        pl.BlockSpec((TM, 128), lambda bi, i: (i, 0)),
        pl.BlockSpec((TM, 128), lambda bi, i: (i, 0)),
        pl.BlockSpec((1, BRANCH), lambda bi, i: (0, 0)),
        pl.BlockSpec((1, KV_A), lambda bi, i: (0, 0)),
        pl.BlockSpec((BRANCH, BRANCH), lambda bi, i: (0, 0)),
    ]
    core_args = (g1n, sh, sc, w, cos_t, sin_t, gq, gk, bd)
    out_specs = [tok(wd) for wd in widths]
    out_shape = [jax.ShapeDtypeStruct((b, t, wd), _BF16) for wd in widths]
    if combine is None:
        return pl.pallas_call(
            functools.partial(_inproj_body, n_lat_tiles=n_lat_tiles),
            grid=(b, nt),
            in_specs=[*_token_specs(n_lat_tiles, ctx_tile, d), *core_specs],
            out_specs=out_specs,
            out_shape=out_shape,
            compiler_params=_cparams(("arbitrary", "arbitrary"), VMEM_LIMIT),
            name="inproj",
        )(x_lat, x_ctx, *core_args)

    pos3, g2, ys = combine
    last_tile = b * nt - 1
    smem_tile = functools.partial(pl.BlockSpec, (1, 1, TM), memory_space=pltpu.SMEM)
    return pl.pallas_call(
        functools.partial(_inproj_combine_body, n_lat_tiles=n_lat_tiles),
        grid=(b, nt),
        in_specs=[
            smem_tile(lambda bi, i: (0, 0, 0)),
            smem_tile(lambda bi, i: (jnp.minimum(bi * nt + i + 1, last_tile), 0, 0)),
            *_token_specs(n_lat_tiles, ctx_tile, d),
            pl.BlockSpec((1, 1, d), mod_idx),
            pl.BlockSpec(memory_space=pl.ANY),
            *core_specs,
        ],
        out_specs=out_specs + [tok(d)],
        out_shape=out_shape + [jax.ShapeDtypeStruct((b, t, d), _F32)],
        scratch_shapes=[pltpu.VMEM((2, TM, d), _F32), pltpu.SemaphoreType.DMA((2,))],
        compiler_params=_cparams(("arbitrary", "arbitrary"), VMEM_LIMIT),
        name="inproj_combine",
    )(pos3, pos3, x_lat, x_ctx, g2, ys, *core_args)


def _stack_heads(q):
    lane = lax.broadcasted_iota(jnp.int32, q.shape, 1)
    lo = lane < HEAD_DIM
    zero = jnp.zeros_like(q)
    return jnp.concatenate([jnp.where(lo, q, zero), jnp.where(lo, zero, q)], axis=0)


def _unstack_heads(o):
    half = o.shape[0] // 2
    lane = lax.broadcasted_iota(jnp.int32, (half, o.shape[1]), 1)
    return jnp.where(lane < HEAD_DIM, o[:half], o[half:])


KA_CHUNK_MAX = 2048
KA_GROUPS = 4


def _flash_update(lhs, kc, vc, m, l128, acc):
    s = _nt(lhs, kc)
    tiles = [s[:, t * 128:(t + 1) * 128] for t in range(s.shape[1] // 128)]
    m_new = jnp.maximum(m, jnp.max(functools.reduce(jnp.maximum, tiles), axis=1, keepdims=True))
    alpha = jnp.exp2(m - m_new)
    ps = [jnp.exp2(t - m_new) for t in tiles]
    l128 = alpha * l128 + functools.reduce(lambda a, c: a + c, ps)
    p = jnp.concatenate([x.astype(_BF16) for x in ps], axis=1)
    acc = alpha * acc + _dot(p, vc)
    return m_new, l128, acc


def _attn_a_body(q_ref, k_ref, v_ref, o_ref, *, s_len, l_ctx, with_ctx):
    def run(chunks):
        for g in range(KA_GROUPS):
            sl = slice(g * 128, (g + 1) * 128)
            lhs = _stack_heads(q_ref[0, :, sl])
            m = jnp.full((2 * TM, 128), NEG_INF, _F32)
            l128 = jnp.zeros((2 * TM, 128), _F32)
            acc = jnp.zeros((2 * TM, 128), _F32)
            for start, size in chunks:
                m, l128, acc = _flash_update(lhs, k_ref[0, start:start + size, :], v_ref[0, start:start + size, :],
                                             m, l128, acc)
            l = jnp.sum(l128, axis=1, keepdims=True)
            o_ref[0, :, sl] = _unstack_heads(acc / l).astype(_BF16)

    ctx_chunk = (s_len, l_ctx)
    chunk = KA_CHUNK_MAX
    while s_len % chunk:
        chunk //= 2
    lat_chunks = [(c * chunk, chunk) for c in range(s_len // chunk)] + [ctx_chunk]
    if with_ctx:
        is_ctx = pl.program_id(1) == s_len // TM
        pl.when(jnp.logical_not(is_ctx))(lambda: run(lat_chunks))
        pl.when(is_ctx)(lambda: run([ctx_chunk]))
    else:
        run(lat_chunks)


def _attn_a_call(q, k, v, s_len, with_ctx):
    b, t, _ = q.shape
    nq = s_len // TM + (1 if with_ctx else 0)
    return pl.pallas_call(
        functools.partial(_attn_a_body, s_len=s_len, l_ctx=t - s_len, with_ctx=with_ctx),
        grid=(b, nq, 4 // KA_GROUPS),
        in_specs=[
            pl.BlockSpec((1, TM, 128 * KA_GROUPS), lambda bi, i, j: (bi, i, j)),
            pl.BlockSpec((1, t, 128), lambda bi, i, j: (bi, 0, 0)),
            pl.BlockSpec((1, t, 128), lambda bi, i, j: (bi, 0, 0)),
        ],
        out_specs=pl.BlockSpec((1, TM, 128 * KA_GROUPS), lambda bi, i, j: (bi, i, j)),
        out_shape=jax.ShapeDtypeStruct((b, nq * TM, BRANCH), _BF16),
        compiler_params=_cparams(("arbitrary", "arbitrary", "arbitrary"), VMEM_LIMIT),
        name="attn_global",
    )(q, k, v)


def _softmax_parts(parts, extra=None):
    m = functools.reduce(jnp.maximum, [jnp.max(s, axis=1, keepdims=True) for s in parts])
    if extra is not None:
        m = jnp.maximum(m, extra)
    ps = [jnp.exp2(s - m) for s in parts]
    l = functools.reduce(lambda a, c: a + c, [jnp.sum(p, axis=1, keepdims=True) for p in ps])
    if extra is not None:
        l = l + jnp.exp2(extra - m)
    return ps, l


def _attn_b_body(sink_ref, q_ref, k_ref, v_ref, o_ref, *, s_len, l_ctx, with_ctx):
    span = TM + 2 * WINDOW
    i = pl.program_id(1)
    is_hi = lax.broadcasted_iota(jnp.int32, (2 * TM, 1), 0) >= TM

    def finish(j, ps, vs, l):
        o = functools.reduce(lambda a, c: a + c, [_dot(p.astype(_BF16), v) for p, v in zip(ps, vs)]) / l
        o_ref[0, :, j * 128:(j + 1) * 128] = _unstack_heads(o).astype(_BF16)

    def latent():
        start = i * TM
        ws = pl.multiple_of(jnp.clip(start - WINDOW, 0, s_len - span), WINDOW)
        kw = k_ref[0, pl.ds(ws, span), :]
        vw = v_ref[0, pl.ds(ws, span), :]
        kc = k_ref[0, s_len:s_len + l_ctx, :]
        vc = v_ref[0, s_len:s_len + l_ctx, :]
        row = lax.broadcasted_iota(jnp.int32, (2 * TM, span), 0)
        qpos = start + jnp.where(row >= TM, row - TM, row)
        kpos = ws + lax.broadcasted_iota(jnp.int32, (2 * TM, span), 1)
        band = jnp.abs(kpos - qpos) <= WINDOW
        scores = []
        for j in range(4):
            lhs = _stack_heads(q_ref[0, :, j * 128:(j + 1) * 128])
            scores.append((_nt(lhs, kw), _nt(lhs, kc)))
        for j in range(4):
            s_loc = jnp.where(band, scores[j][0], NEG_INF)
            sink = jnp.where(is_hi, sink_ref[4 + j], sink_ref[j])
            ps, l = _softmax_parts([s_loc, scores[j][1]], sink)
            finish(j, ps, (vw, vc), l)

    def context():
        kc = k_ref[0, s_len:s_len + l_ctx, :]
        vc = v_ref[0, s_len:s_len + l_ctx, :]
        for j in range(4):
            lhs = _stack_heads(q_ref[0, :, j * 128:(j + 1) * 128])
            sink = jnp.where(is_hi, sink_ref[4 + j], sink_ref[j])
            ps, l = _softmax_parts([_nt(lhs, kc)], sink)
            finish(j, ps, (vc,), l)

    if with_ctx:
        is_ctx = i == s_len // TM
        pl.when(jnp.logical_not(is_ctx))(latent)
        pl.when(is_ctx)(context)
    else:
        latent()


def _attn_b_call(q, k, v, sink, s_len, with_ctx):
    b, t, _ = q.shape
    nq = s_len // TM + (1 if with_ctx else 0)
    return pl.pallas_call(
        functools.partial(_attn_b_body, s_len=s_len, l_ctx=t - s_len, with_ctx=with_ctx),
        grid=(b, nq),
        in_specs=[
            pl.BlockSpec(memory_space=pltpu.SMEM),
            pl.BlockSpec((1, TM, BRANCH), lambda bi, i: (bi, i, 0)),
            pl.BlockSpec((1, t, 128), lambda bi, i: (bi, 0, 0)),
            pl.BlockSpec((1, t, 128), lambda bi, i: (bi, 0, 0)),
        ],
        out_specs=pl.BlockSpec((1, TM, BRANCH), lambda bi, i: (bi, i, 0)),
        out_shape=jax.ShapeDtypeStruct((b, nq * TM, BRANCH), _BF16),
        compiler_params=_cparams(("arbitrary", "arbitrary"), VMEM_LIMIT),
        name="attn_window",
    )(sink, q, k, v)


def _attn_n_body(q_ref, k_ref, v_ref, bias_ref, o_ref, *, s_len, l_ctx, with_ctx):
    rows = s_len // GRID_W
    span = NA_WIN_ROWS * GRID_W
    r = pl.program_id(1)

    def finish(j, ps, vs, l):
        o = functools.reduce(lambda a, c: a + c, [_dot(p.astype(_BF16), v) for p, v in zip(ps, vs)]) / l
        o_ref[0, :, j * 128:(j + 1) * 128] = _unstack_heads(o).astype(_BF16)

    def latent():
        u0 = jnp.clip(4 * r - 4, 0, rows - NA_WIN_ROWS)
        ws = pl.multiple_of(u0 * GRID_W, GRID_W)
        scores = []
        for j in range(4):
            sl = slice(j * 128, (j + 1) * 128)
            lhs = _stack_heads(q_ref[0, :, sl])
            scores.append((_nt(lhs, k_ref[0, pl.ds(ws, span), sl]), _nt(lhs, k_ref[0, s_len:s_len + l_ctx, sl])))
        for j in range(4):
            sl = slice(j * 128, (j + 1) * 128)
            ps, l = _softmax_parts([scores[j][0] + bias_ref[0, j], scores[j][1]])
            finish(j, ps, (v_ref[0, pl.ds(ws, span), sl], v_ref[0, s_len:s_len + l_ctx, sl]), l)

    def context():
        for j in range(4):
            sl = slice(j * 128, (j + 1) * 128)
            lhs = _stack_heads(q_ref[0, :, sl])
            ps, l = _softmax_parts([_nt(lhs, k_ref[0, s_len:s_len + l_ctx, sl])])
            finish(j, ps, (v_ref[0, s_len:s_len + l_ctx, sl],), l)

    if with_ctx:
        is_ctx = r == s_len // TM
        pl.when(jnp.logical_not(is_ctx))(latent)
        pl.when(is_ctx)(context)
    else:
        latent()


def _attn_n_call(q, k, v, bias_tbl, s_len, with_ctx):
    b, t, _ = q.shape
    n_r = s_len // TM
    nq = n_r + (1 if with_ctx else 0)
    span = NA_WIN_ROWS * GRID_W

    def case(ri):
        return jnp.where(ri == 0, 0, jnp.where(ri >= n_r - 1, 2, 1))

    return pl.pallas_call(
        functools.partial(_attn_n_body, s_len=s_len, l_ctx=t - s_len, with_ctx=with_ctx),
        grid=(b, nq),
        in_specs=[
            pl.BlockSpec((1, TM, BRANCH), lambda bi, ri: (bi, ri, 0)),
            pl.BlockSpec((1, t, BRANCH), lambda bi, ri: (bi, 0, 0)),
            pl.BlockSpec((1, t, BRANCH), lambda bi, ri: (bi, 0, 0)),
            pl.BlockSpec((1, 4, 2 * TM, span), lambda bi, ri: (case(ri), 0, 0, 0)),
        ],
        out_specs=pl.BlockSpec((1, TM, BRANCH), lambda bi, ri: (bi, ri, 0)),
        out_shape=jax.ShapeDtypeStruct((b, nq * TM, BRANCH), _BF16),
        compiler_params=_cparams(("arbitrary", "arbitrary"), VMEM_LIMIT),
        name="attn_neighbourhood",
    )(q, k, v, bias_tbl)


def _na_tables(rows):
    a = np.arange(4)[:, None, None, None]
    c = np.arange(GRID_W)[None, :, None, None]
    i = np.arange(NA_WIN_ROWS)[None, None, :, None]
    kc = np.arange(GRID_W)[None, None, None, :]
    c0 = np.clip(c - NA_COLS // 2, 0, GRID_W - NA_COLS)
    col_ok = (kc >= c0) & (kc < c0 + NA_COLS)
    cidx = kc - c + (NA_COLS - 1)
    row_ok = [i < NA_ROWS + 0 * a, (i >= a) & (i < a + NA_ROWS), i >= NA_WIN_ROWS - NA_ROWS + 0 * a]
    ridx = [i - a + 7, i - a + 3, i - a - 1]
    shape = (4, GRID_W, NA_WIN_ROWS, GRID_W)
    sel_r, sel_c, ok = [], [], []
    for cs in range(3):
        valid = np.broadcast_to(row_ok[cs] & col_ok, shape)
        ok.append(valid.reshape(4 * GRID_W, NA_WIN_ROWS * GRID_W))
        rsel = (np.broadcast_to(ridx[cs], (4, 1, NA_WIN_ROWS, 1))[:, 0, :, 0][..., None]
                == np.arange(2 * NA_ROWS - 1)[None, None, :])
        sel_r.append(rsel.astype(np.float32))
    csel = (np.broadcast_to(cidx, (1, GRID_W, 1, GRID_W))[0, :, 0, :][..., None]
            == np.arange(2 * NA_COLS - 1)[None, None, :]).astype(np.float32)
    return np.stack(sel_r), csel, np.stack(ok)


def _na_bias_table(rpb, rows):
    sel_r, sel_c, ok = _na_tables(rows)
    hi = lax.Precision.HIGHEST
    t = jnp.einsum("sair,hrg->shaig", jnp.asarray(sel_r), rpb.astype(_F32), precision=hi)
    t = jnp.einsum("shaig,ckg->shacik", t, jnp.asarray(sel_c), precision=hi)
    t = t.reshape(3, N_HEADS, 4 * GRID_W, NA_WIN_ROWS * GRID_W)
    t = jnp.where(jnp.asarray(ok)[:, None], t * LOG2_E, NEG_INF)
    return t.reshape(3, 4, 2 * TM, NA_WIN_ROWS * GRID_W)


def _route_rows(sel, sc):
    gs, gi1, gi2, gs1, gs2 = [], [], [], [], []
    for g in range(N_GROUPS):
        v = sel[4 * g:4 * g + 4]
        s = sc[4 * g:4 * g + 4]
        m1 = jnp.maximum(jnp.maximum(v[0], v[1]), jnp.maximum(v[2], v[3]))
        i1 = jnp.where(v[0] == m1, 0, jnp.where(v[1] == m1, 1, jnp.where(v[2] == m1, 2, 3)))
        rest = [jnp.where(i1 == k, NEG_INF, v[k]) for k in range(4)]
        m2 = jnp.maximum(jnp.maximum(rest[0], rest[1]), jnp.maximum(rest[2], rest[3]))
        i2 = jnp.where(rest[0] == m2, 0, jnp.where(rest[1] == m2, 1, jnp.where(rest[2] == m2, 2, 3)))

        def pick(idx, s=s):
            return jnp.where(idx == 0, s[0], jnp.where(idx == 1, s[1], jnp.where(idx == 2, s[2], s[3])))

        gs.append(m1 + m2)
        gi1.append(i1 + 4 * g)
        gi2.append(i2 + 4 * g)
        gs1.append(pick(i1))
        gs2.append(pick(i2))
    best = jnp.maximum(jnp.maximum(gs[0], gs[1]), jnp.maximum(gs[2], gs[3]))
    gsel = jnp.where(gs[0] == best, 0, jnp.where(gs[1] == best, 1, jnp.where(gs[2] == best, 2, 3)))

    def pickg(vals):
        return jnp.where(gsel == 0, vals[0], jnp.where(gsel == 1, vals[1], jnp.where(gsel == 2, vals[2], vals[3])))

    return pickg(gi1), pickg(gi2), pickg(gs1), pickg(gs2)


def _merge_body(oa_ref, ob_ref, on_ref, gate_ref, x_ref, xctx_ref, wb_ref, wo_ref, g1_ref, sh2_ref, sc2_ref, n2_ref,
                wrt_ref, rb_ref,
                x1_ref, h2_ref, bkt_ref, rank_ref, cnt_ref, carry_ref, *, n_lat_tiles):
    first = (pl.program_id(0) == 0) & (pl.program_id(1) == 0)
    x_res = jnp.where(pl.program_id(1) < n_lat_tiles, x_ref[0], xctx_ref[0])

    @pl.when(first)
    def _():
        carry_ref[...] = jnp.zeros_like(carry_ref)

    outs = (oa_ref, ob_ref, on_ref)
    y = None
    for n in range(3):
        gt = jax.nn.sigmoid(gate_ref[0, :, n * D_MODEL:(n + 1) * D_MODEL].astype(_F32))
        term = gt * _dot(outs[n][0], wb_ref[n])
        y = term if y is None else y + term
    z = _dot(y.astype(_BF16), wo_ref[...])
    x1 = x_res + g1_ref[0] * z
    x1_ref[0] = x1
    ms = jnp.mean(x1 * x1, axis=-1, keepdims=True)
    h2 = (x1 * lax.rsqrt(ms + EPS) * n2_ref[...]) * (1.0 + sc2_ref[0]) + sh2_ref[0]
    h2_ref[0, :, :D_MODEL] = h2

    hb = h2.astype(_BF16)
    hl = (h2 - hb.astype(_F32)).astype(_BF16)
    lt = (_dot(hb, wrt_ref[...]) + _dot(hl, wrt_ref[...])).T
    logits = lt[:N_EXPERTS] + lt[N_EXPERTS:2 * N_EXPERTS]
    scores = jax.nn.sigmoid(logits)
    sel = scores + rb_ref[...]
    sel_rows = [sel[e:e + 1, :] for e in range(N_EXPERTS)]
    sc_rows = [scores[e:e + 1, :] for e in range(N_EXPERTS)]
    e1, e2, s1, s2 = _route_rows(sel_rows, sc_rows)
    den = s1 + s2
    first_lo = e1 < e2
    e_lo = jnp.where(first_lo, e1, e2)
    e_hi = jnp.where(first_lo, e2, e1)
    w_lo = jnp.where(first_lo, s1, s2) / den
    w_hi = jnp.where(first_lo, s2, s1) / den
    a = e_lo & 3
    b = e_hi & 3
    bkt = (e_lo >> 2) * 6 + jnp.where(a == 0, 0, jnp.where(a == 1, 3, 5)) + (b - a - 1)
    bkt_ref[...] = bkt

    bid = lax.broadcasted_iota(jnp.int32, (BUCKET_ROWS, TM), 0)
    oh = bid == bkt
    tri = (lax.broadcasted_iota(jnp.int32, (TM, TM), 0) <= lax.broadcasted_iota(jnp.int32, (TM, TM), 1))
    cum = _dot(oh.astype(_BF16), tri.astype(_BF16))
    carry = carry_ref[:, 0:1]
    rank_ref[...] = jnp.sum(jnp.where(oh, carry + cum - 1.0, 0.0), axis=0, keepdims=True).astype(jnp.int32)
    new_carry = carry + cum[:, TM - 1:TM]
    carry_ref[...] = jnp.broadcast_to(new_carry, carry_ref.shape)
    cnt_ref[...] = jnp.broadcast_to(new_carry, cnt_ref.shape).astype(jnp.int32)

    lane = lax.broadcasted_iota(jnp.int32, (TM, 128), 1)

    def col(row):
        return jnp.concatenate([_col_from_row(row[:, h * 128:(h + 1) * 128]) for h in range(TM // 128)], axis=0)

    h2_ref[0, :, D_MODEL:] = jnp.where(lane == 0, col(w_lo), jnp.where(lane == 1, col(w_hi), 0.0))


def _merge_call(oa, ob, on, gate, x_lat, x_ctx, ctx_tile, wb, wo, g1, sh2, sc2, n2, wrt, rb, t_out,
                n_lat_tiles):
    b, _, d = x_lat.shape
    nb = g1.shape[0] - 1
    nt = t_out // TM
    n_tok = b * t_out

    def mod_idx(bi, i):
        return (jnp.where(i < n_lat_tiles, bi, nb), 0, 0)

    def tok(width):
        return pl.BlockSpec((1, TM, width), lambda bi, i: (bi, i, 0))

    def full(shape):
        return pl.BlockSpec(shape, lambda bi, i: (0,) * len(shape))

    lane_out = pl.BlockSpec((1, TM), lambda bi, i: (0, bi * nt + i))
    return pl.pallas_call(
        functools.partial(_merge_body, n_lat_tiles=n_lat_tiles),
        grid=(b, nt),
        in_specs=[tok(BRANCH), tok(BRANCH), tok(BRANCH), tok(3 * d), *_token_specs(n_lat_tiles, ctx_tile, d),
                  full((3, BRANCH, d)), full((d, d)),
                  pl.BlockSpec((1, 1, d), mod_idx), pl.BlockSpec((1, 1, d), mod_idx),
                  pl.BlockSpec((1, 1, d), mod_idx), full((1, d)),
                  full((d, 128)), full((N_EXPERTS, 1))],
        out_specs=[tok(d), tok(ROW_W), lane_out, lane_out, full((BUCKET_ROWS, 128))],
        out_shape=[jax.ShapeDtypeStruct((b, t_out, d), _F32), jax.ShapeDtypeStruct((b, t_out, ROW_W), _F32),
                   jax.ShapeDtypeStruct((1, n_tok), jnp.int32), jax.ShapeDtypeStruct((1, n_tok), jnp.int32),
                   jax.ShapeDtypeStruct((BUCKET_ROWS, 128), jnp.int32)],
        scratch_shapes=[pltpu.VMEM((BUCKET_ROWS, 128), _F32)],
        compiler_params=_cparams(("arbitrary", "arbitrary"), VMEM_LIMIT),
        name="merge_route",
    )(oa, ob, on, gate, x_lat, x_ctx, wb, wo, g1, sh2, sc2, n2, wrt, rb)


ROW_DMA_UNROLL = 8


def _dispatch_body(pad_start_ref, pad_n_ref, nv_ref, pos_ref, h_ref, xs_ref, zbuf, sem, zsem):
    @pl.when(pl.program_id(0) == 0)
    def _():
        zbuf[...] = jnp.zeros_like(zbuf)

        def zero_row(e, r):
            return pltpu.make_async_copy(zbuf.at[pl.ds(0, 1)], xs_ref.at[pl.ds(pad_start_ref[e] + r, 1)], zsem)

        def zero_tile(i):
            return pltpu.make_async_copy(zbuf, xs_ref.at[pl.ds(pl.multiple_of(i * TM_E, TM_E), TM_E)], zsem)

        n_tiles = xs_ref.shape[0] // TM_E
        for e in range(N_BUCKETS):
            lax.fori_loop(0, pad_n_ref[e], lambda r, c, e=e: (zero_row(e, r).start(), c)[1], 0)
        lax.fori_loop(nv_ref[0], n_tiles, lambda i, c: (zero_tile(i).start(), c)[1], 0)
        for e in range(N_BUCKETS):
            lax.fori_loop(0, pad_n_ref[e], lambda r, c, e=e: (zero_row(e, r).wait(), c)[1], 0)
        lax.fori_loop(nv_ref[0], n_tiles, lambda i, c: (zero_tile(i).wait(), c)[1], 0)

    def row_copy(t):
        return pltpu.make_async_copy(h_ref.at[pl.ds(t, 1)], xs_ref.at[pl.ds(pos_ref[0, 0, t], 1)], sem)

    rows = h_ref.shape[0]
    lax.fori_loop(0, rows, lambda t, c: (row_copy(t).start(), c)[1], 0, unroll=ROW_DMA_UNROLL)
    lax.fori_loop(0, rows, lambda t, c: (row_copy(t).wait(), c)[1], 0, unroll=ROW_DMA_UNROLL)


def _dispatch_call(pad_start, pad_n, n_valid, pos, h2, p_rows):
    n_tok, d = h2.shape
    rows = TM_D if n_tok % TM_D == 0 else TM
    n_tiles = n_tok // rows
    pos3 = pos.reshape(n_tiles, 1, rows)
    return pl.pallas_call(
        _dispatch_body,
        grid_spec=pltpu.PrefetchScalarGridSpec(
            num_scalar_prefetch=3,
            grid=(n_tiles,),
            in_specs=[pl.BlockSpec((1, 1, rows), lambda i, ps, pn, nv: (i, 0, 0), memory_space=pltpu.SMEM),
                      pl.BlockSpec((rows, d), lambda i, ps, pn, nv: (i, 0))],
            out_specs=pl.BlockSpec(memory_space=pl.ANY),
            scratch_shapes=[pltpu.VMEM((TM_E, d), h2.dtype), pltpu.SemaphoreType.DMA, pltpu.SemaphoreType.DMA],
        ),
        out_shape=jax.ShapeDtypeStruct((p_rows, d), h2.dtype),
        compiler_params=_cparams(("arbitrary",)),
        name="moe_dispatch",
    )(pad_start, pad_n, n_valid, pos3, h2)


def _moe_body(elo_ref, ehi_ref, nv_ref, xs_ref, wg_lo, wu_lo, wd_lo, wg_hi, wu_hi, wd_hi, ys_ref):
    del elo_ref, ehi_ref
    live = pl.program_id(0) < nv_ref[0]

    @pl.when(live)
    def _():
        xb = xs_ref[:, :D_MODEL].astype(_BF16)
        wts = xs_ref[:, D_MODEL:]

        def expert(wg_ref, wu_ref, wd_ref):
            gate = _dot(xb, wg_ref[0].astype(_BF16))
            up = _dot(xb, wu_ref[0].astype(_BF16))
            act = (gate * jax.nn.sigmoid(gate)) * up
            return _dot(act.astype(_BF16), wd_ref[0].astype(_BF16))

        ys_ref[...] = wts[:, 0:1] * expert(wg_lo, wu_lo, wd_lo) + wts[:, 1:2] * expert(wg_hi, wu_hi, wd_hi)

    @pl.when(jnp.logical_not(live))
    def _():
        ys_ref[...] = jnp.zeros_like(ys_ref)


def _moe_call(tile_elo, tile_ehi, n_valid, xs, wg, wu, wd, layer):
    p_rows = xs.shape[0]
    d = D_MODEL
    n_tiles = p_rows // TM_E

    def row_idx(i, elo, ehi, nv):
        return (jnp.minimum(i, nv[0] - 1), 0)

    def lo_idx(i, elo, ehi, nv):
        return (layer, elo[i], 0, 0)

    def hi_idx(i, elo, ehi, nv):
        return (layer, ehi[i], 0, 0)

    def w_specs(idx):
        return [pl.BlockSpec((None, 1, d, D_FF), idx), pl.BlockSpec((None, 1, d, D_FF), idx),
                pl.BlockSpec((None, 1, D_FF, d), idx)]

    return pl.pallas_call(
        _moe_body,
        grid_spec=pltpu.PrefetchScalarGridSpec(
            num_scalar_prefetch=3,
            grid=(n_tiles,),
            in_specs=[pl.BlockSpec((TM_E, ROW_W), row_idx), *w_specs(lo_idx), *w_specs(hi_idx)],
            out_specs=pl.BlockSpec((TM_E, d), lambda i, elo, ehi, nv: (i, 0)),
        ),
        out_shape=jax.ShapeDtypeStruct((p_rows, d), _F32),
        compiler_params=_cparams(("arbitrary",), VMEM_LIMIT),
        name="moe_experts",
    )(tile_elo, tile_ehi, n_valid, xs, wg, wu, wd, wg, wu, wd)


def _combine_body(pos0_ref, posn_ref, x1_ref, g2_ref, fn_ref, ys_ref, o_ref, gbuf, sems):
    step = pl.program_id(0)
    y, issue_next, finish = _moe_rows_pipeline(step, pl.num_programs(0) - 1, pos0_ref, posn_ref, ys_ref,
                                               gbuf, sems)
    x2 = x1_ref[...] + g2_ref[0] * y
    issue_next()
    ms = jnp.mean(x2 * x2, axis=-1, keepdims=True)
    o_ref[...] = x2 * lax.rsqrt(ms + EPS) * fn_ref[...]
    finish()


def _combine_call(pos3, x1, g2, fnorm, ys, t_out, n_lat_tiles):
    n_tok, d = x1.shape
    nb = g2.shape[0] - 1
    nt = t_out // TM
    n_tiles = n_tok // TM

    def mod_idx(i):
        return (jnp.where(i % nt < n_lat_tiles, i // nt, nb), 0, 0)

    smem_tile = functools.partial(pl.BlockSpec, (1, 1, TM), memory_space=pltpu.SMEM)
    return pl.pallas_call(
        _combine_body,
        grid=(n_tiles,),
        in_specs=[smem_tile(lambda i: (0, 0, 0)),
                  smem_tile(lambda i: (jnp.minimum(i + 1, n_tiles - 1), 0, 0)),
                  pl.BlockSpec((TM, d), lambda i: (i, 0)),
                  pl.BlockSpec((1, 1, d), mod_idx),
                  pl.BlockSpec((1, d), lambda i: (0, 0)),
                  pl.BlockSpec(memory_space=pl.ANY)],
        out_specs=pl.BlockSpec((TM, d), lambda i: (i, 0)),
        out_shape=jax.ShapeDtypeStruct((n_tok, d), _F32),
        scratch_shapes=[pltpu.VMEM((2, TM, d), _F32), pltpu.SemaphoreType.DMA((2,))],
        compiler_params=_cparams(("arbitrary",)),
        name="moe_combine",
    )(pos3, pos3, x1, g2, fnorm, ys)


def _rope_tables(s_len, l_ctx):
    t = jnp.arange(s_len, dtype=jnp.int32)
    row = (t // GRID_W).astype(_F32)
    col = (t % GRID_W).astype(_F32)
    half = HEAD_DIM // 2
    inv = ROPE_THETA ** (-jnp.arange(0, half, 2, dtype=_F32) / half)
    ar = row[:, None] * inv
    ac = col[:, None] * inv
    cos64 = jnp.concatenate([jnp.cos(ar), jnp.cos(ar), jnp.cos(ac), jnp.cos(ac)], axis=1)
    sin64 = jnp.concatenate([-jnp.sin(ar), jnp.sin(ar), -jnp.sin(ac), jnp.sin(ac)], axis=1)
    cos_t = jnp.concatenate([jnp.tile(cos64, (1, 2)), jnp.ones((l_ctx, 128), _F32)], axis=0)
    sin_t = jnp.concatenate([jnp.tile(sin64, (1, 2)), jnp.zeros((l_ctx, 128), _F32)], axis=0)
    return cos_t, sin_t


def _block_diag_ones():
    r = np.arange(BRANCH)
    return jnp.asarray((r[:, None] // HEAD_DIM) == (r[None, :] // HEAD_DIM), dtype=_BF16)


def kernel(x, c, ctx, c_ctx, w_mod, b_mod, norm1, norm2, w_in, q_norm_a, k_norm_a, sink_b, rpb_n, w_branch, w_out,
           w_router, router_bias, w_e_gate, w_e_up, w_e_down, final_norm):
    b, s_len, d = x.shape
    l_ctx = ctx.shape[1]
    depth = w_in.shape[0]
    t_all = s_len + l_ctx
    rows = s_len // GRID_W
    assert d == D_MODEL and l_ctx == TM and s_len % TM == 0 and b + 1 <= 16
    assert s_len >= TM + 2 * WINDOW and rows >= NA_WIN_ROWS and rows % 4 == 0
    n_lat_tiles = s_len // TM

    cos_t, sin_t = _rope_tables(s_len, l_ctx)
    bd = _block_diag_ones()
    cvec = jnp.zeros((16, d), _F32).at[:b].set(c).at[b].set(c_ctx)
    mods = _mod_call(cvec, w_mod, b_mod)[:, :b + 1].reshape(depth, b + 1, 6, 1, d)

    wr = w_router.astype(_F32)
    wr_hi = wr.astype(_BF16)
    wr_lo = (wr - wr_hi.astype(_F32)).astype(_BF16)
    wrt = jnp.concatenate([wr_hi, wr_lo, jnp.zeros((d, 128 - 2 * N_EXPERTS), _BF16)], axis=1)
    rb = router_bias.astype(_F32).reshape(N_EXPERTS, 1)

    x_lat, x_ctx, ctx_tile = x, ctx, 0
    pending = None
    for l in range(depth):
        last = l == depth - 1
        sh1, sc1, g1, sh2, sc2, g2 = [mods[l, :, n] for n in range(6)]
        w_f = w_in[l]
        w_l = jnp.concatenate([_pair_heads(w_f[:, OFF_QA:OFF_KA], 1), w_f[:, OFF_KA:OFF_QB],
                               _pair_heads(w_f[:, OFF_QB:OFF_KB], 1), w_f[:, OFF_KB:]], axis=1).astype(_BF16)
        gq = (jnp.tile(q_norm_a[l].astype(_F32), N_HEADS) * Q_SCALE).reshape(1, BRANCH)
        gk = jnp.tile(k_norm_a[l].astype(_F32), 2).reshape(1, KV_A)
        outs = _inproj_call(x_lat, x_ctx, ctx_tile, n_lat_tiles, norm1[l].reshape(1, d), sh1, sc1, w_l, cos_t, sin_t,
                            gq, gk, bd, combine=pending)
        qa, ka, va, qb, kb, vb, qn, kn, vn, gate = outs[:10]
        if pending is not None:
            x_lat = x_ctx = outs[10]

        sink = sink_b[l].astype(_F32) * LOG2_E
        oa = _attn_a_call(qa, ka, va, s_len, not last)
        ob = _attn_b_call(qb, kb, vb, sink, s_len, not last)
        on = _attn_n_call(qn, kn, vn, _na_bias_table(rpb_n[l], rows), s_len, not last)

        t_out = s_len if last else t_all
        wb = jnp.stack([_pair_heads(w_branch[l, 0], 0), _pair_heads(w_branch[l, 1], 0),
                        w_branch[l, 2]]).astype(_BF16)
        x1, h2w, bkt, rank, cnt = _merge_call(
            oa, ob, on, gate, x_lat, x_ctx, ctx_tile, wb, w_out[l].astype(_BF16), g1, sh2, sc2,
            norm2[l].reshape(1, d), wrt, rb, t_out, n_lat_tiles)

        n_tok = b * t_out
        counts = cnt[:N_BUCKETS, 0]
        padded = ((counts + TM_E - 1) // TM_E) * TM_E
        ends = jnp.cumsum(padded)
        offs = ends - padded
        pos = rank[0] + jnp.sum(jnp.where(bkt == jnp.arange(N_BUCKETS)[:, None], offs[:, None], 0), axis=0)
        pos3 = pos.reshape(n_tok // TM, 1, TM)
        p_rows = n_tok + N_BUCKETS * TM_E
        n_tiles = p_rows // TM_E
        n_valid = (ends[-1] // TM_E).astype(jnp.int32).reshape(1)
        tile_start = jnp.arange(n_tiles, dtype=jnp.int32) * TM_E
        tile_bucket = jnp.minimum(jnp.sum(tile_start[:, None] >= ends[None, :], axis=1), N_BUCKETS - 1)
        tile_group = (tile_bucket // 6) * EXPERTS_PER_GROUP
        tile_elo = (tile_group + jnp.asarray(_PAIR_LO)[tile_bucket % 6]).astype(jnp.int32)
        tile_ehi = (tile_group + jnp.asarray(_PAIR_HI)[tile_bucket % 6]).astype(jnp.int32)

        pad_start = (offs + counts).astype(jnp.int32)
        pad_n = (padded - counts).astype(jnp.int32)
        xs = _dispatch_call(pad_start, pad_n, n_valid, pos, h2w.reshape(n_tok, ROW_W), p_rows)
        ys = _moe_call(tile_elo, tile_ehi, n_valid, xs, w_e_gate, w_e_up, w_e_down, l)
        if last:
            out = _combine_call(pos3, x1.reshape(n_tok, d), g2, final_norm.reshape(1, d), ys, t_out, n_lat_tiles)
            return out.reshape(b, t_out, d)
        pending = (pos3, g2, ys)
        x_lat = x_ctx = x1
        ctx_tile = n_lat_tiles
```

```python
import functools

import numpy as np
import jax
import jax.numpy as jnp
from jax import lax
from jax.experimental import pallas as pl
from jax.experimental.pallas import tpu as pltpu

D_MODEL = 1024
HEAD_DIM = 64
GRID_W = 64
N_HEADS = 8
WINDOW = 128
NA_ROWS = 8
NA_COLS = 16
ROPE_THETA = 10000.0
N_EXPERTS = 16
N_GROUPS = 4
EXPERTS_PER_GROUP = N_EXPERTS // N_GROUPS
D_FF = 512
EPS = 1e-6
NEG_INF = -1e30
ATTN_SCALE = HEAD_DIM ** -0.5
LOG2_E = 1.4426950408889634
Q_SCALE = ATTN_SCALE * LOG2_E
BRANCH = N_HEADS * HEAD_DIM
KV_A = 2 * HEAD_DIM
IN_COLS = 6144

OFF_QA, OFF_KA, OFF_VA = 0, 512, 640
OFF_QB, OFF_KB, OFF_VB = 768, 1280, 1408
OFF_QN, OFF_KN, OFF_VN = 1536, 2048, 2560
OFF_GATE = 3072

TM = 256
TM_E = 512
TM_D = 1024
TM_C = 512
N_BUCKETS = N_GROUPS * 6
BUCKET_ROWS = 32
ROW_W = D_MODEL + 128
_PAIR_LO = np.array([0, 0, 0, 1, 1, 2])
_PAIR_HI = np.array([1, 2, 3, 2, 3, 3])
NA_WIN_ROWS = 12
VMEM_LIMIT = 56 * 1024 * 1024

_F32 = jnp.float32
_BF16 = jnp.bfloat16

def _pair_heads(w, axis):
    shape = w.shape
    split = shape[:axis] + (2, 4, HEAD_DIM) + shape[axis + 1:]
    return jnp.swapaxes(w.reshape(split), axis, axis + 1).reshape(shape)


def _nt(a, b):
    return lax.dot_general(a, b, (((1,), (1,)), ((), ())), preferred_element_type=_F32)


def _dot(a, b):
    return jnp.dot(a, b, preferred_element_type=_F32)


def _cparams(sem, vmem=None):
    return pltpu.CompilerParams(dimension_semantics=sem, vmem_limit_bytes=vmem)


def _mod_body(c_ref, w_ref, b_ref, o_ref):
    cv = c_ref[...]
    a = (cv * jax.nn.sigmoid(cv)).astype(_BF16)
    o_ref[0] = _dot(a, w_ref[0].astype(_BF16)) + b_ref[0]


def _mod_call(cvec, w_mod, b_mod):
    depth, d, n = w_mod.shape
    tn = 1536
    return pl.pallas_call(
        _mod_body,
        grid=(depth, n // tn),
        in_specs=[
            pl.BlockSpec((16, d), lambda l, j: (0, 0)),
            pl.BlockSpec((1, d, tn), lambda l, j: (l, 0, j)),
            pl.BlockSpec((1, 1, tn), lambda l, j: (l, 0, j)),
        ],
        out_specs=pl.BlockSpec((1, 16, tn), lambda l, j: (l, 0, j)),
        out_shape=jax.ShapeDtypeStruct((depth, 16, n), _F32),
        compiler_params=_cparams(("arbitrary", "arbitrary")),
        name="mod_vectors",
    )(cvec, w_mod, b_mod.reshape(depth, 1, n))


def _rope(v, c, s):
    w = v.shape[1]
    lane = lax.broadcasted_iota(jnp.int32, v.shape, 1)
    first = (lane & 16) == 0
    sw = jnp.where(first, pltpu.roll(v, w - 16, 1), pltpu.roll(v, 16, 1))
    return v * c + sw * s


def _inproj_body(x_ref, xctx_ref, *refs, n_lat_tiles):
    x = jnp.where(pl.program_id(1) < n_lat_tiles, x_ref[0], xctx_ref[0])
    _inproj_core(x, *refs)


def _col_from_row(row):
    blk = jnp.broadcast_to(row, (128, 128))
    return blk.T[:, 0:1]


def _moe_rows_pipeline(step, last_step, pos0_ref, posn_ref, ys_ref, gbuf, sems):
    def row_copy(pos_ref, slot, t):
        return pltpu.make_async_copy(ys_ref.at[pl.ds(pos_ref[0, 0, t], 1)], gbuf.at[slot, pl.ds(t, 1)],
                                     sems.at[slot])

    rows = gbuf.shape[1]

    def gather(pos_ref, slot):
        for t in range(rows):
            row_copy(pos_ref, slot, t).start()

    def drain(slot):
        for t in range(rows):
            row_copy(pos0_ref, slot, t).wait()

    pl.when(step == 0)(lambda: gather(pos0_ref, 0))
    slot = step % 2
    drain(slot)
    y = gbuf[slot]
    return (y, lambda: gather(posn_ref, (step + 1) % 2),
            lambda: pl.when(step == last_step)(lambda: drain((step + 1) % 2)))


def _inproj_combine_body(pos0_ref, posn_ref, x_ref, xctx_ref, g2_ref, ys_ref, *refs, n_lat_tiles):
    core_refs, x2_ref, gbuf, sems = refs[:-3], refs[-3], refs[-2], refs[-1]
    nt = pl.num_programs(1)
    step = pl.program_id(0) * nt + pl.program_id(1)
    y, issue_next, finish = _moe_rows_pipeline(step, pl.num_programs(0) * nt - 1, pos0_ref, posn_ref,
                                               ys_ref, gbuf, sems)
    x1 = jnp.where(pl.program_id(1) < n_lat_tiles, x_ref[0], xctx_ref[0])
    x = x1 + g2_ref[0] * y
    x2_ref[0] = x
    issue_next()
    _inproj_core(x, *core_refs)
    finish()


def _inproj_core(x, g_ref, sh_ref, sc_ref, w_ref, cos_ref, sin_ref, gq_ref, gk_ref, bd_ref,
                 qa_ref, ka_ref, va_ref, qb_ref, kb_ref, vb_ref, qn_ref, kn_ref, vn_ref, gate_ref):
    ms = jnp.mean(x * x, axis=-1, keepdims=True)
    h = (x * lax.rsqrt(ms + EPS) * g_ref[...]) * (1.0 + sc_ref[0]) + sh_ref[0]
    hb = h.astype(_BF16)

    def seg(off, width):
        return _dot(hb, w_ref[:, off:off + width])

    c1 = cos_ref[...]
    s1 = sin_ref[...]
    c4 = jnp.concatenate([c1] * 4, axis=1)
    s4 = jnp.concatenate([s1] * 4, axis=1)
    bd = bd_ref[...]

    qa = seg(OFF_QA, BRANCH)
    ssq = _dot((qa * qa).astype(_BF16), bd)
    qa = qa * lax.rsqrt(ssq * (1.0 / HEAD_DIM) + EPS) * gq_ref[...]
    qa_ref[0] = _rope(qa, c4, s4).astype(_BF16)
    ka = seg(OFF_KA, KV_A)
    ssk = _dot((ka * ka).astype(_BF16), bd_ref[:KV_A, :KV_A])
    ka = ka * lax.rsqrt(ssk * (1.0 / HEAD_DIM) + EPS) * gk_ref[...]
    ka_ref[0] = _rope(ka, c1, s1).astype(_BF16)
    va_ref[0] = seg(OFF_VA, KV_A).astype(_BF16)
    qb_ref[0] = _rope(seg(OFF_QB, BRANCH) * Q_SCALE, c4, s4).astype(_BF16)
    kb_ref[0] = _rope(seg(OFF_KB, KV_A), c1, s1).astype(_BF16)
    vb_ref[0] = seg(OFF_VB, KV_A).astype(_BF16)
    qn_ref[0] = (seg(OFF_QN, BRANCH) * Q_SCALE).astype(_BF16)
    kn_ref[0] = seg(OFF_KN, BRANCH).astype(_BF16)
    vn_ref[0] = seg(OFF_VN, BRANCH).astype(_BF16)
    for n in range(3):
        gate_ref[0, :, n * D_MODEL:(n + 1) * D_MODEL] = seg(OFF_GATE + n * D_MODEL, D_MODEL).astype(_BF16)


def _token_specs(n_lat_tiles, ctx_tile, d):
    lat = pl.BlockSpec((1, TM, d), lambda bi, i: (bi, jnp.minimum(i, n_lat_tiles - 1), 0))
    ctx = pl.BlockSpec((1, TM, d), lambda bi, i: (bi, ctx_tile, 0))
    return lat, ctx


def _inproj_call(x_lat, x_ctx, ctx_tile, n_lat_tiles, g1n, sh, sc, w, cos_t, sin_t, gq, gk, bd, combine=None):
    b, _, d = x_lat.shape
    nb = sh.shape[0] - 1
    nt = n_lat_tiles + 1
    t = nt * TM

    def mod_idx(bi, i):
        return (jnp.where(i < n_lat_tiles, bi, nb), 0, 0)

    def tok(width):
        return pl.BlockSpec((1, TM, width), lambda bi, i: (bi, i, 0))

    widths = [BRANCH, KV_A, KV_A, BRANCH, KV_A, KV_A, BRANCH, BRANCH, BRANCH, 3 * D_MODEL]
    core_specs = [
        pl.BlockSpec((1, d), lambda bi, i: (0, 0)),
        pl.BlockSpec((1, 1, d), mod_idx),
        pl.BlockSpec((1, 1, d), mod_idx),
        pl.BlockSpec((d, IN_COLS), lambda bi, i: (0, 0)),
        pl.BlockSpec((TM, 128), lambda bi, i: (i, 0)),
        pl.BlockSpec((TM, 128), lambda bi, i: (i, 0)),
        pl.BlockSpec((1, BRANCH), lambda bi, i: (0, 0)),
        pl.BlockSpec((1, KV_A), lambda bi, i: (0, 0)),
        pl.BlockSpec((BRANCH, BRANCH), lambda bi, i: (0, 0)),
    ]
    core_args = (g1n, sh, sc, w, cos_t, sin_t, gq, gk, bd)
    out_specs = [tok(wd) for wd in widths]
    out_shape = [jax.ShapeDtypeStruct((b, t, wd), _BF16) for wd in widths]
    if combine is None:
        return pl.pallas_call(
            functools.partial(_inproj_body, n_lat_tiles=n_lat_tiles),
            grid=(b, nt),
            in_specs=[*_token_specs(n_lat_tiles, ctx_tile, d), *core_specs],
            out_specs=out_specs,
            out_shape=out_shape,
            compiler_params=_cparams(("arbitrary", "arbitrary"), VMEM_LIMIT),
            name="inproj",
        )(x_lat, x_ctx, *core_args)

    pos3, g2, ys = combine
    last_tile = b * nt - 1
    smem_tile = functools.partial(pl.BlockSpec, (1, 1, TM), memory_space=pltpu.SMEM)
    return pl.pallas_call(
        functools.partial(_inproj_combine_body, n_lat_tiles=n_lat_tiles),
        grid=(b, nt),
        in_specs=[
            smem_tile(lambda bi, i: (0, 0, 0)),
            smem_tile(lambda bi, i: (jnp.minimum(bi * nt + i + 1, last_tile), 0, 0)),
            *_token_specs(n_lat_tiles, ctx_tile, d),
            pl.BlockSpec((1, 1, d), mod_idx),
            pl.BlockSpec(memory_space=pl.ANY),
            *core_specs,
        ],
        out_specs=out_specs + [tok(d)],
        out_shape=out_shape + [jax.ShapeDtypeStruct((b, t, d), _F32)],
        scratch_shapes=[pltpu.VMEM((2, TM, d), _F32), pltpu.SemaphoreType.DMA((2,))],
        compiler_params=_cparams(("arbitrary", "arbitrary"), VMEM_LIMIT),
        name="inproj_combine",
    )(pos3, pos3, x_lat, x_ctx, g2, ys, *core_args)


def _stack_heads(q):
    lane = lax.broadcasted_iota(jnp.int32, q.shape, 1)
    lo = lane < HEAD_DIM
    zero = jnp.zeros_like(q)
    return jnp.concatenate([jnp.where(lo, q, zero), jnp.where(lo, zero, q)], axis=0)


def _unstack_heads(o):
    half = o.shape[0] // 2
    lane = lax.broadcasted_iota(jnp.int32, (half, o.shape[1]), 1)
    return jnp.where(lane < HEAD_DIM, o[:half], o[half:])


KA_CHUNK_MAX = 2048
KA_GROUPS = 4


def _flash_update(lhs, kc, vc, m, l128, acc):
    s = _nt(lhs, kc)
    tiles = [s[:, t * 128:(t + 1) * 128] for t in range(s.shape[1] // 128)]
    m_new = jnp.maximum(m, jnp.max(functools.reduce(jnp.maximum, tiles), axis=1, keepdims=True))
    alpha = jnp.exp2(m - m_new)
    ps = [jnp.exp2(t - m_new) for t in tiles]
    l128 = alpha * l128 + functools.reduce(lambda a, c: a + c, ps)
    p = jnp.concatenate([x.astype(_BF16) for x in ps], axis=1)
    acc = alpha * acc + _dot(p, vc)
    return m_new, l128, acc


def _attn_a_body(q_ref, k_ref, v_ref, o_ref, *, s_len, l_ctx, with_ctx):
    def run(chunks):
        for g in range(KA_GROUPS):
            sl = slice(g * 128, (g + 1) * 128)
            lhs = _stack_heads(q_ref[0, :, sl])
            m = jnp.full((2 * TM, 128), NEG_INF, _F32)
            l128 = jnp.zeros((2 * TM, 128), _F32)
            acc = jnp.zeros((2 * TM, 128), _F32)
            for start, size in chunks:
                m, l128, acc = _flash_update(lhs, k_ref[0, start:start + size, :], v_ref[0, start:start + size, :],
                                             m, l128, acc)
            l = jnp.sum(l128, axis=1, keepdims=True)
            o_ref[0, :, sl] = _unstack_heads(acc / l).astype(_BF16)

    ctx_chunk = (s_len, l_ctx)
    chunk = KA_CHUNK_MAX
    while s_len % chunk:
        chunk //= 2
    lat_chunks = [(c * chunk, chunk) for c in range(s_len // chunk)] + [ctx_chunk]
    if with_ctx:
        is_ctx = pl.program_id(1) == s_len // TM
        pl.when(jnp.logical_not(is_ctx))(lambda: run(lat_chunks))
        pl.when(is_ctx)(lambda: run([ctx_chunk]))
    else:
        run(lat_chunks)


def _attn_a_call(q, k, v, s_len, with_ctx):
    b, t, _ = q.shape
    nq = s_len // TM + (1 if with_ctx else 0)
    return pl.pallas_call(
        functools.partial(_attn_a_body, s_len=s_len, l_ctx=t - s_len, with_ctx=with_ctx),
        grid=(b, nq, 4 // KA_GROUPS),
        in_specs=[
            pl.BlockSpec((1, TM, 128 * KA_GROUPS), lambda bi, i, j: (bi, i, j)),
            pl.BlockSpec((1, t, 128), lambda bi, i, j: (bi, 0, 0)),
            pl.BlockSpec((1, t, 128), lambda bi, i, j: (bi, 0, 0)),
        ],
        out_specs=pl.BlockSpec((1, TM, 128 * KA_GROUPS), lambda bi, i, j: (bi, i, j)),
        out_shape=jax.ShapeDtypeStruct((b, nq * TM, BRANCH), _BF16),
        compiler_params=_cparams(("arbitrary", "arbitrary", "arbitrary"), VMEM_LIMIT),
        name="attn_global",
    )(q, k, v)


def _softmax_parts(parts, extra=None):
    m = functools.reduce(jnp.maximum, [jnp.max(s, axis=1, keepdims=True) for s in parts])
    if extra is not None:
        m = jnp.maximum(m, extra)
    ps = [jnp.exp2(s - m) for s in parts]
    l = functools.reduce(lambda a, c: a + c, [jnp.sum(p, axis=1, keepdims=True) for p in ps])
    if extra is not None:
        l = l + jnp.exp2(extra - m)
    return ps, l


def _attn_b_body(sink_ref, q_ref, k_ref, v_ref, o_ref, *, s_len, l_ctx, with_ctx):
    span = TM + 2 * WINDOW
    i = pl.program_id(1)
    is_hi = lax.broadcasted_iota(jnp.int32, (2 * TM, 1), 0) >= TM

    def finish(j, ps, vs, l):
        o = functools.reduce(lambda a, c: a + c, [_dot(p.astype(_BF16), v) for p, v in zip(ps, vs)]) / l
        o_ref[0, :, j * 128:(j + 1) * 128] = _unstack_heads(o).astype(_BF16)

    def latent():
        start = i * TM
        ws = pl.multiple_of(jnp.clip(start - WINDOW, 0, s_len - span), WINDOW)
        kw = k_ref[0, pl.ds(ws, span), :]
        vw = v_ref[0, pl.ds(ws, span), :]
        kc = k_ref[0, s_len:s_len + l_ctx, :]
        vc = v_ref[0, s_len:s_len + l_ctx, :]
        row = lax.broadcasted_iota(jnp.int32, (2 * TM, span), 0)
        qpos = start + jnp.where(row >= TM, row - TM, row)
        kpos = ws + lax.broadcasted_iota(jnp.int32, (2 * TM, span), 1)
        band = jnp.abs(kpos - qpos) <= WINDOW
        scores = []
        for j in range(4):
            lhs = _stack_heads(q_ref[0, :, j * 128:(j + 1) * 128])
            scores.append((_nt(lhs, kw), _nt(lhs, kc)))
        for j in range(4):
            s_loc = jnp.where(band, scores[j][0], NEG_INF)
            sink = jnp.where(is_hi, sink_ref[4 + j], sink_ref[j])
            ps, l = _softmax_parts([s_loc, scores[j][1]], sink)
            finish(j, ps, (vw, vc), l)

    def context():
        kc = k_ref[0, s_len:s_len + l_ctx, :]
        vc = v_ref[0, s_len:s_len + l_ctx, :]
        for j in range(4):
            lhs = _stack_heads(q_ref[0, :, j * 128:(j + 1) * 128])
            sink = jnp.where(is_hi, sink_ref[4 + j], sink_ref[j])
            ps, l = _softmax_parts([_nt(lhs, kc)], sink)
            finish(j, ps, (vc,), l)

    if with_ctx:
        is_ctx = i == s_len // TM
        pl.when(jnp.logical_not(is_ctx))(latent)
        pl.when(is_ctx)(context)
    else:
        latent()


def _attn_b_call(q, k, v, sink, s_len, with_ctx):
    b, t, _ = q.shape
    nq = s_len // TM + (1 if with_ctx else 0)
    return pl.pallas_call(
        functools.partial(_attn_b_body, s_len=s_len, l_ctx=t - s_len, with_ctx=with_ctx),
        grid=(b, nq),
        in_specs=[
            pl.BlockSpec(memory_space=pltpu.SMEM),
            pl.BlockSpec((1, TM, BRANCH), lambda bi, i: (bi, i, 0)),
            pl.BlockSpec((1, t, 128), lambda bi, i: (bi, 0, 0)),
            pl.BlockSpec((1, t, 128), lambda bi, i: (bi, 0, 0)),
        ],
        out_specs=pl.BlockSpec((1, TM, BRANCH), lambda bi, i: (bi, i, 0)),
        out_shape=jax.ShapeDtypeStruct((b, nq * TM, BRANCH), _BF16),
        compiler_params=_cparams(("arbitrary", "arbitrary"), VMEM_LIMIT),
        name="attn_window",
    )(sink, q, k, v)


def _attn_n_body(q_ref, k_ref, v_ref, bias_ref, o_ref, *, s_len, l_ctx, with_ctx):
    rows = s_len // GRID_W
    span = NA_WIN_ROWS * GRID_W
    r = pl.program_id(1)

    def finish(j, ps, vs, l):
        o = functools.reduce(lambda a, c: a + c, [_dot(p.astype(_BF16), v) for p, v in zip(ps, vs)]) / l
        o_ref[0, :, j * 128:(j + 1) * 128] = _unstack_heads(o).astype(_BF16)

    def latent():
        u0 = jnp.clip(4 * r - 4, 0, rows - NA_WIN_ROWS)
        ws = pl.multiple_of(u0 * GRID_W, GRID_W)
        scores = []
        for j in range(4):
            sl = slice(j * 128, (j + 1) * 128)
            lhs = _stack_heads(q_ref[0, :, sl])
            scores.append((_nt(lhs, k_ref[0, pl.ds(ws, span), sl]), _nt(lhs, k_ref[0, s_len:s_len + l_ctx, sl])))
        for j in range(4):
            sl = slice(j * 128, (j + 1) * 128)
            ps, l = _softmax_parts([scores[j][0] + bias_ref[0, j].astype(_F32), scores[j][1]])
            finish(j, ps, (v_ref[0, pl.ds(ws, span), sl], v_ref[0, s_len:s_len + l_ctx, sl]), l)

    def context():
        for j in range(4):
            sl = slice(j * 128, (j + 1) * 128)
            lhs = _stack_heads(q_ref[0, :, sl])
            ps, l = _softmax_parts([_nt(lhs, k_ref[0, s_len:s_len + l_ctx, sl])])
            finish(j, ps, (v_ref[0, s_len:s_len + l_ctx, sl],), l)

    if with_ctx:
        is_ctx = r == s_len // TM
        pl.when(jnp.logical_not(is_ctx))(latent)
        pl.when(is_ctx)(context)
    else:
        latent()


def _attn_n_call(q, k, v, bias_tbl, s_len, with_ctx):
    b, t, _ = q.shape
    n_r = s_len // TM
    nq = n_r + (1 if with_ctx else 0)
    span = NA_WIN_ROWS * GRID_W

    def case(ri):
        return jnp.where(ri == 0, 0, jnp.where(ri >= n_r - 1, 2, 1))

    return pl.pallas_call(
        functools.partial(_attn_n_body, s_len=s_len, l_ctx=t - s_len, with_ctx=with_ctx),
        grid=(b, nq),
        in_specs=[
            pl.BlockSpec((1, TM, BRANCH), lambda bi, ri: (bi, ri, 0)),
            pl.BlockSpec((1, t, BRANCH), lambda bi, ri: (bi, 0, 0)),
            pl.BlockSpec((1, t, BRANCH), lambda bi, ri: (bi, 0, 0)),
            pl.BlockSpec((1, 4, 2 * TM, span), lambda bi, ri: (case(ri), 0, 0, 0)),
        ],
        out_specs=pl.BlockSpec((1, TM, BRANCH), lambda bi, ri: (bi, ri, 0)),
        out_shape=jax.ShapeDtypeStruct((b, nq * TM, BRANCH), _BF16),
        compiler_params=_cparams(("arbitrary", "arbitrary"), VMEM_LIMIT),
        name="attn_neighbourhood",
    )(q, k, v, bias_tbl)


def _na_tables(rows):
    a = np.arange(4)[:, None, None, None]
    c = np.arange(GRID_W)[None, :, None, None]
    i = np.arange(NA_WIN_ROWS)[None, None, :, None]
    kc = np.arange(GRID_W)[None, None, None, :]
    c0 = np.clip(c - NA_COLS // 2, 0, GRID_W - NA_COLS)
    col_ok = (kc >= c0) & (kc < c0 + NA_COLS)
    cidx = kc - c + (NA_COLS - 1)
    row_ok = [i < NA_ROWS + 0 * a, (i >= a) & (i < a + NA_ROWS), i >= NA_WIN_ROWS - NA_ROWS + 0 * a]
    ridx = [i - a + 7, i - a + 3, i - a - 1]
    shape = (4, GRID_W, NA_WIN_ROWS, GRID_W)
    sel_r, sel_c, ok = [], [], []
    for cs in range(3):
        valid = np.broadcast_to(row_ok[cs] & col_ok, shape)
        ok.append(valid.reshape(4 * GRID_W, NA_WIN_ROWS * GRID_W))
        rsel = (np.broadcast_to(ridx[cs], (4, 1, NA_WIN_ROWS, 1))[:, 0, :, 0][..., None]
                == np.arange(2 * NA_ROWS - 1)[None, None, :])
        sel_r.append(rsel.astype(np.float32))
    csel = (np.broadcast_to(cidx, (1, GRID_W, 1, GRID_W))[0, :, 0, :][..., None]
            == np.arange(2 * NA_COLS - 1)[None, None, :]).astype(np.float32)
    return np.stack(sel_r), csel, np.stack(ok)


def _na_bias_table(rpb, rows):
    sel_r, sel_c, ok = _na_tables(rows)
    hi = lax.Precision.HIGHEST
    t = jnp.einsum("sair,hrg->shaig", jnp.asarray(sel_r), rpb.astype(_F32) * LOG2_E, precision=hi)
    t = jnp.einsum("shaig,ckg->shacik", t, jnp.asarray(sel_c), precision=hi)
    t = t.reshape(3, N_HEADS, 4 * GRID_W, NA_WIN_ROWS * GRID_W)
    t = jnp.where(jnp.asarray(ok)[:, None], t, NEG_INF).astype(_BF16)
    return t.reshape(3, 4, 2 * TM, NA_WIN_ROWS * GRID_W)


def _route_rows(sel, sc):
    gs, gi1, gi2, gs1, gs2 = [], [], [], [], []
    for g in range(N_GROUPS):
        v = sel[4 * g:4 * g + 4]
        s = sc[4 * g:4 * g + 4]
        m1 = jnp.maximum(jnp.maximum(v[0], v[1]), jnp.maximum(v[2], v[3]))
        i1 = jnp.where(v[0] == m1, 0, jnp.where(v[1] == m1, 1, jnp.where(v[2] == m1, 2, 3)))
        rest = [jnp.where(i1 == k, NEG_INF, v[k]) for k in range(4)]
        m2 = jnp.maximum(jnp.maximum(rest[0], rest[1]), jnp.maximum(rest[2], rest[3]))
        i2 = jnp.where(rest[0] == m2, 0, jnp.where(rest[1] == m2, 1, jnp.where(rest[2] == m2, 2, 3)))

        def pick(idx, s=s):
            return jnp.where(idx == 0, s[0], jnp.where(idx == 1, s[1], jnp.where(idx == 2, s[2], s[3])))

        gs.append(m1 + m2)
        gi1.append(i1 + 4 * g)
        gi2.append(i2 + 4 * g)
        gs1.append(pick(i1))
        gs2.append(pick(i2))
    best = jnp.maximum(jnp.maximum(gs[0], gs[1]), jnp.maximum(gs[2], gs[3]))
    gsel = jnp.where(gs[0] == best, 0, jnp.where(gs[1] == best, 1, jnp.where(gs[2] == best, 2, 3)))

    def pickg(vals):
        return jnp.where(gsel == 0, vals[0], jnp.where(gsel == 1, vals[1], jnp.where(gsel == 2, vals[2], vals[3])))

    return pickg(gi1), pickg(gi2), pickg(gs1), pickg(gs2)


def _merge_body(oa_ref, ob_ref, on_ref, gate_ref, x_ref, xctx_ref, wb_ref, wo_ref, g1_ref, sh2_ref, sc2_ref, n2_ref,
                wrt_ref, rb_ref,
                x1_ref, h2_ref, bkt_ref, rank_ref, cnt_ref, carry_ref, *, n_lat_tiles):
    first = (pl.program_id(0) == 0) & (pl.program_id(1) == 0)
    x_res = jnp.where(pl.program_id(1) < n_lat_tiles, x_ref[0], xctx_ref[0])

    @pl.when(first)
    def _():
        carry_ref[...] = jnp.zeros_like(carry_ref)

    outs = (oa_ref, ob_ref, on_ref)
    y = None
    for n in range(3):
        gt = jax.nn.sigmoid(gate_ref[0, :, n * D_MODEL:(n + 1) * D_MODEL].astype(_F32))
        term = gt * _dot(outs[n][0], wb_ref[n])
        y = term if y is None else y + term
    z = _dot(y.astype(_BF16), wo_ref[...])
    x1 = x_res + g1_ref[0] * z
    x1_ref[0] = x1
    ms = jnp.mean(x1 * x1, axis=-1, keepdims=True)
    h2 = (x1 * lax.rsqrt(ms + EPS) * n2_ref[...]) * (1.0 + sc2_ref[0]) + sh2_ref[0]
    h2_ref[0, :, :D_MODEL] = h2

    hb = h2.astype(_BF16)
    hl = (h2 - hb.astype(_F32)).astype(_BF16)
    lt = (_dot(hb, wrt_ref[...]) + _dot(hl, wrt_ref[...])).T
    logits = lt[:N_EXPERTS] + lt[N_EXPERTS:2 * N_EXPERTS]
    scores = jax.nn.sigmoid(logits)
    sel = scores + rb_ref[...]
    sel_rows = [sel[e:e + 1, :] for e in range(N_EXPERTS)]
    sc_rows = [scores[e:e + 1, :] for e in range(N_EXPERTS)]
    e1, e2, s1, s2 = _route_rows(sel_rows, sc_rows)
    den = s1 + s2
    first_lo = e1 < e2
    e_lo = jnp.where(first_lo, e1, e2)
    e_hi = jnp.where(first_lo, e2, e1)
    w_lo = jnp.where(first_lo, s1, s2) / den
    w_hi = jnp.where(first_lo, s2, s1) / den
    a = e_lo & 3
    b = e_hi & 3
    bkt = (e_lo >> 2) * 6 + jnp.where(a == 0, 0, jnp.where(a == 1, 3, 5)) + (b - a - 1)
    bkt_ref[...] = bkt

    bid = lax.broadcasted_iota(jnp.int32, (BUCKET_ROWS, TM), 0)
    oh = bid == bkt
    tri = (lax.broadcasted_iota(jnp.int32, (TM, TM), 0) <= lax.broadcasted_iota(jnp.int32, (TM, TM), 1))
    cum = _dot(oh.astype(_BF16), tri.astype(_BF16))
    carry = carry_ref[:, 0:1]
    rank_ref[...] = jnp.sum(jnp.where(oh, carry + cum - 1.0, 0.0), axis=0, keepdims=True).astype(jnp.int32)
    new_carry = carry + cum[:, TM - 1:TM]
    carry_ref[...] = jnp.broadcast_to(new_carry, carry_ref.shape)
    cnt_ref[...] = jnp.broadcast_to(new_carry, cnt_ref.shape).astype(jnp.int32)

    lane = lax.broadcasted_iota(jnp.int32, (TM, 128), 1)

    def col(row):
        return jnp.concatenate([_col_from_row(row[:, h * 128:(h + 1) * 128]) for h in range(TM // 128)], axis=0)

    h2_ref[0, :, D_MODEL:] = jnp.where(lane == 0, col(w_lo), jnp.where(lane == 1, col(w_hi), 0.0))


def _merge_call(oa, ob, on, gate, x_lat, x_ctx, ctx_tile, wb, wo, g1, sh2, sc2, n2, wrt, rb, t_out,
                n_lat_tiles):
    b, _, d = x_lat.shape
    nb = g1.shape[0] - 1
    nt = t_out // TM
    n_tok = b * t_out

    def mod_idx(bi, i):
        return (jnp.where(i < n_lat_tiles, bi, nb), 0, 0)

    def tok(width):
        return pl.BlockSpec((1, TM, width), lambda bi, i: (bi, i, 0))

    def full(shape):
        return pl.BlockSpec(shape, lambda bi, i: (0,) * len(shape))

    lane_out = pl.BlockSpec((1, TM), lambda bi, i: (0, bi * nt + i))
    return pl.pallas_call(
        functools.partial(_merge_body, n_lat_tiles=n_lat_tiles),
        grid=(b, nt),
        in_specs=[tok(BRANCH), tok(BRANCH), tok(BRANCH), tok(3 * d), *_token_specs(n_lat_tiles, ctx_tile, d),
                  full((3, BRANCH, d)), full((d, d)),
                  pl.BlockSpec((1, 1, d), mod_idx), pl.BlockSpec((1, 1, d), mod_idx),
                  pl.BlockSpec((1, 1, d), mod_idx), full((1, d)),
                  full((d, 128)), full((N_EXPERTS, 1))],
        out_specs=[tok(d), tok(ROW_W), lane_out, lane_out, full((BUCKET_ROWS, 128))],
        out_shape=[jax.ShapeDtypeStruct((b, t_out, d), _F32), jax.ShapeDtypeStruct((b, t_out, ROW_W), _F32),
                   jax.ShapeDtypeStruct((1, n_tok), jnp.int32), jax.ShapeDtypeStruct((1, n_tok), jnp.int32),
                   jax.ShapeDtypeStruct((BUCKET_ROWS, 128), jnp.int32)],
        scratch_shapes=[pltpu.VMEM((BUCKET_ROWS, 128), _F32)],
        compiler_params=_cparams(("arbitrary", "arbitrary"), VMEM_LIMIT),
        name="merge_route",
    )(oa, ob, on, gate, x_lat, x_ctx, wb, wo, g1, sh2, sc2, n2, wrt, rb)


ROW_DMA_UNROLL = 8


def _dispatch_body(pad_start_ref, pad_n_ref, nv_ref, pos_ref, h_ref, xs_ref, zbuf, sem, zsem):
    @pl.when(pl.program_id(0) == 0)
    def _():
        zbuf[...] = jnp.zeros_like(zbuf)

        def zero_row(e, r):
            return pltpu.make_async_copy(zbuf.at[pl.ds(0, 1)], xs_ref.at[pl.ds(pad_start_ref[e] + r, 1)], zsem)

        def zero_tile(i):
            return pltpu.make_async_copy(zbuf, xs_ref.at[pl.ds(pl.multiple_of(i * TM_E, TM_E), TM_E)], zsem)

        n_tiles = xs_ref.shape[0] // TM_E
        for e in range(N_BUCKETS):
            lax.fori_loop(0, pad_n_ref[e], lambda r, c, e=e: (zero_row(e, r).start(), c)[1], 0)
        lax.fori_loop(nv_ref[0], n_tiles, lambda i, c: (zero_tile(i).start(), c)[1], 0)
        for e in range(N_BUCKETS):
            lax.fori_loop(0, pad_n_ref[e], lambda r, c, e=e: (zero_row(e, r).wait(), c)[1], 0)
        lax.fori_loop(nv_ref[0], n_tiles, lambda i, c: (zero_tile(i).wait(), c)[1], 0)

    def row_copy(t):
        return pltpu.make_async_copy(h_ref.at[pl.ds(t, 1)], xs_ref.at[pl.ds(pos_ref[0, 0, t], 1)], sem)

    rows = h_ref.shape[0]
    lax.fori_loop(0, rows, lambda t, c: (row_copy(t).start(), c)[1], 0, unroll=ROW_DMA_UNROLL)
    lax.fori_loop(0, rows, lambda t, c: (row_copy(t).wait(), c)[1], 0, unroll=ROW_DMA_UNROLL)


def _dispatch_call(pad_start, pad_n, n_valid, pos, h2, p_rows):
    n_tok, d = h2.shape
    rows = TM_D if n_tok % TM_D == 0 else TM
    n_tiles = n_tok // rows
    pos3 = pos.reshape(n_tiles, 1, rows)
    return pl.pallas_call(
        _dispatch_body,
        grid_spec=pltpu.PrefetchScalarGridSpec(
            num_scalar_prefetch=3,
            grid=(n_tiles,),
            in_specs=[pl.BlockSpec((1, 1, rows), lambda i, ps, pn, nv: (i, 0, 0), memory_space=pltpu.SMEM),
                      pl.BlockSpec((rows, d), lambda i, ps, pn, nv: (i, 0))],
            out_specs=pl.BlockSpec(memory_space=pl.ANY),
            scratch_shapes=[pltpu.VMEM((TM_E, d), h2.dtype), pltpu.SemaphoreType.DMA, pltpu.SemaphoreType.DMA],
        ),
        out_shape=jax.ShapeDtypeStruct((p_rows, d), h2.dtype),
        compiler_params=_cparams(("arbitrary",)),
        name="moe_dispatch",
    )(pad_start, pad_n, n_valid, pos3, h2)


def _moe_body(elo_ref, ehi_ref, nv_ref, xs_ref, wg_lo, wu_lo, wd_lo, wg_hi, wu_hi, wd_hi, ys_ref):
    del elo_ref, ehi_ref
    live = pl.program_id(0) < nv_ref[0]

    @pl.when(live)
    def _():
        xb = xs_ref[:, :D_MODEL].astype(_BF16)
        wts = xs_ref[:, D_MODEL:]

        def expert(wg_ref, wu_ref, wd_ref):
            gate = _dot(xb, wg_ref[0].astype(_BF16))
            up = _dot(xb, wu_ref[0].astype(_BF16))
            act = (gate * jax.nn.sigmoid(gate)) * up
            return _dot(act.astype(_BF16), wd_ref[0].astype(_BF16))

        ys_ref[...] = wts[:, 0:1] * expert(wg_lo, wu_lo, wd_lo) + wts[:, 1:2] * expert(wg_hi, wu_hi, wd_hi)

    @pl.when(jnp.logical_not(live))
    def _():
        ys_ref[...] = jnp.zeros_like(ys_ref)


def _moe_call(tile_elo, tile_ehi, n_valid, xs, wg, wu, wd, layer):
    p_rows = xs.shape[0]
    d = D_MODEL
    n_tiles = p_rows // TM_E

    def row_idx(i, elo, ehi, nv):
        return (jnp.minimum(i, nv[0] - 1), 0)

    def lo_idx(i, elo, ehi, nv):
        return (layer, elo[i], 0, 0)

    def hi_idx(i, elo, ehi, nv):
        return (layer, ehi[i], 0, 0)

    def w_specs(idx):
        return [pl.BlockSpec((None, 1, d, D_FF), idx), pl.BlockSpec((None, 1, d, D_FF), idx),
                pl.BlockSpec((None, 1, D_FF, d), idx)]

    return pl.pallas_call(
        _moe_body,
        grid_spec=pltpu.PrefetchScalarGridSpec(
            num_scalar_prefetch=3,
            grid=(n_tiles,),
            in_specs=[pl.BlockSpec((TM_E, ROW_W), row_idx), *w_specs(lo_idx), *w_specs(hi_idx)],
            out_specs=pl.BlockSpec((TM_E, d), lambda i, elo, ehi, nv: (i, 0)),
        ),
        out_shape=jax.ShapeDtypeStruct((p_rows, d), _F32),
        compiler_params=_cparams(("arbitrary",), VMEM_LIMIT),
        name="moe_experts",
    )(tile_elo, tile_ehi, n_valid, xs, wg, wu, wd, wg, wu, wd)


def _combine_body(pos0_ref, posn_ref, x1_ref, g2_ref, fn_ref, ys_ref, o_ref, gbuf, sems):
    step = pl.program_id(0)
    y, issue_next, finish = _moe_rows_pipeline(step, pl.num_programs(0) - 1, pos0_ref, posn_ref, ys_ref,
                                               gbuf, sems)
    x2 = x1_ref[...] + g2_ref[0] * y
    issue_next()
    ms = jnp.mean(x2 * x2, axis=-1, keepdims=True)
    o_ref[...] = x2 * lax.rsqrt(ms + EPS) * fn_ref[...]
    finish()


def _combine_call(pos, x1, g2, fnorm, ys, t_out):
    n_tok, d = x1.shape
    rows = TM_C if t_out % TM_C == 0 else TM
    nt = t_out // rows
    n_tiles = n_tok // rows
    pos3 = pos.reshape(n_tiles, 1, rows)
    smem_tile = functools.partial(pl.BlockSpec, (1, 1, rows), memory_space=pltpu.SMEM)
    return pl.pallas_call(
        _combine_body,
        grid=(n_tiles,),
        in_specs=[smem_tile(lambda i: (0, 0, 0)),
                  smem_tile(lambda i: (jnp.minimum(i + 1, n_tiles - 1), 0, 0)),
                  pl.BlockSpec((rows, d), lambda i: (i, 0)),
                  pl.BlockSpec((1, 1, d), lambda i: (i // nt, 0, 0)),
                  pl.BlockSpec((1, d), lambda i: (0, 0)),
                  pl.BlockSpec(memory_space=pl.ANY)],
        out_specs=pl.BlockSpec((rows, d), lambda i: (i, 0)),
        out_shape=jax.ShapeDtypeStruct((n_tok, d), _F32),
        scratch_shapes=[pltpu.VMEM((2, rows, d), _F32), pltpu.SemaphoreType.DMA((2,))],
        compiler_params=_cparams(("arbitrary",)),
        name="moe_combine",
    )(pos3, pos3, x1, g2, fnorm, ys)


def _rope_tables(s_len, l_ctx):
    t = jnp.arange(s_len, dtype=jnp.int32)
    row = (t // GRID_W).astype(_F32)
    col = (t % GRID_W).astype(_F32)
    half = HEAD_DIM // 2
    inv = ROPE_THETA ** (-jnp.arange(0, half, 2, dtype=_F32) / half)
    ar = row[:, None] * inv
    ac = col[:, None] * inv
    cos64 = jnp.concatenate([jnp.cos(ar), jnp.cos(ar), jnp.cos(ac), jnp.cos(ac)], axis=1)
    sin64 = jnp.concatenate([-jnp.sin(ar), jnp.sin(ar), -jnp.sin(ac), jnp.sin(ac)], axis=1)
    cos_t = jnp.concatenate([jnp.tile(cos64, (1, 2)), jnp.ones((l_ctx, 128), _F32)], axis=0)
    sin_t = jnp.concatenate([jnp.tile(sin64, (1, 2)), jnp.zeros((l_ctx, 128), _F32)], axis=0)
    return cos_t, sin_t


def _block_diag_ones():
    r = np.arange(BRANCH)
    return jnp.asarray((r[:, None] // HEAD_DIM) == (r[None, :] // HEAD_DIM), dtype=_BF16)


def kernel(x, c, ctx, c_ctx, w_mod, b_mod, norm1, norm2, w_in, q_norm_a, k_norm_a, sink_b, rpb_n, w_branch, w_out,
           w_router, router_bias, w_e_gate, w_e_up, w_e_down, final_norm):
    b, s_len, d = x.shape
    l_ctx = ctx.shape[1]
    depth = w_in.shape[0]
    t_all = s_len + l_ctx
    rows = s_len // GRID_W
    assert d == D_MODEL and l_ctx == TM and s_len % TM == 0 and b + 1 <= 16
    assert s_len >= TM + 2 * WINDOW and rows >= NA_WIN_ROWS and rows % 4 == 0
    n_lat_tiles = s_len // TM

    cos_t, sin_t = _rope_tables(s_len, l_ctx)
    bd = _block_diag_ones()
    cvec = jnp.zeros((16, d), _F32).at[:b].set(c).at[b].set(c_ctx)
    mods = _mod_call(cvec, w_mod, b_mod)[:, :b + 1].reshape(depth, b + 1, 6, 1, d)

    wr = w_router.astype(_F32)
    wr_hi = wr.astype(_BF16)
    wr_lo = (wr - wr_hi.astype(_F32)).astype(_BF16)
    wrt = jnp.concatenate([wr_hi, wr_lo, jnp.zeros((d, 128 - 2 * N_EXPERTS), _BF16)], axis=1)
    rb = router_bias.astype(_F32).reshape(N_EXPERTS, 1)

    x_lat, x_ctx, ctx_tile = x, ctx, 0
    pending = None
    for l in range(depth):
        last = l == depth - 1
        sh1, sc1, g1, sh2, sc2, g2 = [mods[l, :, n] for n in range(6)]
        w_f = w_in[l]
        w_l = jnp.concatenate([_pair_heads(w_f[:, OFF_QA:OFF_KA], 1), w_f[:, OFF_KA:OFF_QB],
                               _pair_heads(w_f[:, OFF_QB:OFF_KB], 1), w_f[:, OFF_KB:]], axis=1).astype(_BF16)
        gq = (jnp.tile(q_norm_a[l].astype(_F32), N_HEADS) * Q_SCALE).reshape(1, BRANCH)
        gk = jnp.tile(k_norm_a[l].astype(_F32), 2).reshape(1, KV_A)
        outs = _inproj_call(x_lat, x_ctx, ctx_tile, n_lat_tiles, norm1[l].reshape(1, d), sh1, sc1, w_l, cos_t, sin_t,
                            gq, gk, bd, combine=pending)
        qa, ka, va, qb, kb, vb, qn, kn, vn, gate = outs[:10]
        if pending is not None:
            x_lat = x_ctx = outs[10]

        sink = sink_b[l].astype(_F32) * LOG2_E
        oa = _attn_a_call(qa, ka, va, s_len, not last)
        ob = _attn_b_call(qb, kb, vb, sink, s_len, not last)
        on = _attn_n_call(qn, kn, vn, _na_bias_table(rpb_n[l], rows), s_len, not last)

        t_out = s_len if last else t_all
        wb = jnp.stack([_pair_heads(w_branch[l, 0], 0), _pair_heads(w_branch[l, 1], 0),
                        w_branch[l, 2]]).astype(_BF16)
        x1, h2w, bkt, rank, cnt = _merge_call(
            oa, ob, on, gate, x_lat, x_ctx, ctx_tile, wb, w_out[l].astype(_BF16), g1, sh2, sc2,
            norm2[l].reshape(1, d), wrt, rb, t_out, n_lat_tiles)

        n_tok = b * t_out
        counts = cnt[:N_BUCKETS, 0]
        padded = ((counts + TM_E - 1) // TM_E) * TM_E
        ends = jnp.cumsum(padded)
        offs = ends - padded
        pos = rank[0] + jnp.sum(jnp.where(bkt == jnp.arange(N_BUCKETS)[:, None], offs[:, None], 0), axis=0)
        pos3 = pos.reshape(n_tok // TM, 1, TM)
        p_rows = n_tok + N_BUCKETS * TM_E
        n_tiles = p_rows // TM_E
        n_valid = (ends[-1] // TM_E).astype(jnp.int32).reshape(1)
        tile_start = jnp.arange(n_tiles, dtype=jnp.int32) * TM_E
        tile_bucket = jnp.minimum(jnp.sum(tile_start[:, None] >= ends[None, :], axis=1), N_BUCKETS - 1)
        tile_group = (tile_bucket // 6) * EXPERTS_PER_GROUP
        tile_elo = (tile_group + jnp.asarray(_PAIR_LO)[tile_bucket % 6]).astype(jnp.int32)
        tile_ehi = (tile_group + jnp.asarray(_PAIR_HI)[tile_bucket % 6]).astype(jnp.int32)

        pad_start = (offs + counts).astype(jnp.int32)
        pad_n = (padded - counts).astype(jnp.int32)
        xs = _dispatch_call(pad_start, pad_n, n_valid, pos, h2w.reshape(n_tok, ROW_W), p_rows)
        ys = _moe_call(tile_elo, tile_ehi, n_valid, xs, w_e_gate, w_e_up, w_e_down, l)
        if last:
            out = _combine_call(pos, x1.reshape(n_tok, d), g2, final_norm.reshape(1, d), ys, t_out)
            return out.reshape(b, t_out, d)
        pending = (pos3, g2, ys)
        x_lat = x_ctx = x1
        ctx_tile = n_lat_tiles
```

```python
import functools

import numpy as np
import jax
import jax.numpy as jnp
from jax import lax
from jax.experimental import pallas as pl
from jax.experimental.pallas import tpu as pltpu

D_MODEL = 1024
HEAD_DIM = 64
GRID_W = 64
N_HEADS = 8
WINDOW = 128
NA_ROWS = 8
NA_COLS = 16
ROPE_THETA = 10000.0
N_EXPERTS = 16
N_GROUPS = 4
EXPERTS_PER_GROUP = N_EXPERTS // N_GROUPS
D_FF = 512
EPS = 1e-6
NEG_INF = -1e30
ATTN_SCALE = HEAD_DIM ** -0.5
LOG2_E = 1.4426950408889634
Q_SCALE = ATTN_SCALE * LOG2_E
BRANCH = N_HEADS * HEAD_DIM
KV_A = 2 * HEAD_DIM
IN_COLS = 6144

OFF_QA, OFF_KA, OFF_VA = 0, 512, 640
OFF_QB, OFF_KB, OFF_VB = 768, 1280, 1408
OFF_QN, OFF_KN, OFF_VN = 1536, 2048, 2560
OFF_GATE = 3072

TM = 256
TM_E = 512
TM_D = 1024
TM_C = 512
N_BUCKETS = N_GROUPS * 6
BUCKET_ROWS = 32
ROW_W = D_MODEL + 128
_PAIR_LO = np.array([0, 0, 0, 1, 1, 2])
_PAIR_HI = np.array([1, 2, 3, 2, 3, 3])
NA_WIN_ROWS = 12
VMEM_LIMIT = 56 * 1024 * 1024

_F32 = jnp.float32
_BF16 = jnp.bfloat16

def _pair_heads(w, axis):
    shape = w.shape
    split = shape[:axis] + (2, 4, HEAD_DIM) + shape[axis + 1:]
    return jnp.swapaxes(w.reshape(split), axis, axis + 1).reshape(shape)


def _nt(a, b):
    return lax.dot_general(a, b, (((1,), (1,)), ((), ())), preferred_element_type=_F32)


def _dot(a, b):
    return jnp.dot(a, b, preferred_element_type=_F32)


def _cparams(sem, vmem=None):
    return pltpu.CompilerParams(dimension_semantics=sem, vmem_limit_bytes=vmem)


def _mod_body(c_ref, w_ref, b_ref, o_ref):
    cv = c_ref[...]
    a = (cv * jax.nn.sigmoid(cv)).astype(_BF16)
    o_ref[0] = _dot(a, w_ref[0].astype(_BF16)) + b_ref[0]


def _mod_call(cvec, w_mod, b_mod):
    depth, d, n = w_mod.shape
    tn = 1536
    return pl.pallas_call(
        _mod_body,
        grid=(depth, n // tn),
        in_specs=[
            pl.BlockSpec((16, d), lambda l, j: (0, 0)),
            pl.BlockSpec((1, d, tn), lambda l, j: (l, 0, j)),
            pl.BlockSpec((1, 1, tn), lambda l, j: (l, 0, j)),
        ],
        out_specs=pl.BlockSpec((1, 16, tn), lambda l, j: (l, 0, j)),
        out_shape=jax.ShapeDtypeStruct((depth, 16, n), _F32),
        compiler_params=_cparams(("arbitrary", "arbitrary")),
        name="mod_vectors",
    )(cvec, w_mod, b_mod.reshape(depth, 1, n))


def _rope(v, c, s):
    w = v.shape[1]
    lane = lax.broadcasted_iota(jnp.int32, v.shape, 1)
    first = (lane & 16) == 0
    sw = jnp.where(first, pltpu.roll(v, w - 16, 1), pltpu.roll(v, 16, 1))
    return v * c + sw * s


def _inproj_body(x_ref, xctx_ref, *refs, n_lat_tiles):
    x = jnp.where(pl.program_id(1) < n_lat_tiles, x_ref[0], xctx_ref[0])
    _inproj_core(x, *refs)


def _col_from_row(row):
    blk = jnp.broadcast_to(row, (128, 128))
    return blk.T[:, 0:1]


def _moe_rows_pipeline(step, last_step, pos0_ref, posn_ref, ys_ref, gbuf, sems):
    def row_copy(pos_ref, slot, t):
        return pltpu.make_async_copy(ys_ref.at[pl.ds(pos_ref[0, 0, t], 1)], gbuf.at[slot, pl.ds(t, 1)],
                                     sems.at[slot])

    rows = gbuf.shape[1]

    def gather(pos_ref, slot):
        for t in range(rows):
            row_copy(pos_ref, slot, t).start(priority=t % 2)

    def drain(slot):
        for t in range(rows):
            row_copy(pos0_ref, slot, t).wait()

    pl.when(step == 0)(lambda: gather(pos0_ref, 0))
    slot = step % 2
    drain(slot)
    y = gbuf[slot]
    return (y, lambda: gather(posn_ref, (step + 1) % 2),
            lambda: pl.when(step == last_step)(lambda: drain((step + 1) % 2)))


def _inproj_combine_body(pos0_ref, posn_ref, x_ref, xctx_ref, g2_ref, ys_ref, *refs, n_lat_tiles):
    core_refs, x2_ref, gbuf, sems = refs[:-3], refs[-3], refs[-2], refs[-1]
    nt = pl.num_programs(1)
    step = pl.program_id(0) * nt + pl.program_id(1)
    y, issue_next, finish = _moe_rows_pipeline(step, pl.num_programs(0) * nt - 1, pos0_ref, posn_ref,
                                               ys_ref, gbuf, sems)
    x1 = jnp.where(pl.program_id(1) < n_lat_tiles, x_ref[0], xctx_ref[0])
    x = x1 + g2_ref[0] * y
    x2_ref[0] = x
    issue_next()
    _inproj_core(x, *core_refs)
    finish()


def _inproj_core(x, g_ref, sh_ref, sc_ref, w_ref, cos_ref, sin_ref, gq_ref, gk_ref, bd_ref,
                 qa_ref, ka_ref, va_ref, qb_ref, kb_ref, vb_ref, qn_ref, kn_ref, vn_ref, gate_ref):
    ms = jnp.mean(x * x, axis=-1, keepdims=True)
    h = (x * lax.rsqrt(ms + EPS) * g_ref[...]) * (1.0 + sc_ref[0]) + sh_ref[0]
    hb = h.astype(_BF16)

    def seg(off, width):
        return _dot(hb, w_ref[:, off:off + width])

    c1 = cos_ref[...]
    s1 = sin_ref[...]
    c4 = jnp.concatenate([c1] * 4, axis=1)
    s4 = jnp.concatenate([s1] * 4, axis=1)
    bd = bd_ref[...]

    qa = seg(OFF_QA, BRANCH)
    ssq = _dot((qa * qa).astype(_BF16), bd)
    qa = qa * lax.rsqrt(ssq * (1.0 / HEAD_DIM) + EPS) * gq_ref[...]
    qa_ref[0] = _rope(qa, c4, s4).astype(_BF16)
    ka = seg(OFF_KA, KV_A)
    ssk = _dot((ka * ka).astype(_BF16), bd_ref[:KV_A, :KV_A])
    ka = ka * lax.rsqrt(ssk * (1.0 / HEAD_DIM) + EPS) * gk_ref[...]
    ka_ref[0] = _rope(ka, c1, s1).astype(_BF16)
    va_ref[0] = seg(OFF_VA, KV_A).astype(_BF16)
    qb_ref[0] = _rope(seg(OFF_QB, BRANCH) * Q_SCALE, c4, s4).astype(_BF16)
    kb_ref[0] = _rope(seg(OFF_KB, KV_A), c1, s1).astype(_BF16)
    vb_ref[0] = seg(OFF_VB, KV_A).astype(_BF16)
    qn_ref[0] = (seg(OFF_QN, BRANCH) * Q_SCALE).astype(_BF16)
    kn_ref[0] = seg(OFF_KN, BRANCH).astype(_BF16)
    vn_ref[0] = seg(OFF_VN, BRANCH).astype(_BF16)
    for n in range(3):
        gate_ref[0, :, n * D_MODEL:(n + 1) * D_MODEL] = seg(OFF_GATE + n * D_MODEL, D_MODEL).astype(_BF16)


def _token_specs(n_lat_tiles, ctx_tile, d):
    lat = pl.BlockSpec((1, TM, d), lambda bi, i: (bi, jnp.minimum(i, n_lat_tiles - 1), 0))
    ctx = pl.BlockSpec((1, TM, d), lambda bi, i: (bi, ctx_tile, 0))
    return lat, ctx


def _inproj_call(x_lat, x_ctx, ctx_tile, n_lat_tiles, g1n, sh, sc, w, cos_t, sin_t, gq, gk, bd, combine=None):
    b, _, d = x_lat.shape
    nb = sh.shape[0] - 1
    nt = n_lat_tiles + 1
    t = nt * TM

    def mod_idx(bi, i):
        return (jnp.where(i < n_lat_tiles, bi, nb), 0, 0)

    def tok(width):
        return pl.BlockSpec((1, TM, width), lambda bi, i: (bi, i, 0))

    widths = [BRANCH, KV_A, KV_A, BRANCH, KV_A, KV_A, BRANCH, BRANCH, BRANCH, 3 * D_MODEL]
    core_specs = [
        pl.BlockSpec((1, d), lambda bi, i: (0, 0)),
        pl.BlockSpec((1, 1, d), mod_idx),
        pl.BlockSpec((1, 1, d), mod_idx),
        pl.BlockSpec((d, IN_COLS), lambda bi, i: (0, 0)),
        pl.BlockSpec((TM, 128), lambda bi, i: (i, 0)),
        pl.BlockSpec((TM, 128), lambda bi, i: (i, 0)),
        pl.BlockSpec((1, BRANCH), lambda bi, i: (0, 0)),
        pl.BlockSpec((1, KV_A), lambda bi, i: (0, 0)),
        pl.BlockSpec((BRANCH, BRANCH), lambda bi, i: (0, 0)),
    ]
    core_args = (g1n, sh, sc, w, cos_t, sin_t, gq, gk, bd)
    out_specs = [tok(wd) for wd in widths]
    out_shape = [jax.ShapeDtypeStruct((b, t, wd), _BF16) for wd in widths]
    if combine is None:
        return pl.pallas_call(
            functools.partial(_inproj_body, n_lat_tiles=n_lat_tiles),
            grid=(b, nt),
            in_specs=[*_token_specs(n_lat_tiles, ctx_tile, d), *core_specs],
            out_specs=out_specs,
            out_shape=out_shape,
            compiler_params=_cparams(("arbitrary", "arbitrary"), VMEM_LIMIT),
            name="inproj",
        )(x_lat, x_ctx, *core_args)

    pos3, g2, ys = combine
    last_tile = b * nt - 1
    smem_tile = functools.partial(pl.BlockSpec, (1, 1, TM), memory_space=pltpu.SMEM)
    return pl.pallas_call(
        functools.partial(_inproj_combine_body, n_lat_tiles=n_lat_tiles),
        grid=(b, nt),
        in_specs=[
            smem_tile(lambda bi, i: (0, 0, 0)),
            smem_tile(lambda bi, i: (jnp.minimum(bi * nt + i + 1, last_tile), 0, 0)),
            *_token_specs(n_lat_tiles, ctx_tile, d),
            pl.BlockSpec((1, 1, d), mod_idx),
            pl.BlockSpec(memory_space=pl.ANY),
            *core_specs,
        ],
        out_specs=out_specs + [tok(d)],
        out_shape=out_shape + [jax.ShapeDtypeStruct((b, t, d), _F32)],
        scratch_shapes=[pltpu.VMEM((2, TM, d), _F32), pltpu.SemaphoreType.DMA((2,))],
        compiler_params=_cparams(("arbitrary", "arbitrary"), VMEM_LIMIT),
        name="inproj_combine",
    )(pos3, pos3, x_lat, x_ctx, g2, ys, *core_args)


def _stack_heads(q):
    lane = lax.broadcasted_iota(jnp.int32, q.shape, 1)
    lo = lane < HEAD_DIM
    zero = jnp.zeros_like(q)
    return jnp.concatenate([jnp.where(lo, q, zero), jnp.where(lo, zero, q)], axis=0)


def _unstack_heads(o):
    half = o.shape[0] // 2
    lane = lax.broadcasted_iota(jnp.int32, (half, o.shape[1]), 1)
    return jnp.where(lane < HEAD_DIM, o[:half], o[half:])


KA_CHUNK_MAX = 2048
KA_GROUPS = 4


def _flash_update(lhs, kc, vc, m, l128, acc):
    s = _nt(lhs, kc)
    tiles = [s[:, t * 128:(t + 1) * 128] for t in range(s.shape[1] // 128)]
    m_new = jnp.maximum(m, jnp.max(functools.reduce(jnp.maximum, tiles), axis=1, keepdims=True))
    alpha = jnp.exp2(m - m_new)
    ps = [jnp.exp2(t - m_new) for t in tiles]
    l128 = alpha * l128 + functools.reduce(lambda a, c: a + c, ps)
    p = jnp.concatenate([x.astype(_BF16) for x in ps], axis=1)
    acc = alpha * acc + _dot(p, vc)
    return m_new, l128, acc


def _attn_a_body(q_ref, k_ref, v_ref, o_ref, *, s_len, l_ctx, with_ctx):
    def run(chunks):
        for g in range(KA_GROUPS):
            sl = slice(g * 128, (g + 1) * 128)
            lhs = _stack_heads(q_ref[0, :, sl])
            m = jnp.full((2 * TM, 128), NEG_INF, _F32)
            l128 = jnp.zeros((2 * TM, 128), _F32)
            acc = jnp.zeros((2 * TM, 128), _F32)
            for start, size in chunks:
                m, l128, acc = _flash_update(lhs, k_ref[0, start:start + size, :], v_ref[0, start:start + size, :],
                                             m, l128, acc)
            l = jnp.sum(l128, axis=1, keepdims=True)
            o_ref[0, :, sl] = _unstack_heads(acc / l).astype(_BF16)

    ctx_chunk = (s_len, l_ctx)
    chunk = KA_CHUNK_MAX
    while s_len % chunk:
        chunk //= 2
    lat_chunks = [(c * chunk, chunk) for c in range(s_len // chunk)] + [ctx_chunk]
    if with_ctx:
        is_ctx = pl.program_id(1) == s_len // TM
        pl.when(jnp.logical_not(is_ctx))(lambda: run(lat_chunks))
        pl.when(is_ctx)(lambda: run([ctx_chunk]))
    else:
        run(lat_chunks)


def _attn_a_call(q, k, v, s_len, with_ctx):
    b, t, _ = q.shape
    nq = s_len // TM + (1 if with_ctx else 0)
    return pl.pallas_call(
        functools.partial(_attn_a_body, s_len=s_len, l_ctx=t - s_len, with_ctx=with_ctx),
        grid=(b, nq, 4 // KA_GROUPS),
        in_specs=[
            pl.BlockSpec((1, TM, 128 * KA_GROUPS), lambda bi, i, j: (bi, i, j)),
            pl.BlockSpec((1, t, 128), lambda bi, i, j: (bi, 0, 0)),
            pl.BlockSpec((1, t, 128), lambda bi, i, j: (bi, 0, 0)),
        ],
        out_specs=pl.BlockSpec((1, TM, 128 * KA_GROUPS), lambda bi, i, j: (bi, i, j)),
        out_shape=jax.ShapeDtypeStruct((b, nq * TM, BRANCH), _BF16),
        compiler_params=_cparams(("arbitrary", "arbitrary", "arbitrary"), VMEM_LIMIT),
        name="attn_global",
    )(q, k, v)


def _softmax_parts(parts, extra=None):
    m = functools.reduce(jnp.maximum, [jnp.max(s, axis=1, keepdims=True) for s in parts])
    if extra is not None:
        m = jnp.maximum(m, extra)
    ps = [jnp.exp2(s - m) for s in parts]
    l = functools.reduce(lambda a, c: a + c, [jnp.sum(p, axis=1, keepdims=True) for p in ps])
    if extra is not None:
        l = l + jnp.exp2(extra - m)
    return ps, l


def _attn_b_body(sink_ref, q_ref, k_ref, v_ref, o_ref, *, s_len, l_ctx, with_ctx):
    span = TM + 2 * WINDOW
    i = pl.program_id(1)
    is_hi = lax.broadcasted_iota(jnp.int32, (2 * TM, 1), 0) >= TM

    def finish(j, ps, vs, l):
        o = functools.reduce(lambda a, c: a + c, [_dot(p.astype(_BF16), v) for p, v in zip(ps, vs)]) / l
        o_ref[0, :, j * 128:(j + 1) * 128] = _unstack_heads(o).astype(_BF16)

    def latent():
        start = i * TM
        ws = pl.multiple_of(jnp.clip(start - WINDOW, 0, s_len - span), WINDOW)
        kw = k_ref[0, pl.ds(ws, span), :]
        vw = v_ref[0, pl.ds(ws, span), :]
        kc = k_ref[0, s_len:s_len + l_ctx, :]
        vc = v_ref[0, s_len:s_len + l_ctx, :]
        row = lax.broadcasted_iota(jnp.int32, (2 * TM, span), 0)
        qpos = start + jnp.where(row >= TM, row - TM, row)
        kpos = ws + lax.broadcasted_iota(jnp.int32, (2 * TM, span), 1)
        band = jnp.abs(kpos - qpos) <= WINDOW
        scores = []
        for j in range(4):
            lhs = _stack_heads(q_ref[0, :, j * 128:(j + 1) * 128])
            scores.append((_nt(lhs, kw), _nt(lhs, kc)))
        for j in range(4):
            s_loc = jnp.where(band, scores[j][0], NEG_INF)
            sink = jnp.where(is_hi, sink_ref[4 + j], sink_ref[j])
            ps, l = _softmax_parts([s_loc, scores[j][1]], sink)
            finish(j, ps, (vw, vc), l)

    def context():
        kc = k_ref[0, s_len:s_len + l_ctx, :]
        vc = v_ref[0, s_len:s_len + l_ctx, :]
        for j in range(4):
            lhs = _stack_heads(q_ref[0, :, j * 128:(j + 1) * 128])
            sink = jnp.where(is_hi, sink_ref[4 + j], sink_ref[j])
            ps, l = _softmax_parts([_nt(lhs, kc)], sink)
            finish(j, ps, (vc,), l)

    if with_ctx:
        is_ctx = i == s_len // TM
        pl.when(jnp.logical_not(is_ctx))(latent)
        pl.when(is_ctx)(context)
    else:
        latent()


def _attn_b_call(q, k, v, sink, s_len, with_ctx):
    b, t, _ = q.shape
    nq = s_len // TM + (1 if with_ctx else 0)
    return pl.pallas_call(
        functools.partial(_attn_b_body, s_len=s_len, l_ctx=t - s_len, with_ctx=with_ctx),
        grid=(b, nq),
        in_specs=[
            pl.BlockSpec(memory_space=pltpu.SMEM),
            pl.BlockSpec((1, TM, BRANCH), lambda bi, i: (bi, i, 0)),
            pl.BlockSpec((1, t, 128), lambda bi, i: (bi, 0, 0)),
            pl.BlockSpec((1, t, 128), lambda bi, i: (bi, 0, 0)),
        ],
        out_specs=pl.BlockSpec((1, TM, BRANCH), lambda bi, i: (bi, i, 0)),
        out_shape=jax.ShapeDtypeStruct((b, nq * TM, BRANCH), _BF16),
        compiler_params=_cparams(("arbitrary", "arbitrary"), VMEM_LIMIT),
        name="attn_window",
    )(sink, q, k, v)


def _attn_n_body(q_ref, k_ref, v_ref, bias_ref, o_ref, *, s_len, l_ctx, with_ctx):
    rows = s_len // GRID_W
    span = NA_WIN_ROWS * GRID_W
    r = pl.program_id(1)

    def finish(j, ps, vs, l):
        o = functools.reduce(lambda a, c: a + c, [_dot(p.astype(_BF16), v) for p, v in zip(ps, vs)]) / l
        o_ref[0, :, j * 128:(j + 1) * 128] = _unstack_heads(o).astype(_BF16)

    def latent():
        u0 = jnp.clip(4 * r - 4, 0, rows - NA_WIN_ROWS)
        ws = pl.multiple_of(u0 * GRID_W, GRID_W)
        scores = []
        for j in range(4):
            sl = slice(j * 128, (j + 1) * 128)
            lhs = _stack_heads(q_ref[0, :, sl])
            scores.append((_nt(lhs, k_ref[0, pl.ds(ws, span), sl]), _nt(lhs, k_ref[0, s_len:s_len + l_ctx, sl])))
        for j in range(4):
            sl = slice(j * 128, (j + 1) * 128)
            ps, l = _softmax_parts([scores[j][0] + bias_ref[0, j].astype(_F32), scores[j][1]])
            finish(j, ps, (v_ref[0, pl.ds(ws, span), sl], v_ref[0, s_len:s_len + l_ctx, sl]), l)

    def context():
        for j in range(4):
            sl = slice(j * 128, (j + 1) * 128)
            lhs = _stack_heads(q_ref[0, :, sl])
            ps, l = _softmax_parts([_nt(lhs, k_ref[0, s_len:s_len + l_ctx, sl])])
            finish(j, ps, (v_ref[0, s_len:s_len + l_ctx, sl],), l)

    if with_ctx:
        is_ctx = r == s_len // TM
        pl.when(jnp.logical_not(is_ctx))(latent)
        pl.when(is_ctx)(context)
    else:
        latent()


def _attn_n_call(q, k, v, bias_tbl, s_len, with_ctx):
    b, t, _ = q.shape
    n_r = s_len // TM
    nq = n_r + (1 if with_ctx else 0)
    span = NA_WIN_ROWS * GRID_W

    def case(ri):
        return jnp.where(ri == 0, 0, jnp.where(ri >= n_r - 1, 2, 1))

    return pl.pallas_call(
        functools.partial(_attn_n_body, s_len=s_len, l_ctx=t - s_len, with_ctx=with_ctx),
        grid=(b, nq),
        in_specs=[
            pl.BlockSpec((1, TM, BRANCH), lambda bi, ri: (bi, ri, 0)),
            pl.BlockSpec((1, t, BRANCH), lambda bi, ri: (bi, 0, 0)),
            pl.BlockSpec((1, t, BRANCH), lambda bi, ri: (bi, 0, 0)),
            pl.BlockSpec((1, 4, 2 * TM, span), lambda bi, ri: (case(ri), 0, 0, 0)),
        ],
        out_specs=pl.BlockSpec((1, TM, BRANCH), lambda bi, ri: (bi, ri, 0)),
        out_shape=jax.ShapeDtypeStruct((b, nq * TM, BRANCH), _BF16),
        compiler_params=_cparams(("arbitrary", "arbitrary"), VMEM_LIMIT),
        name="attn_neighbourhood",
    )(q, k, v, bias_tbl)


def _na_tables(rows):
    a = np.arange(4)[:, None, None, None]
    c = np.arange(GRID_W)[None, :, None, None]
    i = np.arange(NA_WIN_ROWS)[None, None, :, None]
    kc = np.arange(GRID_W)[None, None, None, :]
    c0 = np.clip(c - NA_COLS // 2, 0, GRID_W - NA_COLS)
    col_ok = (kc >= c0) & (kc < c0 + NA_COLS)
    cidx = kc - c + (NA_COLS - 1)
    row_ok = [i < NA_ROWS + 0 * a, (i >= a) & (i < a + NA_ROWS), i >= NA_WIN_ROWS - NA_ROWS + 0 * a]
    ridx = [i - a + 7, i - a + 3, i - a - 1]
    shape = (4, GRID_W, NA_WIN_ROWS, GRID_W)
    sel_r, sel_c, ok = [], [], []
    for cs in range(3):
        valid = np.broadcast_to(row_ok[cs] & col_ok, shape)
        ok.append(valid.reshape(4 * GRID_W, NA_WIN_ROWS * GRID_W))
        rsel = (np.broadcast_to(ridx[cs], (4, 1, NA_WIN_ROWS, 1))[:, 0, :, 0][..., None]
                == np.arange(2 * NA_ROWS - 1)[None, None, :])
        sel_r.append(rsel.astype(np.float32))
    csel = (np.broadcast_to(cidx, (1, GRID_W, 1, GRID_W))[0, :, 0, :][..., None]
            == np.arange(2 * NA_COLS - 1)[None, None, :]).astype(np.float32)
    return np.stack(sel_r), csel, np.stack(ok)


def _na_bias_table(rpb, rows):
    sel_r, sel_c, ok = _na_tables(rows)
    hi = lax.Precision.HIGHEST
    t = jnp.einsum("sair,hrg->shaig", jnp.asarray(sel_r), rpb.astype(_F32) * LOG2_E, precision=hi)
    t = jnp.einsum("shaig,ckg->shacik", t, jnp.asarray(sel_c), precision=hi)
    t = t.reshape(3, N_HEADS, 4 * GRID_W, NA_WIN_ROWS * GRID_W)
    t = jnp.where(jnp.asarray(ok)[:, None], t, NEG_INF).astype(_BF16)
    return t.reshape(3, 4, 2 * TM, NA_WIN_ROWS * GRID_W)


def _route_rows(sel, sc):
    gs, gi1, gi2, gs1, gs2 = [], [], [], [], []
    for g in range(N_GROUPS):
        v = sel[4 * g:4 * g + 4]
        s = sc[4 * g:4 * g + 4]
        m1 = jnp.maximum(jnp.maximum(v[0], v[1]), jnp.maximum(v[2], v[3]))
        i1 = jnp.where(v[0] == m1, 0, jnp.where(v[1] == m1, 1, jnp.where(v[2] == m1, 2, 3)))
        rest = [jnp.where(i1 == k, NEG_INF, v[k]) for k in range(4)]
        m2 = jnp.maximum(jnp.maximum(rest[0], rest[1]), jnp.maximum(rest[2], rest[3]))
        i2 = jnp.where(rest[0] == m2, 0, jnp.where(rest[1] == m2, 1, jnp.where(rest[2] == m2, 2, 3)))

        def pick(idx, s=s):
            return jnp.where(idx == 0, s[0], jnp.where(idx == 1, s[1], jnp.where(idx == 2, s[2], s[3])))

        gs.append(m1 + m2)
        gi1.append(i1 + 4 * g)
        gi2.append(i2 + 4 * g)
        gs1.append(pick(i1))
        gs2.append(pick(i2))
    best = jnp.maximum(jnp.maximum(gs[0], gs[1]), jnp.maximum(gs[2], gs[3]))
    gsel = jnp.where(gs[0] == best, 0, jnp.where(gs[1] == best, 1, jnp.where(gs[2] == best, 2, 3)))

    def pickg(vals):
        return jnp.where(gsel == 0, vals[0], jnp.where(gsel == 1, vals[1], jnp.where(gsel == 2, vals[2], vals[3])))

    return pickg(gi1), pickg(gi2), pickg(gs1), pickg(gs2)


def _merge_body(oa_ref, ob_ref, on_ref, gate_ref, x_ref, xctx_ref, wb_ref, wo_ref, g1_ref, sh2_ref, sc2_ref, n2_ref,
                wrt_ref, rb_ref,
                x1_ref, h2_ref, bkt_ref, rank_ref, cnt_ref, carry_ref, *, n_lat_tiles):
    first = (pl.program_id(0) == 0) & (pl.program_id(1) == 0)
    x_res = jnp.where(pl.program_id(1) < n_lat_tiles, x_ref[0], xctx_ref[0])

    @pl.when(first)
    def _():
        carry_ref[...] = jnp.zeros_like(carry_ref)

    outs = (oa_ref, ob_ref, on_ref)
    y = None
    for n in range(3):
        gt = jax.nn.sigmoid(gate_ref[0, :, n * D_MODEL:(n + 1) * D_MODEL].astype(_F32))
        term = gt * _dot(outs[n][0], wb_ref[n])
        y = term if y is None else y + term
    z = _dot(y.astype(_BF16), wo_ref[...])
    x1 = x_res + g1_ref[0] * z
    x1_ref[0] = x1
    ms = jnp.mean(x1 * x1, axis=-1, keepdims=True)
    h2 = (x1 * lax.rsqrt(ms + EPS) * n2_ref[...]) * (1.0 + sc2_ref[0]) + sh2_ref[0]
    h2_ref[0, :, :D_MODEL] = h2

    hb = h2.astype(_BF16)
    hl = (h2 - hb.astype(_F32)).astype(_BF16)
    lt = (_dot(hb, wrt_ref[...]) + _dot(hl, wrt_ref[...])).T
    logits = lt[:N_EXPERTS] + lt[N_EXPERTS:2 * N_EXPERTS]
    scores = jax.nn.sigmoid(logits)
    sel = scores + rb_ref[...]
    sel_rows = [sel[e:e + 1, :] for e in range(N_EXPERTS)]
    sc_rows = [scores[e:e + 1, :] for e in range(N_EXPERTS)]
    e1, e2, s1, s2 = _route_rows(sel_rows, sc_rows)
    den = s1 + s2
    first_lo = e1 < e2
    e_lo = jnp.where(first_lo, e1, e2)
    e_hi = jnp.where(first_lo, e2, e1)
    w_lo = jnp.where(first_lo, s1, s2) / den
    w_hi = jnp.where(first_lo, s2, s1) / den
    a = e_lo & 3
    b = e_hi & 3
    bkt = (e_lo >> 2) * 6 + jnp.where(a == 0, 0, jnp.where(a == 1, 3, 5)) + (b - a - 1)
    bkt_ref[...] = bkt

    bid = lax.broadcasted_iota(jnp.int32, (BUCKET_ROWS, TM), 0)
    oh = bid == bkt
    tri = (lax.broadcasted_iota(jnp.int32, (TM, TM), 0) <= lax.broadcasted_iota(jnp.int32, (TM, TM), 1))
    cum = _dot(oh.astype(_BF16), tri.astype(_BF16))
    carry = carry_ref[:, 0:1]
    rank_ref[...] = jnp.sum(jnp.where(oh, carry + cum - 1.0, 0.0), axis=0, keepdims=True).astype(jnp.int32)
    new_carry = carry + cum[:, TM - 1:TM]
    carry_ref[...] = jnp.broadcast_to(new_carry, carry_ref.shape)
    cnt_ref[...] = jnp.broadcast_to(new_carry, cnt_ref.shape).astype(jnp.int32)

    lane = lax.broadcasted_iota(jnp.int32, (TM, 128), 1)

    def col(row):
        return jnp.concatenate([_col_from_row(row[:, h * 128:(h + 1) * 128]) for h in range(TM // 128)], axis=0)

    h2_ref[0, :, D_MODEL:] = jnp.where(lane == 0, col(w_lo), jnp.where(lane == 1, col(w_hi), 0.0))


def _merge_call(oa, ob, on, gate, x_lat, x_ctx, ctx_tile, wb, wo, g1, sh2, sc2, n2, wrt, rb, t_out,
                n_lat_tiles):
    b, _, d = x_lat.shape
    nb = g1.shape[0] - 1
    nt = t_out // TM
    n_tok = b * t_out

    def mod_idx(bi, i):
        return (jnp.where(i < n_lat_tiles, bi, nb), 0, 0)

    def tok(width):
        return pl.BlockSpec((1, TM, width), lambda bi, i: (bi, i, 0))

    def full(shape):
        return pl.BlockSpec(shape, lambda bi, i: (0,) * len(shape))

    lane_out = pl.BlockSpec((1, TM), lambda bi, i: (0, bi * nt + i))
    return pl.pallas_call(
        functools.partial(_merge_body, n_lat_tiles=n_lat_tiles),
        grid=(b, nt),
        in_specs=[tok(BRANCH), tok(BRANCH), tok(BRANCH), tok(3 * d), *_token_specs(n_lat_tiles, ctx_tile, d),
                  full((3, BRANCH, d)), full((d, d)),
                  pl.BlockSpec((1, 1, d), mod_idx), pl.BlockSpec((1, 1, d), mod_idx),
                  pl.BlockSpec((1, 1, d), mod_idx), full((1, d)),
                  full((d, 128)), full((N_EXPERTS, 1))],
        out_specs=[tok(d), tok(ROW_W), lane_out, lane_out, full((BUCKET_ROWS, 128))],
        out_shape=[jax.ShapeDtypeStruct((b, t_out, d), _F32), jax.ShapeDtypeStruct((b, t_out, ROW_W), _F32),
                   jax.ShapeDtypeStruct((1, n_tok), jnp.int32), jax.ShapeDtypeStruct((1, n_tok), jnp.int32),
                   jax.ShapeDtypeStruct((BUCKET_ROWS, 128), jnp.int32)],
        scratch_shapes=[pltpu.VMEM((BUCKET_ROWS, 128), _F32)],
        compiler_params=_cparams(("arbitrary", "arbitrary"), VMEM_LIMIT),
        name="merge_route",
    )(oa, ob, on, gate, x_lat, x_ctx, wb, wo, g1, sh2, sc2, n2, wrt, rb)


ROW_DMA_UNROLL = 8


def _dispatch_body(pad_start_ref, pad_n_ref, nv_ref, pos_ref, h_ref, xs_ref, zbuf, sem, zsem):
    @pl.when(pl.program_id(0) == 0)
    def _():
        zbuf[...] = jnp.zeros_like(zbuf)

        def zero_row(e, r):
            return pltpu.make_async_copy(zbuf.at[pl.ds(0, 1)], xs_ref.at[pl.ds(pad_start_ref[e] + r, 1)], zsem)

        def zero_tile(i):
            return pltpu.make_async_copy(zbuf, xs_ref.at[pl.ds(pl.multiple_of(i * TM_E, TM_E), TM_E)], zsem)

        n_tiles = xs_ref.shape[0] // TM_E
        for e in range(N_BUCKETS):
            lax.fori_loop(0, pad_n_ref[e], lambda r, c, e=e: (zero_row(e, r).start(), c)[1], 0)
        lax.fori_loop(nv_ref[0], n_tiles, lambda i, c: (zero_tile(i).start(), c)[1], 0)
        for e in range(N_BUCKETS):
            lax.fori_loop(0, pad_n_ref[e], lambda r, c, e=e: (zero_row(e, r).wait(), c)[1], 0)
        lax.fori_loop(nv_ref[0], n_tiles, lambda i, c: (zero_tile(i).wait(), c)[1], 0)

    def row_copy(t):
        return pltpu.make_async_copy(h_ref.at[pl.ds(t, 1)], xs_ref.at[pl.ds(pos_ref[0, 0, t], 1)], sem)

    rows = h_ref.shape[0]
    def issue_pair(t2, c):
        row_copy(2 * t2).start(priority=0)
        row_copy(2 * t2 + 1).start(priority=1)
        return c

    lax.fori_loop(0, rows // 2, issue_pair, 0, unroll=ROW_DMA_UNROLL // 2)
    lax.fori_loop(0, rows, lambda t, c: (row_copy(t).wait(), c)[1], 0, unroll=ROW_DMA_UNROLL)


def _dispatch_call(pad_start, pad_n, n_valid, pos, h2, p_rows):
    n_tok, d = h2.shape
    rows = TM_D if n_tok % TM_D == 0 else TM
    n_tiles = n_tok // rows
    pos3 = pos.reshape(n_tiles, 1, rows)
    return pl.pallas_call(
        _dispatch_body,
        grid_spec=pltpu.PrefetchScalarGridSpec(
            num_scalar_prefetch=3,
            grid=(n_tiles,),
            in_specs=[pl.BlockSpec((1, 1, rows), lambda i, ps, pn, nv: (i, 0, 0), memory_space=pltpu.SMEM),
                      pl.BlockSpec((rows, d), lambda i, ps, pn, nv: (i, 0))],
            out_specs=pl.BlockSpec(memory_space=pl.ANY),
            scratch_shapes=[pltpu.VMEM((TM_E, d), h2.dtype), pltpu.SemaphoreType.DMA, pltpu.SemaphoreType.DMA],
        ),
        out_shape=jax.ShapeDtypeStruct((p_rows, d), h2.dtype),
        compiler_params=_cparams(("arbitrary",)),
        name="moe_dispatch",
    )(pad_start, pad_n, n_valid, pos3, h2)


def _moe_body(elo_ref, ehi_ref, nv_ref, xs_ref, wg_lo, wu_lo, wd_lo, wg_hi, wu_hi, wd_hi, ys_ref):
    del elo_ref, ehi_ref
    live = pl.program_id(0) < nv_ref[0]

    @pl.when(live)
    def _():
        xb = xs_ref[:, :D_MODEL].astype(_BF16)
        wts = xs_ref[:, D_MODEL:]

        def expert(wg_ref, wu_ref, wd_ref):
            gate = _dot(xb, wg_ref[0].astype(_BF16))
            up = _dot(xb, wu_ref[0].astype(_BF16))
            act = (gate * jax.nn.sigmoid(gate)) * up
            return _dot(act.astype(_BF16), wd_ref[0].astype(_BF16))

        ys_ref[...] = wts[:, 0:1] * expert(wg_lo, wu_lo, wd_lo) + wts[:, 1:2] * expert(wg_hi, wu_hi, wd_hi)

    @pl.when(jnp.logical_not(live))
    def _():
        ys_ref[...] = jnp.zeros_like(ys_ref)


def _moe_call(tile_elo, tile_ehi, n_valid, xs, wg, wu, wd, layer):
    p_rows = xs.shape[0]
    d = D_MODEL
    n_tiles = p_rows // TM_E

    def row_idx(i, elo, ehi, nv):
        return (jnp.minimum(i, nv[0] - 1), 0)

    def lo_idx(i, elo, ehi, nv):
        return (layer, elo[i], 0, 0)

    def hi_idx(i, elo, ehi, nv):
        return (layer, ehi[i], 0, 0)

    def w_specs(idx):
        return [pl.BlockSpec((None, 1, d, D_FF), idx), pl.BlockSpec((None, 1, d, D_FF), idx),
                pl.BlockSpec((None, 1, D_FF, d), idx)]

    return pl.pallas_call(
        _moe_body,
        grid_spec=pltpu.PrefetchScalarGridSpec(
            num_scalar_prefetch=3,
            grid=(n_tiles,),
            in_specs=[pl.BlockSpec((TM_E, ROW_W), row_idx), *w_specs(lo_idx), *w_specs(hi_idx)],
            out_specs=pl.BlockSpec((TM_E, d), lambda i, elo, ehi, nv: (i, 0)),
        ),
        out_shape=jax.ShapeDtypeStruct((p_rows, d), _F32),
        compiler_params=_cparams(("arbitrary",), VMEM_LIMIT),
        name="moe_experts",
    )(tile_elo, tile_ehi, n_valid, xs, wg, wu, wd, wg, wu, wd)


def _combine_body(pos0_ref, posn_ref, x1_ref, g2_ref, fn_ref, ys_ref, o_ref, gbuf, sems):
    step = pl.program_id(0)
    y, issue_next, finish = _moe_rows_pipeline(step, pl.num_programs(0) - 1, pos0_ref, posn_ref, ys_ref,
                                               gbuf, sems)
    x2 = x1_ref[...] + g2_ref[0] * y
    issue_next()
    ms = jnp.mean(x2 * x2, axis=-1, keepdims=True)
    o_ref[...] = x2 * lax.rsqrt(ms + EPS) * fn_ref[...]
    finish()


def _combine_call(pos, x1, g2, fnorm, ys, t_out):
    n_tok, d = x1.shape
    rows = TM_C if t_out % TM_C == 0 else TM
    nt = t_out // rows
    n_tiles = n_tok // rows
    pos3 = pos.reshape(n_tiles, 1, rows)
    smem_tile = functools.partial(pl.BlockSpec, (1, 1, rows), memory_space=pltpu.SMEM)
    return pl.pallas_call(
        _combine_body,
        grid=(n_tiles,),
        in_specs=[smem_tile(lambda i: (0, 0, 0)),
                  smem_tile(lambda i: (jnp.minimum(i + 1, n_tiles - 1), 0, 0)),
                  pl.BlockSpec((rows, d), lambda i: (i, 0)),
                  pl.BlockSpec((1, 1, d), lambda i: (i // nt, 0, 0)),
                  pl.BlockSpec((1, d), lambda i: (0, 0)),
                  pl.BlockSpec(memory_space=pl.ANY)],
        out_specs=pl.BlockSpec((rows, d), lambda i: (i, 0)),
        out_shape=jax.ShapeDtypeStruct((n_tok, d), _F32),
        scratch_shapes=[pltpu.VMEM((2, rows, d), _F32), pltpu.SemaphoreType.DMA((2,))],
        compiler_params=_cparams(("arbitrary",)),
        name="moe_combine",
    )(pos3, pos3, x1, g2, fnorm, ys)


def _rope_tables(s_len, l_ctx):
    t = jnp.arange(s_len, dtype=jnp.int32)
    row = (t // GRID_W).astype(_F32)
    col = (t % GRID_W).astype(_F32)
    half = HEAD_DIM // 2
    inv = ROPE_THETA ** (-jnp.arange(0, half, 2, dtype=_F32) / half)
    ar = row[:, None] * inv
    ac = col[:, None] * inv
    cos64 = jnp.concatenate([jnp.cos(ar), jnp.cos(ar), jnp.cos(ac), jnp.cos(ac)], axis=1)
    sin64 = jnp.concatenate([-jnp.sin(ar), jnp.sin(ar), -jnp.sin(ac), jnp.sin(ac)], axis=1)
    cos_t = jnp.concatenate([jnp.tile(cos64, (1, 2)), jnp.ones((l_ctx, 128), _F32)], axis=0)
    sin_t = jnp.concatenate([jnp.tile(sin64, (1, 2)), jnp.zeros((l_ctx, 128), _F32)], axis=0)
    return cos_t, sin_t


def _block_diag_ones():
    r = np.arange(BRANCH)
    return jnp.asarray((r[:, None] // HEAD_DIM) == (r[None, :] // HEAD_DIM), dtype=_BF16)


def kernel(x, c, ctx, c_ctx, w_mod, b_mod, norm1, norm2, w_in, q_norm_a, k_norm_a, sink_b, rpb_n, w_branch, w_out,
           w_router, router_bias, w_e_gate, w_e_up, w_e_down, final_norm):
    b, s_len, d = x.shape
    l_ctx = ctx.shape[1]
    depth = w_in.shape[0]
    t_all = s_len + l_ctx
    rows = s_len // GRID_W
    assert d == D_MODEL and l_ctx == TM and s_len % TM == 0 and b + 1 <= 16
    assert s_len >= TM + 2 * WINDOW and rows >= NA_WIN_ROWS and rows % 4 == 0
    n_lat_tiles = s_len // TM

    cos_t, sin_t = _rope_tables(s_len, l_ctx)
    bd = _block_diag_ones()
    cvec = jnp.zeros((16, d), _F32).at[:b].set(c).at[b].set(c_ctx)
    mods = _mod_call(cvec, w_mod, b_mod)[:, :b + 1].reshape(depth, b + 1, 6, 1, d)

    wr = w_router.astype(_F32)
    wr_hi = wr.astype(_BF16)
    wr_lo = (wr - wr_hi.astype(_F32)).astype(_BF16)
    wrt = jnp.concatenate([wr_hi, wr_lo, jnp.zeros((d, 128 - 2 * N_EXPERTS), _BF16)], axis=1)
    rb = router_bias.astype(_F32).reshape(N_EXPERTS, 1)

    x_lat, x_ctx, ctx_tile = x, ctx, 0
    pending = None
    for l in range(depth):
        last = l == depth - 1
        sh1, sc1, g1, sh2, sc2, g2 = [mods[l, :, n] for n in range(6)]
        w_f = w_in[l]
        w_l = jnp.concatenate([_pair_heads(w_f[:, OFF_QA:OFF_KA], 1), w_f[:, OFF_KA:OFF_QB],
                               _pair_heads(w_f[:, OFF_QB:OFF_KB], 1), w_f[:, OFF_KB:]], axis=1).astype(_BF16)
        gq = (jnp.tile(q_norm_a[l].astype(_F32), N_HEADS) * Q_SCALE).reshape(1, BRANCH)
        gk = jnp.tile(k_norm_a[l].astype(_F32), 2).reshape(1, KV_A)
        outs = _inproj_call(x_lat, x_ctx, ctx_tile, n_lat_tiles, norm1[l].reshape(1, d), sh1, sc1, w_l, cos_t, sin_t,
                            gq, gk, bd, combine=pending)
        qa, ka, va, qb, kb, vb, qn, kn, vn, gate = outs[:10]
        if pending is not None:
            x_lat = x_ctx = outs[10]

        sink = sink_b[l].astype(_F32) * LOG2_E
        oa = _attn_a_call(qa, ka, va, s_len, not last)
        ob = _attn_b_call(qb, kb, vb, sink, s_len, not last)
        on = _attn_n_call(qn, kn, vn, _na_bias_table(rpb_n[l], rows), s_len, not last)

        t_out = s_len if last else t_all
        wb = jnp.stack([_pair_heads(w_branch[l, 0], 0), _pair_heads(w_branch[l, 1], 0),
                        w_branch[l, 2]]).astype(_BF16)
        x1, h2w, bkt, rank, cnt = _merge_call(
            oa, ob, on, gate, x_lat, x_ctx, ctx_tile, wb, w_out[l].astype(_BF16), g1, sh2, sc2,
            norm2[l].reshape(1, d), wrt, rb, t_out, n_lat_tiles)

        n_tok = b * t_out
        counts = cnt[:N_BUCKETS, 0]
        padded = ((counts + TM_E - 1) // TM_E) * TM_E
        ends = jnp.cumsum(padded)
        offs = ends - padded
        pos = rank[0] + jnp.sum(jnp.where(bkt == jnp.arange(N_BUCKETS)[:, None], offs[:, None], 0), axis=0)
        pos3 = pos.reshape(n_tok // TM, 1, TM)
        p_rows = n_tok + N_BUCKETS * TM_E
        n_tiles = p_rows // TM_E
        n_valid = (ends[-1] // TM_E).astype(jnp.int32).reshape(1)
        tile_start = jnp.arange(n_tiles, dtype=jnp.int32) * TM_E
        tile_bucket = jnp.minimum(jnp.sum(tile_start[:, None] >= ends[None, :], axis=1), N_BUCKETS - 1)
        tile_group = (tile_bucket // 6) * EXPERTS_PER_GROUP
        tile_elo = (tile_group + jnp.asarray(_PAIR_LO)[tile_bucket % 6]).astype(jnp.int32)
        tile_ehi = (tile_group + jnp.asarray(_PAIR_HI)[tile_bucket % 6]).astype(jnp.int32)

        pad_start = (offs + counts).astype(jnp.int32)
        pad_n = (padded - counts).astype(jnp.int32)
        xs = _dispatch_call(pad_start, pad_n, n_valid, pos, h2w.reshape(n_tok, ROW_W), p_rows)
        ys = _moe_call(tile_elo, tile_ehi, n_valid, xs, w_e_gate, w_e_up, w_e_down, l)
        if last:
            out = _combine_call(pos, x1.reshape(n_tok, d), g2, final_norm.reshape(1, d), ys, t_out)
            return out.reshape(b, t_out, d)
        pending = (pos3, g2, ys)
        x_lat = x_ctx = x1
        ctx_tile = n_lat_tiles
```
